```python
import jax, jax.numpy as jnp
from jax import lax
import numpy as np

D_MODEL = 2048
BATCH = 2
SEQ = 16384
DEPTH = 2

GRID_W = 64
CTX_LEN = 256
EPS = 1e-6
NEG_INF = -1e30

GROUP_W = D_MODEL // 4
D_MIX = 4 * GROUP_W
RWKV_HEAD = 64
RWKV_HEADS = GROUP_W // RWKV_HEAD
DECAY_LORA = 64
ICLR_LORA = 64
GATE_LORA = 128
VRES_LORA = 32
RWKV_GN_EPS = 64e-5
CONV_W = 3
FOURIER_GROUPS = 4
FOURIER_CH = GROUP_W // FOURIER_GROUPS
GMLP_GROUPS = 4
GMLP_CH = GROUP_W // GMLP_GROUPS
GMLP_CHUNK = 128
ATT_HEAD = 64
ATT_Q_HEADS = GROUP_W // ATT_HEAD
ATT_KV_HEADS = 2
ATT_REP = ATT_Q_HEADS // ATT_KV_HEADS
WINDOW = 128
ATT_BLOCK = 128
ROPE_BASE = 10000.0
ROPE_AXIS = ATT_HEAD // 2
N_EXPERTS = 64
N_EXPERT_GROUPS = 8
EXPERTS_PER_GROUP = N_EXPERTS // N_EXPERT_GROUPS
TOP_K = 2
D_EXPERT = 512
MOE_BLOCK = 128

A_COLS = 3 * GROUP_W + 2 * DECAY_LORA + 2 * ICLR_LORA + GATE_LORA
B_COLS = GROUP_W
C_COLS = 2 * GROUP_W
D_COLS = GROUP_W + 2 * ATT_KV_HEADS * ATT_HEAD
D_IN = A_COLS + B_COLS + C_COLS + D_COLS

kernel_name = "hybrid_rwkv_fourier_gmlp_swa_moe_dit"


def split_cols(p, sizes):
    offs = [int(o) for o in np.cumsum(sizes)[:-1]]
    return jnp.split(p, offs, axis=-1)


def rmsnorm(x, g):
    xf = x.astype(jnp.float32)
    y = xf * lax.rsqrt(jnp.mean(xf * xf, -1, keepdims=True) + EPS)
    return (y * g.astype(jnp.float32)).astype(x.dtype)


def modulate(h, shift, scale):
    return h * (1 + scale) + shift


def short_conv(x, w):
    L = x.shape[1]
    half = CONV_W // 2
    xp = jnp.pad(x, ((0, 0), (half, half), (0, 0)))
    out = xp[:, 0:L] * w[0]
    for j in range(1, CONV_W):
        out = out + xp[:, j:j + L] * w[j]
    return out


def axial_rope(t):
    L = t.shape[1]
    n_rows = L // GRID_W
    row = jnp.repeat(jnp.arange(n_rows), GRID_W).astype(jnp.float32)
    col = jnp.tile(jnp.arange(GRID_W), n_rows).astype(jnp.float32)
    half = ROPE_AXIS // 2
    inv_freq = ROPE_BASE ** (-jnp.arange(half, dtype=jnp.float32) / half)
    tf = t.astype(jnp.float32)

    def rot(u, pos):
        ang = pos[:, None] * inv_freq[None, :]
        cos = jnp.cos(ang)[None, :, None, :]
        sin = jnp.sin(ang)[None, :, None, :]
        u1, u2 = u[..., :half], u[..., half:]
        return jnp.concatenate([u1 * cos - u2 * sin, u1 * sin + u2 * cos], -1)

    out = jnp.concatenate([rot(tf[..., :ROPE_AXIS], row), rot(tf[..., ROPE_AXIS:], col)], -1)
    return out.astype(t.dtype)


def rwkv_prepare(pa, conv_w, w0, w_up, a0, a_up, g_up, k_k, k_a, v_first, vres):
    Bn, L, _ = pa.shape
    pa = short_conv(pa, conv_w).astype(jnp.float32)
    r, k, v, wl, al, gl = split_cols(pa, [GROUP_W, GROUP_W, GROUP_W, 2 * DECAY_LORA, 2 * ICLR_LORA, GATE_LORA])
    if vres is not None:
        v0, v_down, v_up = vres
        v = v + (v_first - v) * jax.nn.sigmoid(v0 + (v @ v_down) @ v_up)
    heads = lambda t: t.reshape(Bn, L, RWKV_HEADS, RWKV_HEAD)
    kk = heads(k * k_k)
    kk = kk / jnp.maximum(jnp.sqrt(jnp.sum(kk * kk, -1, keepdims=True)), 1e-12)
    per_dir = []
    for d in range(2):
        w_raw = w0[d] + jnp.tanh(wl[..., d * DECAY_LORA:(d + 1) * DECAY_LORA]) @ w_up[d]
        decay = jnp.exp(-jnp.exp(-jax.nn.softplus(-w_raw) - 0.5))
        a = jax.nn.sigmoid(a0[d] + al[..., d * ICLR_LORA:(d + 1) * ICLR_LORA] @ a_up[d])
        per_dir.append((heads(k * (1.0 + (a - 1.0) * k_a)), heads(decay), kk * heads(a)))
    g = jax.nn.sigmoid(gl) @ g_up
    return heads(r), kk, heads(v), v, per_dir, g


def wkv_scan(s0, r, k, v, decay, kk, kka, reverse):
    xs = tuple(jnp.swapaxes(t, 0, 1) for t in (r, k, v, decay, kk, kka))

    def step(S, inp):
        r_t, k_t, v_t, w_t, kk_t, kka_t = inp
        sk = jnp.einsum('bhvk,bhk->bhv', S, kk_t)
        S = S * w_t[:, :, None, :] - sk[..., None] * kka_t[:, :, None, :] + v_t[..., None] * k_t[:, :, None, :]
        return S, jnp.einsum('bhvk,bhk->bhv', S, r_t)

    s_fin, ys = lax.scan(step, s0, xs, reverse=reverse)
    return s_fin, jnp.swapaxes(ys, 0, 1)


def rwkv_readout(y, r, k, v, g, r_k, ln_w, ln_b):
    Bn, L = y.shape[:2]
    mu = jnp.mean(y, -1, keepdims=True)
    var = jnp.mean(jnp.square(y - mu), -1, keepdims=True)
    yn = ((y - mu) * lax.rsqrt(var + RWKV_GN_EPS)).reshape(Bn, L, GROUP_W) * ln_w + ln_b
    bonus = (jnp.sum(r * k * r_k, -1, keepdims=True) * v).reshape(Bn, L, GROUP_W)
    return (yn + bonus) * g


def rwkv_mixer(pa_x, pa_c, conv_w, w0, w_up, a0, a_up, g_up, k_k, k_a, r_k, ln_w, ln_b,
               vf_x, vf_c, vres, need_ctx):
    common = (conv_w, w0, w_up, a0, a_up, g_up, k_k, k_a)
    rx, kkx, vx, vflat_x, dirs_x, gx = rwkv_prepare(pa_x, *common, vf_x, vres)
    rc, kkc, vc, vflat_c, dirs_c, gc = rwkv_prepare(pa_c, *common, vf_c, vres)
    s0 = jnp.zeros((pa_x.shape[0], RWKV_HEADS, RWKV_HEAD, RWKV_HEAD), jnp.float32)
    y_x = jnp.zeros_like(rx)
    y_c = jnp.zeros_like(rc)
    for d, reverse in enumerate((False, True)):
        kxd, wxd, kkaxd = dirs_x[d]
        kcd, wcd, kkacd = dirs_c[d]
        s_ctx, yc_d = wkv_scan(s0, rc, kcd, vc, wcd, kkc, kkacd, reverse)
        _, yx_d = wkv_scan(s_ctx, rx, kxd, vx, wxd, kkx, kkaxd, reverse)
        y_x = y_x + yx_d
        if need_ctx:
            y_c = y_c + yc_d
    out_x = rwkv_readout(y_x, rx, 0.5 * (dirs_x[0][0] + dirs_x[1][0]), vx, gx, r_k, ln_w, ln_b)
    out_c = None
    if need_ctx:
        out_c = rwkv_readout(y_c, rc, 0.5 * (dirs_c[0][0] + dirs_c[1][0]), vc, gc, r_k, ln_w, ln_b)
    return out_x, out_c, vflat_x, vflat_c


def fourier_mix(pb, w, b):
    Bn, L, _ = pb.shape
    f = pb.reshape(Bn, L, FOURIER_GROUPS, FOURIER_CH).astype(jnp.float32)
    F = jnp.fft.fftn(f, axes=(1, 3), norm='ortho').real
    y = jnp.einsum('blgc,gcd->blgd', F, w.astype(jnp.float32)) + b.astype(jnp.float32)
    return y.reshape(Bn, L, GROUP_W)


def gmlp_mix(pc, norm_g, ws, bs):
    Bn, L, _ = pc.shape
    z = jax.nn.gelu(pc)
    u, v = jnp.split(z, 2, axis=-1)
    v = rmsnorm(v.reshape(Bn, L, GMLP_GROUPS, GMLP_CH), norm_g)
    v = v.reshape(Bn, L // GMLP_CHUNK, GMLP_CHUNK, GMLP_GROUPS, GMLP_CH)
    f = jnp.einsum('gpq,bnqgc->bnpgc', ws, v) + bs.T[None, None, :, :, None]
    return (u.reshape(Bn, L, GMLP_GROUPS, GMLP_CH) * f.reshape(Bn, L, GMLP_GROUPS, GMLP_CH)).reshape(Bn, L, GROUP_W)


def window_attention(pd_x, pd_c, q_gain, k_gain, sink, need_ctx):
    Bn, L, _ = pd_x.shape
    C = pd_c.shape[1]
    kvw = ATT_KV_HEADS * ATT_HEAD
    qx, kx, vx = split_cols(pd_x, [GROUP_W, kvw, kvw])
    qc, kc, vc = split_cols(pd_c, [GROUP_W, kvw, kvw])
    qx = axial_rope(rmsnorm(qx.reshape(Bn, L, ATT_Q_HEADS, ATT_HEAD), q_gain))
    kx = axial_rope(rmsnorm(kx.reshape(Bn, L, ATT_KV_HEADS, ATT_HEAD), k_gain))
    vx = vx.reshape(Bn, L, ATT_KV_HEADS, ATT_HEAD)
    qc = rmsnorm(qc.reshape(Bn, C, ATT_Q_HEADS, ATT_HEAD), q_gain)
    kc = rmsnorm(kc.reshape(Bn, C, ATT_KV_HEADS, ATT_HEAD), k_gain)
    vc = vc.reshape(Bn, C, ATT_KV_HEADS, ATT_HEAD)
    scale = ATT_HEAD ** -0.5
    sink_l = sink.astype(jnp.float32).reshape(ATT_KV_HEADS, ATT_REP)[None, :, :, None, None]
    nb = L // ATT_BLOCK
    span = ATT_BLOCK + 2 * WINDOW
    kpad = jnp.pad(kx, ((0, 0), (WINDOW, WINDOW), (0, 0), (0, 0)))
    vpad = jnp.pad(vx, ((0, 0), (WINDOW, WINDOW), (0, 0), (0, 0)))
    qb = jnp.swapaxes(qx.reshape(Bn, nb, ATT_BLOCK, ATT_KV_HEADS, ATT_REP, ATT_HEAD), 0, 1)
    offs_q = jnp.arange(ATT_BLOCK)
    offs_k = jnp.arange(span) - WINDOW

    def attend(args):
        n, q_n = args
        start = n * ATT_BLOCK
        k_n = lax.dynamic_slice_in_dim(kpad, start, span, axis=1)
        v_n = lax.dynamic_slice_in_dim(vpad, start, span, axis=1)
        s = jnp.einsum('bqgrd,bkgd->bgrqk', q_n, k_n, preferred_element_type=jnp.float32) * scale
        qpos = start + offs_q
        kpos = start + offs_k
        valid = (jnp.abs(kpos[None, :] - qpos[:, None]) <= WINDOW) & (kpos >= 0)[None, :] & (kpos < L)[None, :]
        s = jnp.where(valid, s, NEG_INF)
        sc = jnp.einsum('bqgrd,bkgd->bgrqk', q_n, kc, preferred_element_type=jnp.float32) * scale
        m = jnp.maximum(jnp.maximum(s.max(-1, keepdims=True), sc.max(-1, keepdims=True)), sink_l)
        p = jnp.exp(s - m)
        pc = jnp.exp(sc - m)
        denom = p.sum(-1, keepdims=True) + pc.sum(-1, keepdims=True) + jnp.exp(sink_l - m)
        p = p / denom
        pc = pc / denom
        return (jnp.einsum('bgrqk,bkgd->bqgrd', p, v_n.astype(jnp.float32))
                + jnp.einsum('bgrqk,bkgd->bqgrd', pc, vc.astype(jnp.float32)))

    out = lax.map(attend, (jnp.arange(nb), qb))
    out_x = jnp.swapaxes(out, 0, 1).reshape(Bn, L, GROUP_W)
    out_c = None
    if need_ctx:
        qcg = qc.reshape(Bn, C, ATT_KV_HEADS, ATT_REP, ATT_HEAD)
        s = jnp.einsum('bqgrd,bkgd->bgrqk', qcg, kc, preferred_element_type=jnp.float32) * scale
        logits = jnp.concatenate([s, jnp.broadcast_to(sink_l, s.shape[:-1] + (1,))], -1)
        p = jax.nn.softmax(logits, axis=-1)[..., :-1]
        out_c = jnp.einsum('bgrqk,bkgd->bqgrd', p, vc.astype(jnp.float32)).reshape(Bn, C, GROUP_W)
    return out_x, out_c


def moe_ffn(tok, w_router, router_bias, wg, wu, wd):
    T, D = tok.shape
    scores = jax.nn.sigmoid(jnp.dot(tok, w_router, preferred_element_type=jnp.float32))
    sel = (scores + router_bias.astype(jnp.float32)).reshape(T, N_EXPERT_GROUPS, EXPERTS_PER_GROUP)
    group_score = lax.top_k(sel, 2)[0].sum(-1)
    best = jnp.argmax(group_score, -1)
    sel_in = jnp.take_along_axis(sel, best[:, None, None], axis=1)[:, 0]
    idx_in = lax.top_k(sel_in, TOP_K)[1]
    eidx = best[:, None] * EXPERTS_PER_GROUP + idx_in
    gw = jnp.take_along_axis(scores, eidx, axis=1)
    gw = gw / gw.sum(-1, keepdims=True)
    n_assign = T * TOP_K
    flat_e = eidx.reshape(-1)
    flat_t = (jnp.arange(n_assign) // TOP_K).astype(jnp.int32)
    order = jnp.argsort(flat_e)
    se = flat_e[order]
    counts = jnp.bincount(flat_e, length=N_EXPERTS)
    padded = (counts + MOE_BLOCK - 1) // MOE_BLOCK * MOE_BLOCK
    pad_end = jnp.cumsum(padded)
    pad_start = pad_end - padded
    start = jnp.cumsum(counts) - counts
    dest = pad_start[se] + jnp.arange(n_assign) - start[se]
    n_blocks = -(-n_assign // MOE_BLOCK) + N_EXPERTS
    row_tok = jnp.full((n_blocks * MOE_BLOCK,), T, jnp.int32).at[dest].set(flat_t[order])
    row_w = jnp.zeros((n_blocks * MOE_BLOCK,), jnp.float32).at[dest].set(gw.reshape(-1)[order])
    block_e = jnp.minimum(jnp.searchsorted(pad_end, jnp.arange(n_blocks) * MOE_BLOCK, side='right'), N_EXPERTS - 1)
    tok_pad = jnp.concatenate([tok, jnp.zeros((1, D), tok.dtype)], 0)

    def body(acc, blk):
        rows, wts, e = blk
        xb = tok_pad[rows]
        hb = jax.nn.silu(xb @ wg[e]) * (xb @ wu[e])
        yb = (hb @ wd[e]) * wts[:, None].astype(xb.dtype)
        return acc.at[rows].add(yb.astype(acc.dtype)), None

    acc, _ = lax.scan(body, jnp.zeros((T + 1, D), tok.dtype),
                      (row_tok.reshape(n_blocks, MOE_BLOCK), row_w.reshape(n_blocks, MOE_BLOCK), block_e))
    return acc[:T]


def setup_inputs(seed: int = 0) -> dict:
    key = jax.random.key(seed)
    ks = iter(list(jax.random.split(key, 48)))
    D = D_MODEL
    L_ = DEPTH
    nrm = lambda shape, s: jax.random.normal(next(ks), shape, jnp.float32) * s
    gain = lambda shape: 1.0 + nrm(shape, 0.05)
    conv_base = jnp.array([0.2, 0.6, 0.2], jnp.float32)[None, :, None]
    return {
        "x": nrm((BATCH, SEQ, D), 1.0),
        "c": nrm((BATCH, D), 1.0),
        "ctx": nrm((BATCH, CTX_LEN, D), 1.0),
        "c_ctx": nrm((D,), 1.0),
        "norm1": gain((L_, D)),
        "norm2": gain((L_, D)),
        "w_ada": nrm((L_, D, 6 * D), 0.5 * D ** -0.5),
        "b_ada": nrm((L_, 6 * D), 0.02),
        "w_in": nrm((L_, D, D_IN), D ** -0.5),
        "w_out": nrm((L_, D_MIX, D), D_MIX ** -0.5),
        "rwkv_conv": conv_base + nrm((L_, CONV_W, A_COLS), 0.05),
        "rwkv_w0": jax.random.uniform(next(ks), (L_, 2, GROUP_W), jnp.float32, -5.0, 1.0),
        "rwkv_w_up": nrm((L_, 2, DECAY_LORA, GROUP_W), 0.1),
        "rwkv_a0": nrm((L_, 2, GROUP_W), 0.5),
        "rwkv_a_up": nrm((L_, 2, ICLR_LORA, GROUP_W), 0.1),
        "rwkv_g_up": nrm((L_, GATE_LORA, GROUP_W), GATE_LORA ** -0.5),
        "rwkv_k_k": 0.85 + nrm((L_, GROUP_W), 0.05),
        "rwkv_k_a": gain((L_, GROUP_W)),
        "rwkv_r_k": nrm((L_, RWKV_HEADS, RWKV_HEAD), 0.1),
        "rwkv_ln_w": gain((L_, GROUP_W)),
        "rwkv_ln_b": nrm((L_, GROUP_W), 0.02),
        "rwkv_v0": nrm((L_ - 1, GROUP_W), 0.5),
        "rwkv_v_down": nrm((L_ - 1, GROUP_W, VRES_LORA), GROUP_W ** -0.5),
        "rwkv_v_up": nrm((L_ - 1, VRES_LORA, GROUP_W), 0.1),
        "fourier_w": nrm((L_, FOURIER_GROUPS, FOURIER_CH, FOURIER_CH), FOURIER_CH ** -0.5),
        "fourier_b": nrm((L_, FOURIER_GROUPS, FOURIER_CH), 0.02),
        "gmlp_norm": gain((L_, GMLP_GROUPS, GMLP_CH)),
        "gmlp_ws": nrm((L_, GMLP_GROUPS, GMLP_CHUNK, GMLP_CHUNK), GMLP_CHUNK ** -0.5),
        "gmlp_bs": 1.0 + nrm((L_, GMLP_GROUPS, GMLP_CHUNK), 0.02),
        "attn_q_norm": gain((L_, ATT_HEAD)),
        "attn_k_norm": gain((L_, ATT_HEAD)),
        "attn_sink": nrm((L_, ATT_Q_HEADS), 0.5),
        "w_router": nrm((D, N_EXPERTS), D ** -0.5),
        "router_bias": nrm((N_EXPERTS,), 0.01),
        "w_e_gate": nrm((L_, N_EXPERTS, D, D_EXPERT), D ** -0.5),
        "w_e_up": nrm((L_, N_EXPERTS, D, D_EXPERT), D ** -0.5),
        "w_e_down": nrm((L_, N_EXPERTS, D_EXPERT, D), D_EXPERT ** -0.5),
    }


def reference(x, c, ctx, c_ctx, norm1, norm2, w_ada, b_ada, w_in, w_out,
              rwkv_conv, rwkv_w0, rwkv_w_up, rwkv_a0, rwkv_a_up, rwkv_g_up, rwkv_k_k, rwkv_k_a,
              rwkv_r_k, rwkv_ln_w, rwkv_ln_b, rwkv_v0, rwkv_v_down, rwkv_v_up,
              fourier_w, fourier_b, gmlp_norm, gmlp_ws, gmlp_bs,
              attn_q_norm, attn_k_norm, attn_sink,
              w_router, router_bias, w_e_gate, w_e_up, w_e_down):
    Bn, L, D = x.shape
    C = ctx.shape[1]
    h_ctx = ctx
    silu_c = jax.nn.silu(c)
    silu_cc = jax.nn.silu(c_ctx)
    vf_x = None
    vf_c = None
    for l in range(DEPTH):
        need_ctx = l < DEPTH - 1
        mod = silu_c @ w_ada[l] + b_ada[l]
        mod_c = silu_cc @ w_ada[l] + b_ada[l]
        sh1, sc1, g1, sh2, sc2, g2 = jnp.split(mod[:, None, :], 6, axis=-1)
        sh1c, sc1c, g1c, sh2c, sc2c, g2c = jnp.split(mod_c, 6, axis=-1)
        hx = modulate(rmsnorm(x, norm1[l]), sh1, sc1)
        hc = modulate(rmsnorm(h_ctx, norm1[l]), sh1c, sc1c)
        px = hx @ w_in[l]
        pc = hc @ w_in[l]
        pa_x, pb_x, pc_x, pd_x = split_cols(px, [A_COLS, B_COLS, C_COLS, D_COLS])
        pa_c, pb_c, pc_c, pd_c = split_cols(pc, [A_COLS, B_COLS, C_COLS, D_COLS])
        vres = None if l == 0 else (rwkv_v0[l - 1], rwkv_v_down[l - 1], rwkv_v_up[l - 1])
        ya_x, ya_c, v_x, v_c = rwkv_mixer(pa_x, pa_c, rwkv_conv[l], rwkv_w0[l], rwkv_w_up[l], rwkv_a0[l],
                                          rwkv_a_up[l], rwkv_g_up[l], rwkv_k_k[l], rwkv_k_a[l], rwkv_r_k[l],
                                          rwkv_ln_w[l], rwkv_ln_b[l], vf_x, vf_c, vres, need_ctx)
        if l == 0:
            vf_x, vf_c = v_x, v_c
        yb_x = fourier_mix(pb_x, fourier_w[l], fourier_b[l])
        yc_x = gmlp_mix(pc_x, gmlp_norm[l], gmlp_ws[l], gmlp_bs[l])
        yd_x, yd_c = window_attention(pd_x, pd_c, attn_q_norm[l], attn_k_norm[l], attn_sink[l], need_ctx)
        mix_x = jnp.concatenate([ya_x, yb_x, yc_x, yd_x], -1).astype(x.dtype)
        x = x + g1 * (mix_x @ w_out[l])
        if need_ctx:
            yb_c = fourier_mix(pb_c, fourier_w[l], fourier_b[l])
            yc_c = gmlp_mix(pc_c, gmlp_norm[l], gmlp_ws[l], gmlp_bs[l])
            mix_c = jnp.concatenate([ya_c, yb_c, yc_c, yd_c], -1).astype(x.dtype)
            h_ctx = h_ctx + g1c * (mix_c @ w_out[l])
        h2x = modulate(rmsnorm(x, norm2[l]), sh2, sc2).reshape(Bn * L, D)
        if need_ctx:
            h2c = modulate(rmsnorm(h_ctx, norm2[l]), sh2c, sc2c).reshape(Bn * C, D)
            y = moe_ffn(jnp.concatenate([h2x, h2c], 0), w_router, router_bias, w_e_gate[l], w_e_up[l], w_e_down[l])
            h_ctx = h_ctx + g2c * y[Bn * L:].reshape(Bn, C, D)
        else:
            y = moe_ffn(h2x, w_router, router_bias, w_e_gate[l], w_e_up[l], w_e_down[l])
        x = x + g2 * y[:Bn * L].reshape(Bn, L, D)
    return x
```

```python
import functools

import numpy as np
import jax
import jax.numpy as jnp
from jax import lax
from jax.experimental import pallas as pl
from jax.experimental.pallas import tpu as pltpu

F32 = jnp.float32
BF16 = jnp.bfloat16

D_MODEL = 2048
DEPTH = 2
GRID_W = 64
EPS = 1e-6
NEG_INF = -1e30
GROUP_W = 512
RWKV_HEAD = 64
RWKV_HEADS = 8
DECAY_LORA = 64
ICLR_LORA = 64
GATE_LORA = 128
VRES_LORA = 32
RWKV_GN_EPS = 64e-5
FOURIER_GROUPS = 4
FOURIER_CH = 128
GMLP_GROUPS = 4
GMLP_CH = 128
GMLP_CHUNK = 128
ATT_HEAD = 64
ATT_Q_HEADS = 8
ATT_KV_HEADS = 2
WINDOW = 128
ATT_BLOCK = 128
ROPE_BASE = 10000.0
N_EXPERTS = 64
N_EXPERT_GROUPS = 8
EXPERTS_PER_GROUP = 8
TOP_K = 2
D_EXPERT = 512
MOE_BLOCK = 128
A_COLS = 1920
B_COLS = 512
C_COLS = 1024
D_COLS = 768

LANES = 128
SUBLANES = 8
TM = 256
SCAN_TB = 128
SCAN_GROUP = SUBLANES
VMEM_LIMIT = 56 * 1024 * 1024


def _cparams(sem, vmem=VMEM_LIMIT):
    return pltpu.CompilerParams(dimension_semantics=sem, vmem_limit_bytes=vmem)


def _dot(a, b):
    return jnp.dot(a, b, preferred_element_type=F32)


def _dot_nt(a, b):
    return lax.dot_general(a, b, (((1,), (1,)), ((), ())), preferred_element_type=F32)


def _dot_tn(a, b):
    return lax.dot_general(a, b, (((0,), (0,)), ((), ())), preferred_element_type=F32)


def _split(x):
    hi = x.astype(BF16)
    lo = (x - hi.astype(F32)).astype(BF16)
    return hi, lo


def _dot3(a, b):
    ah, al = _split(a)
    bh, bl = _split(b)
    return _dot(ah, bh) + _dot(ah, bl) + _dot(al, bh)


def _segsum(x, seg):
    xh, xl = _split(x)
    return _dot(xh, seg) + _dot(xl, seg)


def _sigmoid(x):
    return 1.0 / (1.0 + jnp.exp(-x))


def _seg_matrix(width, seg):
    i = np.arange(width) // seg
    return jnp.asarray((i[:, None] == i[None, :]).astype(np.float32), dtype=BF16)


def _ada_kernel(c_ref, w_ref, b_ref, o_ref):
    c = c_ref[...]
    a = (c * _sigmoid(c)).astype(BF16)
    o_ref[...] = _dot(a, w_ref[...].astype(BF16)) + b_ref[...]


def ada_modulation(c_all, w_ada, b_ada, tn=1024):
    rows, d = c_all.shape
    n = w_ada.shape[1]
    return pl.pallas_call(
        _ada_kernel, name="ada_mod",
        out_shape=jax.ShapeDtypeStruct((rows, n), F32),
        grid=(n // tn,),
        in_specs=[pl.BlockSpec((rows, d), lambda j: (0, 0)),
                  pl.BlockSpec((d, tn), lambda j: (0, j)),
                  pl.BlockSpec((1, tn), lambda j: (0, j))],
        out_specs=pl.BlockSpec((rows, tn), lambda j: (0, j)),
        compiler_params=_cparams(("arbitrary",)),
    )(c_all, w_ada, b_ada.reshape(1, n))


def _inproj_kernel(x_ref, mod_ref, g_ref, wa_ref, wb_ref, wc_ref, wd_ref, pa_ref, pb_ref, pc_ref, pd_ref):
    d = x_ref.shape[-1]
    x = x_ref[...]
    y = x * lax.rsqrt(jnp.mean(x * x, axis=-1, keepdims=True) + EPS) * g_ref[...]
    sh = mod_ref[:, 0:d]
    sc = mod_ref[:, d:2 * d]
    h = (y * (1.0 + sc) + sh).astype(BF16)
    pa_ref[...] = _dot(h, wa_ref[...])
    pb_ref[...] = _dot(h, wb_ref[...])
    pc_ref[...] = _dot(h, wc_ref[...])
    pd_ref[...] = _dot(h, wd_ref[...])


def _mod_index(n_batch, ctx_tile):
    return lambda b, i: (jnp.where(i >= ctx_tile, n_batch, b), 0, 0)


def in_projection(stream, mod, gain, w_parts, seq):
    n_batch, lc, d = stream.shape
    ctx_tile = seq // TM
    resident = lambda w: pl.BlockSpec(w.shape, lambda b, i: (0, 0), pipeline_mode=pl.Buffered(1))
    row = lambda n: pl.BlockSpec((None, TM, n), lambda b, i: (b, i, 0))
    return pl.pallas_call(
        _inproj_kernel, name="in_proj",
        out_shape=[jax.ShapeDtypeStruct((n_batch, lc, w.shape[1]), F32) for w in w_parts],
        grid=(n_batch, lc // TM),
        in_specs=[row(d),
                  pl.BlockSpec((None, 1, mod.shape[-1]), _mod_index(n_batch, ctx_tile)),
                  pl.BlockSpec((1, d), lambda b, i: (0, 0))] + [resident(w) for w in w_parts],
        out_specs=[row(w.shape[1]) for w in w_parts],
        compiler_params=_cparams(("parallel", "arbitrary")),
    )(stream, mod, gain.reshape(1, d), *w_parts)


def _rwkv_prep_kernel(has_vres, *refs):
    (pa_ref, hp_ref, hn_ref, conv_ref, w0_ref, wup_ref, a0_ref, aup_ref, gup_ref, kk_par_ref, ka_par_ref,
     seg_ref) = refs[:12]
    rest = refs[12:]
    if has_vres:
        vf_ref, v0_ref, vdn_ref, vup_ref = rest[:4]
        rest = rest[4:]
    r_o, kk_o, v_o, k0_o, w0_o, kka0_o, k1_o, w1_o, kka1_o, g_o = rest
    gw = GROUP_W
    x = pa_ref[...]
    tm = x.shape[0]
    row = lax.broadcasted_iota(jnp.int32, x.shape, 0)
    x_prev = jnp.where(row == 0, hp_ref[...], pltpu.roll(x, 1, 0))
    x_next = jnp.where(row == tm - 1, hn_ref[...], pltpu.roll(x, tm - 1, 0))
    y = x_prev * conv_ref[0:1, :] + x * conv_ref[1:2, :] + x_next * conv_ref[2:3, :]
    r = y[:, 0:gw]
    k = y[:, gw:2 * gw]
    v = y[:, 2 * gw:3 * gw]
    wl = y[:, 3 * gw:3 * gw + 2 * DECAY_LORA]
    al = y[:, 3 * gw + 2 * DECAY_LORA:3 * gw + 2 * DECAY_LORA + 2 * ICLR_LORA]
    gl = y[:, 3 * gw + 2 * DECAY_LORA + 2 * ICLR_LORA:]
    if has_vres:
        mix = _sigmoid(v0_ref[...] + _dot3(_dot3(v, vdn_ref[...]), vup_ref[...]))
        v = v + (vf_ref[...] - v) * mix
    kk = k * kk_par_ref[...]
    nrm = jnp.maximum(jnp.sqrt(_segsum(kk * kk, seg_ref[...])), 1e-12)
    kk = kk / nrm
    r_o[...] = r
    kk_o[...] = kk
    v_o[...] = v
    tanh_wl = jnp.tanh(wl)
    for d, (k_o, w_o, kka_o) in enumerate(((k0_o, w0_o, kka0_o), (k1_o, w1_o, kka1_o))):
        w_raw = w0_ref[d:d + 1, :] + _dot3(tanh_wl, wup_ref[d])
        z = -w_raw
        softplus = jnp.maximum(z, 0.0) + jnp.log(1.0 + jnp.exp(-jnp.abs(z)))
        w_o[...] = jnp.exp(-jnp.exp(-softplus - 0.5))
        a = _sigmoid(a0_ref[d:d + 1, :] + _dot3(al, aup_ref[d]))
        k_o[...] = k * (1.0 + (a - 1.0) * ka_par_ref[...])
        kka_o[...] = kk * a
    g_o[...] = _dot3(_sigmoid(gl), gup_ref[...])


def _conv_halos(pa, seq):
    n_batch, lc, c = pa.shape
    nt = lc // TM
    ctx_tile = seq // TM
    zero = jnp.zeros((n_batch, 1, c), pa.dtype)
    last_rows = pa[:, TM - 1::TM]
    first_rows = pa[:, ::TM]
    prev = jnp.concatenate([zero, last_rows[:, :-1]], axis=1)
    nxt = jnp.concatenate([first_rows[:, 1:], zero], axis=1)
    tile = jnp.arange(nt)[None, :, None]
    prev = jnp.where(tile == ctx_tile, 0.0, prev)
    nxt = jnp.where(tile == ctx_tile - 1, 0.0, nxt)
    return prev.reshape(n_batch, nt, 1, c), nxt.reshape(n_batch, nt, 1, c)


def rwkv_prepare(pa, seq, conv_w, w0, w_up, a0, a_up, g_up, k_k, k_a, v_first=None, vres=None):
    n_batch, lc, c = pa.shape
    gw = GROUP_W
    prev, nxt = _conv_halos(pa, seq)
    pad_rows = lambda w, d, n: jnp.zeros((2 * n, gw), F32).at[d * n:(d + 1) * n].set(w)
    wup = jnp.stack([pad_rows(w_up[d], d, DECAY_LORA) for d in range(2)])
    aup = jnp.stack([pad_rows(a_up[d], d, ICLR_LORA) for d in range(2)])
    seg = _seg_matrix(gw, RWKV_HEAD)
    const = lambda a: pl.BlockSpec(a.shape, lambda b, i: (0,) * a.ndim)
    row = lambda n: pl.BlockSpec((None, TM, n), lambda b, i: (b, i, 0))
    halo = pl.BlockSpec((None, None, 1, c), lambda b, i: (b, i, 0, 0))
    args = [pa, prev, nxt, conv_w, w0, wup, a0, aup, g_up, k_k.reshape(1, gw), k_a.reshape(1, gw), seg]
    specs = [row(c), halo, halo] + [const(a) for a in args[3:]]
    has_vres = vres is not None
    if has_vres:
        v0, v_down, v_up = vres
        vdn = jnp.zeros((gw, LANES), F32).at[:, :VRES_LORA].set(v_down)
        vup = jnp.zeros((LANES, gw), F32).at[:VRES_LORA].set(v_up)
        extra = [v_first, v0.reshape(1, gw), vdn, vup]
        args += extra
        specs += [row(gw)] + [const(a) for a in extra[1:]]
    return pl.pallas_call(
        functools.partial(_rwkv_prep_kernel, has_vres), name="rwkv_prep",
        out_shape=[jax.ShapeDtypeStruct((n_batch, lc, gw), F32)] * 10,
        grid=(n_batch, lc // TM),
        in_specs=specs,
        out_specs=[row(gw)] * 10,
        compiler_params=_cparams(("parallel", "arbitrary")),
    )(*args)


def _scan_kernel(n_batch, rf, kkf, vf, kf, wf, kaf, rb, kkb, vb, kb, wb, kab, bsel_ref, segl_ref, segb_ref,
                 yf_ref, yb_ref, s_ref):
    tb = rf.shape[1]
    n_pairs = GROUP_W // LANES

    @pl.when(pl.program_id(0) == 0)
    def _():
        s_ref[...] = jnp.zeros_like(s_ref)

    segb = segb_ref[...]
    bsel = bsel_ref[...]
    dirs = ((rf, kkf, vf, kf, wf, kaf, yf_ref), (rb, kkb, vb, kb, wb, kab, yb_ref))

    def group(g, carry):
        for d, (r_ref, kk_ref, v_ref, k_ref, w_ref, ka_ref, y_ref) in enumerate(dirs):
            base = g * SCAN_GROUP if d == 0 else tb - SCAN_GROUP - g * SCAN_GROUP
            base = pl.multiple_of(base, SCAN_GROUP)
            for b in range(n_batch):
                ys = [None] * SCAN_GROUP
                for p in range(n_pairs):
                    ln = slice(p * LANES, (p + 1) * LANES)
                    rows = pl.ds(base, SCAN_GROUP)
                    r8, kk8, v8 = r_ref[b, rows, ln], kk_ref[b, rows, ln], v_ref[b, rows, ln]
                    k8, w8, ka8 = k_ref[b, rows, ln], w_ref[b, rows, ln], ka_ref[b, rows, ln]
                    v16 = jnp.concatenate([v8[:, :RWKV_HEAD], v8[:, RWKV_HEAD:]], axis=0).astype(BF16)
                    vcol = _dot_tn(v16, bsel)
                    s = s_ref[d, b, p]
                    for j in range(SCAN_GROUP):
                        t = j if d == 0 else SCAN_GROUP - 1 - j
                        sl = slice(t, t + 1)
                        sk = _dot((s * kk8[sl]).astype(BF16), segb)
                        s = s * w8[sl] - sk * ka8[sl] + vcol[:, t * LANES:(t + 1) * LANES] * k8[sl]
                        y = _dot_nt(segl_ref[p], (s * r8[sl]).astype(BF16))
                        ys[t] = y if ys[t] is None else ys[t] + y
                    s_ref[d, b, p] = s
                for t in range(SCAN_GROUP):
                    y_ref[b, base + t] = ys[t]
        return carry

    lax.fori_loop(0, tb // SCAN_GROUP, group, 0)


def _scan_selectors():
    n_pairs = GROUP_W // LANES
    bsel = np.zeros((2 * SCAN_GROUP, SCAN_GROUP * LANES), np.float32)
    for h in range(2):
        for j in range(SCAN_GROUP):
            bsel[h * SCAN_GROUP + j, j * LANES + h * RWKV_HEAD:j * LANES + (h + 1) * RWKV_HEAD] = 1.0
    segl = np.zeros((n_pairs, SUBLANES, LANES), np.float32)
    for p in range(n_pairs):
        for h in range(2):
            segl[p, 2 * p + h, h * RWKV_HEAD:(h + 1) * RWKV_HEAD] = 1.0
    return jnp.asarray(bsel, BF16), jnp.asarray(segl, BF16), _seg_matrix(LANES, RWKV_HEAD)


def rwkv_scan(r, kk, v, dirs, seq):
    n_batch, lc, gw = r.shape
    tb = SCAN_TB
    nx = seq // tb
    nc = (lc - seq) // tb
    fwd = lambda i: (0, jnp.where(i < nc, nx + i, i - nc), 0)
    bwd = lambda i: (0, jnp.where(i < nc, nx + nc - 1 - i, nx + nc - 1 - i), 0)
    blk = lambda m: pl.BlockSpec((n_batch, tb, gw), m)
    const = lambda a: pl.BlockSpec(a.shape, lambda i: (0,) * a.ndim)
    bsel, segl, segb = _scan_selectors()
    yblk = lambda m: pl.BlockSpec((n_batch, tb, RWKV_HEADS, RWKV_HEAD), lambda i: m(i) + (0,))
    yf, yb = pl.pallas_call(
        functools.partial(_scan_kernel, n_batch), name="rwkv_scan",
        out_shape=[jax.ShapeDtypeStruct((n_batch, lc, RWKV_HEADS, RWKV_HEAD), F32)] * 2,
        grid=(nx + nc,),
        in_specs=[blk(fwd)] * 6 + [blk(bwd)] * 6 + [const(bsel), const(segl), const(segb)],
        out_specs=[yblk(fwd), yblk(bwd)],
        scratch_shapes=[pltpu.VMEM((2, n_batch, gw // LANES, RWKV_HEAD, LANES), F32)],
        compiler_params=_cparams(("arbitrary",)),
    )(r, kk, v, *dirs[0], r, kk, v, *dirs[1], bsel, segl, segb)
    return yf.reshape(n_batch, lc, gw), yb.reshape(n_batch, lc, gw)


def _rwkv_readout_kernel(yf_ref, yb_ref, r_ref, k0_ref, k1_ref, v_ref, g_ref, rk_ref, lnw_ref, lnb_ref, seg_ref,
                         o_ref):
    seg = seg_ref[...]
    inv = 1.0 / RWKV_HEAD
    y = yf_ref[...] + yb_ref[...]
    mu = _segsum(y, seg) * inv
    dlt = y - mu
    var = _segsum(dlt * dlt, seg) * inv
    yn = dlt * lax.rsqrt(var + RWKV_GN_EPS) * lnw_ref[...] + lnb_ref[...]
    kbar = 0.5 * (k0_ref[...] + k1_ref[...])
    bonus = _segsum(r_ref[...] * kbar * rk_ref[...], seg) * v_ref[...]
    o_ref[...] = ((yn + bonus) * g_ref[...]).astype(o_ref.dtype)


def rwkv_readout(yf, yb, r, k0, k1, v, g, r_k, ln_w, ln_b):
    n_batch, lc, gw = r.shape
    seg = _seg_matrix(gw, RWKV_HEAD)
    row = pl.BlockSpec((None, TM, gw), lambda b, i: (b, i, 0))
    const = lambda a: pl.BlockSpec(a.shape, lambda b, i: (0,) * a.ndim)
    params = [r_k.reshape(1, gw), ln_w.reshape(1, gw), ln_b.reshape(1, gw), seg]
    return pl.pallas_call(
        _rwkv_readout_kernel, name="rwkv_readout",
        out_shape=jax.ShapeDtypeStruct((n_batch, lc, gw), BF16),
        grid=(n_batch, lc // TM),
        in_specs=[row] * 7 + [const(a) for a in params],
        out_specs=row,
        compiler_params=_cparams(("parallel", "arbitrary")),
    )(yf, yb, r, k0, k1, v, g, *params)


def _dft_tables(n):
    k = np.arange(n)
    ang = 2.0 * np.pi * ((k[:, None] * k[None, :]) % n) / n
    return np.cos(ang), np.sin(ang)


def _fft_a_kernel(nb, f_ref, gr_ref, gi_ref, zr_ref, zi_ref):
    n1 = gr_ref.shape[1]
    n2_total = f_ref.shape[0] // n1
    j = pl.program_id(2)
    for i in range(nb):
        rows = f_ref[pl.ds(j * nb + i, n1, stride=n2_total), :].astype(BF16)
        zr_ref[i * n1:(i + 1) * n1, :] = _dot(gr_ref[i], rows)
        zi_ref[i * n1:(i + 1) * n1, :] = _dot(gi_ref[i], rows)


def _fft_b_kernel(kb, scale, zr_ref, zi_ref, fc_ref, fs_ref, cc_ref, cs_ref, w_ref, b_ref, o_ref):
    n2 = fc_ref.shape[0]
    n1 = zr_ref.shape[0] // n2
    j = pl.program_id(2)
    fc, fs = fc_ref[...], fs_ref[...]
    for i in range(kb):
        k1 = j * kb + i
        zr = zr_ref[pl.ds(k1, n2, stride=n1), :].astype(BF16)
        zi = zi_ref[pl.ds(k1, n2, stride=n1), :].astype(BF16)
        xr = _dot(fc, zr) + _dot(fs, zi)
        xi = _dot(fc, zi) - _dot(fs, zr)
        re = (_dot(xr.astype(BF16), cc_ref[...]) + _dot(xi.astype(BF16), cs_ref[...])) * scale
        o_ref[pl.ds(k1, n2, stride=n1), :] = _dot(re.astype(BF16), w_ref[...].astype(BF16)) + b_ref[...]


def _dft_dense_kernel(scale, f_ref, fc_ref, fs_ref, cc_ref, cs_ref, w_ref, b_ref, o_ref):
    f = f_ref[...].astype(BF16)
    xr = _dot(fc_ref[...], f)
    xi = -_dot(fs_ref[...], f)
    re = (_dot(xr.astype(BF16), cc_ref[...]) + _dot(xi.astype(BF16), cs_ref[...])) * scale
    o_ref[...] = _dot(re.astype(BF16), w_ref[...].astype(BF16)) + b_ref[...]


def fourier_mix(pb, seq, w, bias):
    n_batch, lc, gw = pb.shape
    ctx = lc - seq
    ch = FOURIER_CH
    groups = FOURIER_GROUPS
    n2 = LANES
    n1 = seq // n2
    cc, cs = (jnp.asarray(t, BF16) for t in _dft_tables(ch))
    bias3 = bias.reshape(groups, 1, ch)

    k1 = np.arange(n1)[None, :, None]
    m1 = np.arange(n1)[None, None, :]
    m2 = np.arange(n2)[:, None, None]
    ang = 2.0 * np.pi * ((k1 * (n2 * m1 + m2)) % seq) / seq
    g_re = jnp.asarray(np.cos(ang), BF16)
    g_im = jnp.asarray(-np.sin(ang), BF16)
    nb = min(16, n2)
    slab = pl.BlockSpec((None, seq, ch), lambda b, g, j: (b, 0, g))
    gspec = pl.BlockSpec((nb, n1, n1), lambda b, g, j: (j, 0, 0))
    zspec = pl.BlockSpec((None, None, nb * n1, ch), lambda b, g, j: (b, g, j, 0))
    zr, zi = pl.pallas_call(
        functools.partial(_fft_a_kernel, nb), name="fft_stage_a",
        out_shape=[jax.ShapeDtypeStruct((n_batch, groups, seq, ch), F32)] * 2,
        grid=(n_batch, groups, n2 // nb),
        in_specs=[slab, gspec, gspec],
        out_specs=[zspec, zspec],
        compiler_params=_cparams(("parallel", "parallel", "arbitrary")),
    )(pb, g_re, g_im)

    fc, fs = (jnp.asarray(t, BF16) for t in _dft_tables(n2))
    kb = min(16, n1)
    scale = 1.0 / np.sqrt(float(seq) * ch)
    zslab = pl.BlockSpec((None, None, seq, ch), lambda b, g, j: (b, g, 0, 0), pipeline_mode=pl.Buffered(1))
    const = lambda a: pl.BlockSpec(a.shape, lambda b, g, j: (0,) * a.ndim)
    wspec = pl.BlockSpec((None, ch, ch), lambda b, g, j: (g, 0, 0))
    bspec = pl.BlockSpec((None, 1, ch), lambda b, g, j: (g, 0, 0))
    y_x = pl.pallas_call(
        functools.partial(_fft_b_kernel, kb, scale), name="fft_stage_b",
        out_shape=jax.ShapeDtypeStruct((n_batch, seq, gw), F32),
        grid=(n_batch, groups, n1 // kb),
        in_specs=[zslab, zslab, const(fc), const(fs), const(cc), const(cs), wspec, bspec],
        out_specs=pl.BlockSpec((None, seq, ch), lambda b, g, j: (b, 0, g)),
        compiler_params=_cparams(("parallel", "parallel", "arbitrary")),
    )(zr, zi, fc, fs, cc, cs, w, bias3)

    fcc, fsc = (jnp.asarray(t, BF16) for t in _dft_tables(ctx))
    const2 = lambda a: pl.BlockSpec(a.shape, lambda b, g: (0,) * a.ndim)
    ctx_tile = seq // ctx
    y_c = pl.pallas_call(
        functools.partial(_dft_dense_kernel, 1.0 / np.sqrt(float(ctx) * ch)), name="dft_ctx",
        out_shape=jax.ShapeDtypeStruct((n_batch, ctx, gw), F32),
        grid=(n_batch, groups),
        in_specs=[pl.BlockSpec((None, ctx, ch), lambda b, g: (b, ctx_tile, g)),
                  const2(fcc), const2(fsc), const2(cc), const2(cs),
                  pl.BlockSpec((None, ch, ch), lambda b, g: (g, 0, 0)),
                  pl.BlockSpec((None, 1, ch), lambda b, g: (g, 0, 0))],
        out_specs=pl.BlockSpec((None, ctx, ch), lambda b, g: (b, 0, g)),
        compiler_params=_cparams(("parallel", "arbitrary")),
    )(pb, fcc, fsc, cc, cs, w, bias3)
    return jnp.concatenate([y_x, y_c], axis=1)


def _gmlp_kernel(pc_ref, norm_ref, ws_ref, bs_ref, o_ref):
    gw = GROUP_W
    x = pc_ref[...]
    z = 0.5 * x * (1.0 + jnp.tanh(0.7978845608028654 * (x + 0.044715 * (x * x * x))))
    tm = x.shape[0]
    for g in range(GMLP_GROUPS):
        u = z[:, g * GMLP_CH:(g + 1) * GMLP_CH]
        v = z[:, gw + g * GMLP_CH:gw + (g + 1) * GMLP_CH]
        v = v * lax.rsqrt(jnp.mean(v * v, axis=-1, keepdims=True) + EPS) * norm_ref[g:g + 1, :]
        v = v.astype(BF16)
        ws = ws_ref[g].astype(BF16)
        for c in range(tm // GMLP_CHUNK):
            rows = slice(c * GMLP_CHUNK, (c + 1) * GMLP_CHUNK)
            f = _dot(ws, v[rows]) + bs_ref[g]
            o_ref[rows, g * GMLP_CH:(g + 1) * GMLP_CH] = (u[rows] * f).astype(o_ref.dtype)


def gmlp_mix(pc, norm_g, ws, bs):
    n_batch, lc, c = pc.shape
    bs_b = jnp.broadcast_to(bs[:, :, None], bs.shape + (GMLP_CH,))
    const = lambda a: pl.BlockSpec(a.shape, lambda b, i: (0,) * a.ndim)
    return pl.pallas_call(
        _gmlp_kernel, name="gmlp",
        out_shape=jax.ShapeDtypeStruct((n_batch, lc, GROUP_W), BF16),
        grid=(n_batch, lc // TM),
        in_specs=[pl.BlockSpec((None, TM, c), lambda b, i: (b, i, 0)), const(norm_g), const(ws), const(bs_b)],
        out_specs=pl.BlockSpec((None, TM, GROUP_W), lambda b, i: (b, i, 0)),
        compiler_params=_cparams(("parallel", "arbitrary")),
    )(pc, norm_g, ws, bs_b)


def _rope_tables(seq, ctx):
    half = ATT_HEAD // 4
    inv_freq = ROPE_BASE ** (-np.arange(half, dtype=np.float64) / half)
    pos = np.arange(seq)
    ang_r = (pos // GRID_W)[:, None] * inv_freq[None, :]
    ang_c = (pos % GRID_W)[:, None] * inv_freq[None, :]
    cos = np.concatenate([np.cos(ang_r)] * 2 + [np.cos(ang_c)] * 2, axis=1)
    sin = np.concatenate([-np.sin(ang_r), np.sin(ang_r), -np.sin(ang_c), np.sin(ang_c)], axis=1)
    cos = np.concatenate([cos, np.ones((ctx, ATT_HEAD))], axis=0)
    sin = np.concatenate([sin, np.zeros((ctx, ATT_HEAD))], axis=0)
    return jnp.asarray(np.tile(cos, (1, 2)), F32), jnp.asarray(np.tile(sin, (1, 2)), F32)


def _rope(t, cos, sin):
    n = t.shape[1]
    q = ATT_HEAD // 4
    lane = lax.broadcasted_iota(jnp.int32, t.shape, 1)
    swapped = jnp.where((lane % (2 * q)) < q, pltpu.roll(t, n - q, 1), pltpu.roll(t, q, 1))
    return t * cos + swapped * sin


def _attn_prep_kernel(pd_ref, cos_ref, sin_ref, qg_ref, kg_ref, seg_ref, q_o, k_o, v_o):
    gw = GROUP_W
    kvw = ATT_KV_HEADS * ATT_HEAD
    seg = seg_ref[...]
    inv = 1.0 / ATT_HEAD
    cos, sin = cos_ref[...], sin_ref[...]
    q = pd_ref[:, 0:gw]
    q = q * lax.rsqrt(_segsum(q * q, seg) * inv + EPS) * qg_ref[...]
    q = _rope(q, jnp.concatenate([cos] * (gw // kvw), axis=1), jnp.concatenate([sin] * (gw // kvw), axis=1))
    q_o[...] = (q * (ATT_HEAD ** -0.5)).astype(q_o.dtype)
    k = pd_ref[:, gw:gw + kvw]
    k = k * lax.rsqrt(_segsum(k * k, seg[:kvw, :kvw]) * inv + EPS) * kg_ref[...]
    k = _rope(k, cos, sin)
    v = pd_ref[:, gw + kvw:gw + 2 * kvw]
    lane = lax.broadcasted_iota(jnp.int32, k.shape, 1)

    def variants(t):
        h0 = jnp.where(lane < ATT_HEAD, t, 0.0)
        h1 = jnp.where(lane >= ATT_HEAD, t, 0.0)
        return jnp.concatenate([h0, pltpu.roll(h0, ATT_HEAD, 1), pltpu.roll(h1, ATT_HEAD, 1), h1], axis=1)

    k_o[...] = variants(k).astype(k_o.dtype)
    v_o[...] = variants(v).astype(v_o.dtype)


def attention_prepare(pd, seq, q_gain, k_gain):
    n_batch, lc, c = pd.shape
    gw = GROUP_W
    kvw = ATT_KV_HEADS * ATT_HEAD
    cos, sin = _rope_tables(seq, lc - seq)
    seg = _seg_matrix(gw, ATT_HEAD)
    qg = jnp.tile(q_gain, ATT_Q_HEADS).reshape(1, gw)
    kg = jnp.tile(k_gain, ATT_KV_HEADS).reshape(1, kvw)
    const = lambda a: pl.BlockSpec(a.shape, lambda b, i: (0,) * a.ndim)
    tab = pl.BlockSpec((TM, kvw), lambda b, i: (i, 0))
    row = lambda n: pl.BlockSpec((None, TM, n), lambda b, i: (b, i, 0))
    return pl.pallas_call(
        _attn_prep_kernel, name="attn_prep",
        out_shape=[jax.ShapeDtypeStruct((n_batch, lc, gw), BF16)] * 3,
        grid=(n_batch, lc // TM),
        in_specs=[row(c), tab, tab, const(qg), const(kg), const(seg)],
        out_specs=[row(gw)] * 3,
        compiler_params=_cparams(("parallel", "arbitrary")),
    )(pd, cos, sin, qg, kg, seg)


def _attn_kernel(nb, sink_ref, q_ref, kp_ref, kc_ref, kn_ref, kx_ref, vp_ref, vc_ref, vn_ref, vx_ref, o_ref):
    i = pl.program_id(1)
    blk = q_ref.shape[0]
    is_lat = i < nb
    rowi = lax.broadcasted_iota(jnp.int32, (blk, blk), 0)
    coli = lax.broadcasted_iota(jnp.int32, (blk, blk), 1)
    m_prev = (coli >= rowi) & is_lat & (i >= 1)
    m_cur = jnp.broadcast_to(is_lat, (blk, blk))
    m_next = (coli <= rowi) & (i < nb - 1)
    for p in range(ATT_Q_HEADS // 2):
        q = q_ref[:, p * LANES:(p + 1) * LANES]
        acc = None
        for par in range(2):
            h = 2 * p + par
            g = h // (ATT_Q_HEADS // ATT_KV_HEADS)
            col = slice((2 * g + par) * LANES, (2 * g + par + 1) * LANES)
            sink = sink_ref[h]
            s1 = jnp.where(m_prev, _dot_nt(q, kp_ref[:, col]), NEG_INF)
            s2 = jnp.where(m_cur, _dot_nt(q, kc_ref[:, col]), NEG_INF)
            s3 = jnp.where(m_next, _dot_nt(q, kn_ref[:, col]), NEG_INF)
            sx = _dot_nt(q, kx_ref[:, col])
            m = jnp.maximum(jnp.maximum(jnp.max(jnp.maximum(jnp.maximum(s1, s2), s3), axis=-1, keepdims=True),
                                        jnp.max(sx, axis=-1, keepdims=True)), sink)
            p1, p2, p3, px = jnp.exp(s1 - m), jnp.exp(s2 - m), jnp.exp(s3 - m), jnp.exp(sx - m)
            den = (jnp.sum(p1 + p2 + p3, axis=-1, keepdims=True) + jnp.sum(px, axis=-1, keepdims=True)
                   + jnp.exp(sink - m))
            o = (_dot(p1.astype(BF16), vp_ref[:, col]) + _dot(p2.astype(BF16), vc_ref[:, col])
                 + _dot(p3.astype(BF16), vn_ref[:, col]) + _dot(px.astype(BF16), vx_ref[:, col])) / den
            acc = o if acc is None else acc + o
        o_ref[:, p * LANES:(p + 1) * LANES] = acc.astype(o_ref.dtype)


def window_attention(q, kv, vv, sink, seq):
    n_batch, lc, gw = q.shape
    blk = ATT_BLOCK
    nb = seq // blk
    ctx = lc - seq
    wide = kv.shape[-1]
    qspec = pl.BlockSpec((None, blk, gw), lambda b, i, s: (b, i, 0))
    prev = pl.BlockSpec((None, blk, wide), lambda b, i, s: (b, jnp.clip(i - 1, 0, nb - 1), 0))
    cur = pl.BlockSpec((None, blk, wide), lambda b, i, s: (b, jnp.minimum(i, nb - 1), 0))
    nxt = pl.BlockSpec((None, blk, wide), lambda b, i, s: (b, jnp.clip(i + 1, 0, nb - 1), 0))
    cx = pl.BlockSpec((None, ctx, wide), lambda b, i, s: (b, seq // ctx, 0))
    return pl.pallas_call(
        functools.partial(_attn_kernel, nb), name="window_attn",
        out_shape=jax.ShapeDtypeStruct((n_batch, lc, gw), BF16),
        grid_spec=pltpu.PrefetchScalarGridSpec(
            num_scalar_prefetch=1, grid=(n_batch, lc // blk),
            in_specs=[qspec, prev, cur, nxt, cx, prev, cur, nxt, cx],
            out_specs=qspec),
        compiler_params=_cparams(("parallel", "arbitrary")),
    )(sink.astype(F32), q, kv, kv, kv, kv, vv, vv, vv, vv)


def _top2_rows(val, iota):
    m1 = jnp.max(val, axis=0, keepdims=True)
    i1 = jnp.min(jnp.where(val == m1, iota, SUBLANES), axis=0, keepdims=True)
    rest = jnp.where(iota == i1, -jnp.inf, val)
    m2 = jnp.max(rest, axis=0, keepdims=True)
    i2 = jnp.min(jnp.where(rest == m2, iota, SUBLANES), axis=0, keepdims=True)
    return m1, m2, i1, i2


def _outproj_kernel(ya_ref, yb_ref, yc_ref, yd_ref, x_ref, mod_ref, g_ref, wo_ref, wr_ref, rb_ref,
                    xo_ref, h_ref, e_ref, gw_ref):
    d = x_ref.shape[-1]
    gw = GROUP_W
    acc = None
    for n, y_ref in enumerate((ya_ref, yb_ref, yc_ref, yd_ref)):
        part = _dot(y_ref[...].astype(BF16), wo_ref[n * gw:(n + 1) * gw, :])
        acc = part if acc is None else acc + part
    x = x_ref[...] + mod_ref[:, 2 * d:3 * d] * acc
    xo_ref[...] = x
    y = x * lax.rsqrt(jnp.mean(x * x, axis=-1, keepdims=True) + EPS) * g_ref[...]
    h = y * (1.0 + mod_ref[:, 4 * d:5 * d]) + mod_ref[:, 3 * d:4 * d]
    h_ref[...] = h.astype(h_ref.dtype)
    hh, hl = _split(h)
    wh, wl = _split(wr_ref[...])
    logits = _dot_nt(wh, hh) + _dot_nt(wh, hl) + _dot_nt(wl, hh)
    scores = _sigmoid(logits)
    sel = scores + rb_ref[...]
    tm = x.shape[0]
    iota = lax.broadcasted_iota(jnp.int32, (EXPERTS_PER_GROUP, tm), 0)
    best_val, best = None, None
    for grp in range(N_EXPERT_GROUPS):
        m1, m2, _, _ = _top2_rows(sel[grp * EXPERTS_PER_GROUP:(grp + 1) * EXPERTS_PER_GROUP], iota)
        gs = m1 + m2
        if grp == 0:
            best_val, best = gs, jnp.zeros_like(gs, dtype=jnp.int32)
        else:
            upd = gs > best_val
            best_val = jnp.where(upd, gs, best_val)
            best = jnp.where(upd, grp, best)
    sel_in = jnp.zeros((EXPERTS_PER_GROUP, tm), F32)
    sc_in = jnp.zeros((EXPERTS_PER_GROUP, tm), F32)
    for grp in range(N_EXPERT_GROUPS):
        rows = slice(grp * EXPERTS_PER_GROUP, (grp + 1) * EXPERTS_PER_GROUP)
        hit = best == grp
        sel_in = jnp.where(hit, sel[rows], sel_in)
        sc_in = jnp.where(hit, scores[rows], sc_in)
    _, _, i1, i2 = _top2_rows(sel_in, iota)
    w1 = jnp.sum(jnp.where(iota == i1, sc_in, 0.0), axis=0, keepdims=True)
    w2 = jnp.sum(jnp.where(iota == i2, sc_in, 0.0), axis=0, keepdims=True)
    tot = w1 + w2
    e_ref[0:1, :] = best * EXPERTS_PER_GROUP + i1
    e_ref[1:2, :] = best * EXPERTS_PER_GROUP + i2
    gw_ref[0:1, :] = w1 / tot
    gw_ref[1:2, :] = w2 / tot


def out_projection_router(ys, stream, mod, gain2, w_out, w_router, router_bias, seq):
    n_batch, lc, d = stream.shape
    gw = GROUP_W
    ctx_tile = seq // TM
    wr_t = w_router.T
    rb = jnp.broadcast_to(router_bias.astype(F32)[:, None], (N_EXPERTS, TM))
    row = lambda n: pl.BlockSpec((None, TM, n), lambda b, i: (b, i, 0))
    const = lambda a: pl.BlockSpec(a.shape, lambda b, i: (0,) * a.ndim)
    lane_row = pl.BlockSpec((None, TOP_K, TM), lambda b, i: (b, 0, i))
    return pl.pallas_call(
        _outproj_kernel, name="out_proj_router",
        out_shape=[jax.ShapeDtypeStruct((n_batch, lc, d), F32), jax.ShapeDtypeStruct((n_batch, lc, d), BF16),
                   jax.ShapeDtypeStruct((n_batch, TOP_K, lc), jnp.int32),
                   jax.ShapeDtypeStruct((n_batch, TOP_K, lc), F32)],
        grid=(n_batch, lc // TM),
        in_specs=[row(gw)] * 4 + [row(d), pl.BlockSpec((None, 1, mod.shape[-1]), _mod_index(n_batch, ctx_tile)),
                                  pl.BlockSpec((1, d), lambda b, i: (0, 0)),
                                  pl.BlockSpec(w_out.shape, lambda b, i: (0, 0), pipeline_mode=pl.Buffered(1)),
                                  const(wr_t), const(rb)],
        out_specs=[row(d), row(d), lane_row, lane_row],
        compiler_params=_cparams(("parallel", "arbitrary")),
    )(*ys, stream, mod, gain2.reshape(1, d), w_out, wr_t, rb)


def _expert_kernel(be_ref, nu_ref, x_ref, wg_ref, wu_ref, wd_ref, o_ref, wg_s, wu_s, wd_s):
    i = pl.program_id(0)
    prev = be_ref[jnp.maximum(i - 1, 0)]
    used = i < nu_ref[0]

    @pl.when(used & ((i == 0) | (be_ref[i] != prev)))
    def _():
        wg_s[...] = wg_ref[...].astype(BF16)
        wu_s[...] = wu_ref[...].astype(BF16)
        wd_s[...] = wd_ref[...].astype(BF16)

    @pl.when(used)
    def _():
        x = x_ref[...]
        a = _dot(x, wg_s[...])
        hb = (a * _sigmoid(a)) * _dot(x, wu_s[...])
        o_ref[...] = _dot(hb.astype(BF16), wd_s[...])

    @pl.when(jnp.logical_not(used))
    def _():
        o_ref[...] = jnp.zeros_like(o_ref)


def expert_ffn(xs, block_e, n_used, wg, wu, wd):
    rows, d = xs.shape
    n_blocks = rows // MOE_BLOCK
    de = wg.shape[-1]
    return pl.pallas_call(
        _expert_kernel, name="moe_experts",
        out_shape=jax.ShapeDtypeStruct((rows, d), F32),
        grid_spec=pltpu.PrefetchScalarGridSpec(
            num_scalar_prefetch=2, grid=(n_blocks,),
            in_specs=[pl.BlockSpec((MOE_BLOCK, d), lambda i, be, nu: (i, 0)),
                      pl.BlockSpec((None, d, de), lambda i, be, nu: (be[i], 0, 0)),
                      pl.BlockSpec((None, d, de), lambda i, be, nu: (be[i], 0, 0)),
                      pl.BlockSpec((None, de, d), lambda i, be, nu: (be[i], 0, 0))],
            out_specs=pl.BlockSpec((MOE_BLOCK, d), lambda i, be, nu: (i, 0)),
            scratch_shapes=[pltpu.VMEM((d, de), BF16), pltpu.VMEM((d, de), BF16), pltpu.VMEM((de, d), BF16)]),
        compiler_params=_cparams(("arbitrary",)),
    )(block_e, n_used, xs, wg, wu, wd)


def _residual_kernel(x_ref, y_ref, mod_ref, o_ref):
    d = x_ref.shape[-1]
    o_ref[...] = x_ref[...] + mod_ref[:, 5 * d:6 * d] * y_ref[...]


def gated_residual(stream, y, mod, seq):
    n_batch, lc, d = stream.shape
    row = pl.BlockSpec((None, TM, d), lambda b, i: (b, i, 0))
    return pl.pallas_call(
        _residual_kernel, name="moe_residual",
        out_shape=jax.ShapeDtypeStruct((n_batch, lc, d), F32),
        grid=(n_batch, lc // TM),
        in_specs=[row, row, pl.BlockSpec((None, 1, mod.shape[-1]), _mod_index(n_batch, seq // TM))],
        out_specs=row,
        compiler_params=_cparams(("parallel", "arbitrary")),
    )(stream, y, mod)


def moe_ffn(h2, eidx, gates, wg, wu, wd):
    n_batch, lc, d = h2.shape
    n_tok = n_batch * lc
    n_assign = n_tok * TOP_K
    e = jnp.swapaxes(eidx, 1, 2).reshape(n_assign)
    gate = jnp.swapaxes(gates, 1, 2).reshape(n_assign)
    onehot = (e[:, None] == jnp.arange(N_EXPERTS, dtype=jnp.int32)[None, :]).astype(jnp.int32)
    cum = jnp.cumsum(onehot, axis=0)
    rank = jnp.take_along_axis(cum, e[:, None], axis=1)[:, 0] - 1
    counts = cum[-1]
    padded = (counts + MOE_BLOCK - 1) // MOE_BLOCK * MOE_BLOCK
    pad_end = jnp.cumsum(padded)
    dest = (pad_end - padded)[e] + rank
    n_blocks = n_assign // MOE_BLOCK + N_EXPERTS
    tok = jnp.arange(n_assign, dtype=jnp.int32) // TOP_K
    row_tok = jnp.zeros((n_blocks * MOE_BLOCK,), jnp.int32).at[dest].set(tok)
    block_e = jnp.minimum(jnp.searchsorted(pad_end, jnp.arange(n_blocks, dtype=jnp.int32) * MOE_BLOCK, side='right'),
                          N_EXPERTS - 1).astype(jnp.int32)
    n_used = (pad_end[-1:] // MOE_BLOCK).astype(jnp.int32)
    xs = h2.reshape(n_tok, d)[row_tok]
    ys = expert_ffn(xs, block_e, n_used, wg, wu, wd)
    contrib = ys[dest] * gate[:, None]
    return contrib.reshape(n_tok, TOP_K, d).sum(axis=1).reshape(n_batch, lc, d)


def kernel(x, c, ctx, c_ctx, norm1, norm2, w_ada, b_ada, w_in, w_out, rwkv_conv, rwkv_w0, rwkv_w_up, rwkv_a0,
           rwkv_a_up, rwkv_g_up, rwkv_k_k, rwkv_k_a, rwkv_r_k, rwkv_ln_w, rwkv_ln_b, rwkv_v0, rwkv_v_down,
           rwkv_v_up, fourier_w, fourier_b, gmlp_norm, gmlp_ws, gmlp_bs, attn_q_norm, attn_k_norm, attn_sink,
           w_router, router_bias, w_e_gate, w_e_up, w_e_down):
    n_batch, seq, d = x.shape
    depth = w_in.shape[0]
    stream = jnp.concatenate([x, ctx], axis=1)
    c_all = jnp.zeros((SUBLANES, d), F32).at[:n_batch].set(c).at[n_batch].set(c_ctx)
    col_splits = np.cumsum([A_COLS, B_COLS, C_COLS])
    v_first = None
    for l in range(depth):
        mod = ada_modulation(c_all, w_ada[l], b_ada[l]).reshape(SUBLANES, 1, 6 * d)
        w_parts = [w.astype(BF16) for w in jnp.split(w_in[l], col_splits, axis=1)]
        pa, pb, pc, pd = in_projection(stream, mod, norm1[l], w_parts, seq)

        vres = None if l == 0 else (rwkv_v0[l - 1], rwkv_v_down[l - 1], rwkv_v_up[l - 1])
        r, kk, v, k0, wd0, kka0, k1, wd1, kka1, g = rwkv_prepare(
            pa, seq, rwkv_conv[l], rwkv_w0[l], rwkv_w_up[l], rwkv_a0[l], rwkv_a_up[l], rwkv_g_up[l],
            rwkv_k_k[l], rwkv_k_a[l], v_first, vres)
        if l == 0:
            v_first = v
        yf, yb = rwkv_scan(r, kk, v, ((k0, wd0, kka0), (k1, wd1, kka1)), seq)
        y_a = rwkv_readout(yf, yb, r, k0, k1, v, g, rwkv_r_k[l].reshape(-1), rwkv_ln_w[l], rwkv_ln_b[l])

        y_b = fourier_mix(pb, seq, fourier_w[l], fourier_b[l])
        y_c = gmlp_mix(pc, gmlp_norm[l], gmlp_ws[l], gmlp_bs[l])
        q, kv, vv = attention_prepare(pd, seq, attn_q_norm[l], attn_k_norm[l])
        y_d = window_attention(q, kv, vv, attn_sink[l], seq)

        stream, h2, eidx, gates = out_projection_router(
            (y_a, y_b, y_c, y_d), stream, mod, norm2[l], w_out[l].astype(BF16), w_router, router_bias, seq)
        y_moe = moe_ffn(h2, eidx, gates, w_e_gate[l], w_e_up[l], w_e_down[l])
        stream = gated_residual(stream, y_moe, mod, seq)
    return stream[:, :seq]
```

```python
import functools

import numpy as np
import jax
import jax.numpy as jnp
from jax import lax
from jax.experimental import pallas as pl
from jax.experimental.pallas import tpu as pltpu

F32 = jnp.float32
BF16 = jnp.bfloat16

D_MODEL = 2048
DEPTH = 2
GRID_W = 64
EPS = 1e-6
NEG_INF = -1e30
GROUP_W = 512
RWKV_HEAD = 64
RWKV_HEADS = 8
DECAY_LORA = 64
ICLR_LORA = 64
GATE_LORA = 128
VRES_LORA = 32
RWKV_GN_EPS = 64e-5
FOURIER_GROUPS = 4
FOURIER_CH = 128
GMLP_GROUPS = 4
GMLP_CH = 128
GMLP_CHUNK = 128
ATT_HEAD = 64
ATT_Q_HEADS = 8
ATT_KV_HEADS = 2
WINDOW = 128
ATT_BLOCK = 128
ROPE_BASE = 10000.0
N_EXPERTS = 64
N_EXPERT_GROUPS = 8
EXPERTS_PER_GROUP = 8
TOP_K = 2
D_EXPERT = 512
MOE_BLOCK = 128
A_COLS = 1920
B_COLS = 512
C_COLS = 1024
D_COLS = 768

LANES = 128
SUBLANES = 8
TM = 256
SCAN_TB = 128
SCAN_GROUP = SUBLANES
VMEM_LIMIT = 56 * 1024 * 1024


def _cparams(sem, vmem=VMEM_LIMIT):
    return pltpu.CompilerParams(dimension_semantics=sem, vmem_limit_bytes=vmem)


def _dot(a, b):
    return jnp.dot(a, b, preferred_element_type=F32)


def _dot_nt(a, b):
    return lax.dot_general(a, b, (((1,), (1,)), ((), ())), preferred_element_type=F32)


def _dot_tn(a, b):
    return lax.dot_general(a, b, (((0,), (0,)), ((), ())), preferred_element_type=F32)


def _split(x):
    hi = x.astype(BF16)
    lo = (x - hi.astype(F32)).astype(BF16)
    return hi, lo


def _dot3(a, b):
    ah, al = _split(a)
    bh, bl = _split(b)
    return _dot(ah, bh) + _dot(ah, bl) + _dot(al, bh)


def _segsum(x, seg):
    xh, xl = _split(x)
    return _dot(xh, seg) + _dot(xl, seg)


def _sigmoid(x):
    return 1.0 / (1.0 + jnp.exp(-x))


def _seg_matrix(width, seg):
    i = np.arange(width) // seg
    return jnp.asarray((i[:, None] == i[None, :]).astype(np.float32), dtype=BF16)


def _ada_kernel(c_ref, w_ref, b_ref, o_ref):
    c = c_ref[...]
    a = (c * _sigmoid(c)).astype(BF16)
    o_ref[...] = _dot(a, w_ref[...].astype(BF16)) + b_ref[...]


def ada_modulation(c_all, w_ada, b_ada, tn=1024):
    rows, d = c_all.shape
    n = w_ada.shape[1]
    return pl.pallas_call(
        _ada_kernel, name="ada_mod",
        out_shape=jax.ShapeDtypeStruct((rows, n), F32),
        grid=(n // tn,),
        in_specs=[pl.BlockSpec((rows, d), lambda j: (0, 0)),
                  pl.BlockSpec((d, tn), lambda j: (0, j)),
                  pl.BlockSpec((1, tn), lambda j: (0, j))],
        out_specs=pl.BlockSpec((rows, tn), lambda j: (0, j)),
        compiler_params=_cparams(("arbitrary",)),
    )(c_all, w_ada, b_ada.reshape(1, n))


def _inproj_kernel(x_ref, mod_ref, g_ref, wa_ref, wb_ref, wc_ref, wd_ref, pa_ref, pb_ref, pc_ref, pd_ref):
    d = x_ref.shape[-1]
    x = x_ref[...]
    y = x * lax.rsqrt(jnp.mean(x * x, axis=-1, keepdims=True) + EPS) * g_ref[...]
    sh = mod_ref[:, 0:d]
    sc = mod_ref[:, d:2 * d]
    h = (y * (1.0 + sc) + sh).astype(BF16)
    pa_ref[...] = _dot(h, wa_ref[...])
    pb_ref[...] = _dot(h, wb_ref[...])
    pc_ref[...] = _dot(h, wc_ref[...])
    pd_ref[...] = _dot(h, wd_ref[...])


def _mod_index(n_batch, ctx_tile):
    return lambda b, i: (jnp.where(i >= ctx_tile, n_batch, b), 0, 0)


def in_projection(stream, mod, gain, w_parts, seq):
    n_batch, lc, d = stream.shape
    ctx_tile = seq // TM
    resident = lambda w: pl.BlockSpec(w.shape, lambda b, i: (0, 0), pipeline_mode=pl.Buffered(1))
    row = lambda n: pl.BlockSpec((None, TM, n), lambda b, i: (b, i, 0))
    return pl.pallas_call(
        _inproj_kernel, name="in_proj",
        out_shape=[jax.ShapeDtypeStruct((n_batch, lc, w.shape[1]), F32) for w in w_parts],
        grid=(n_batch, lc // TM),
        in_specs=[row(d),
                  pl.BlockSpec((None, 1, mod.shape[-1]), _mod_index(n_batch, ctx_tile)),
                  pl.BlockSpec((1, d), lambda b, i: (0, 0))] + [resident(w) for w in w_parts],
        out_specs=[row(w.shape[1]) for w in w_parts],
        compiler_params=_cparams(("parallel", "arbitrary")),
    )(stream, mod, gain.reshape(1, d), *w_parts)


def _rwkv_prep_kernel(has_vres, *refs):
    (pa_ref, hp_ref, hn_ref, conv_ref, w0_ref, wup_ref, a0_ref, aup_ref, gup_ref, kk_par_ref, ka_par_ref,
     seg_ref) = refs[:12]
    rest = refs[12:]
    if has_vres:
        vf_ref, v0_ref, vdn_ref, vup_ref = rest[:4]
        rest = rest[4:]
    r_o, kk_o, v_o, k0_o, w0_o, kka0_o, k1_o, w1_o, kka1_o, g_o = rest
    gw = GROUP_W
    x = pa_ref[...]
    tm = x.shape[0]
    row = lax.broadcasted_iota(jnp.int32, x.shape, 0)
    x_prev = jnp.where(row == 0, hp_ref[...], pltpu.roll(x, 1, 0))
    x_next = jnp.where(row == tm - 1, hn_ref[...], pltpu.roll(x, tm - 1, 0))
    y = x_prev * conv_ref[0:1, :] + x * conv_ref[1:2, :] + x_next * conv_ref[2:3, :]
    r = y[:, 0:gw]
    k = y[:, gw:2 * gw]
    v = y[:, 2 * gw:3 * gw]
    wl = y[:, 3 * gw:3 * gw + 2 * DECAY_LORA]
    al = y[:, 3 * gw + 2 * DECAY_LORA:3 * gw + 2 * DECAY_LORA + 2 * ICLR_LORA]
    gl = y[:, 3 * gw + 2 * DECAY_LORA + 2 * ICLR_LORA:]
    if has_vres:
        mix = _sigmoid(v0_ref[...] + _dot3(_dot3(v, vdn_ref[...]), vup_ref[...]))
        v = v + (vf_ref[...] - v) * mix
    kk = k * kk_par_ref[...]
    nrm = jnp.maximum(jnp.sqrt(_segsum(kk * kk, seg_ref[...])), 1e-12)
    kk = kk / nrm
    r_o[...] = r
    kk_o[...] = kk
    v_o[...] = v
    tanh_wl = jnp.tanh(wl)
    for d, (k_o, w_o, kka_o) in enumerate(((k0_o, w0_o, kka0_o), (k1_o, w1_o, kka1_o))):
        w_raw = w0_ref[d:d + 1, :] + _dot3(tanh_wl, wup_ref[d])
        z = -w_raw
        softplus = jnp.maximum(z, 0.0) + jnp.log(1.0 + jnp.exp(-jnp.abs(z)))
        w_o[...] = jnp.exp(-jnp.exp(-softplus - 0.5))
        a = _sigmoid(a0_ref[d:d + 1, :] + _dot3(al, aup_ref[d]))
        k_o[...] = k * (1.0 + (a - 1.0) * ka_par_ref[...])
        kka_o[...] = kk * a
    g_o[...] = _dot3(_sigmoid(gl), gup_ref[...])


def _conv_halos(pa, seq):
    n_batch, lc, c = pa.shape
    nt = lc // TM
    ctx_tile = seq // TM
    zero = jnp.zeros((n_batch, 1, c), pa.dtype)
    last_rows = pa[:, TM - 1::TM]
    first_rows = pa[:, ::TM]
    prev = jnp.concatenate([zero, last_rows[:, :-1]], axis=1)
    nxt = jnp.concatenate([first_rows[:, 1:], zero], axis=1)
    tile = jnp.arange(nt)[None, :, None]
    prev = jnp.where(tile == ctx_tile, 0.0, prev)
    nxt = jnp.where(tile == ctx_tile - 1, 0.0, nxt)
    return prev.reshape(n_batch, nt, 1, c), nxt.reshape(n_batch, nt, 1, c)


def rwkv_prepare(pa, seq, conv_w, w0, w_up, a0, a_up, g_up, k_k, k_a, v_first=None, vres=None):
    n_batch, lc, c = pa.shape
    gw = GROUP_W
    prev, nxt = _conv_halos(pa, seq)
    pad_rows = lambda w, d, n: jnp.zeros((2 * n, gw), F32).at[d * n:(d + 1) * n].set(w)
    wup = jnp.stack([pad_rows(w_up[d], d, DECAY_LORA) for d in range(2)])
    aup = jnp.stack([pad_rows(a_up[d], d, ICLR_LORA) for d in range(2)])
    seg = _seg_matrix(gw, RWKV_HEAD)
    const = lambda a: pl.BlockSpec(a.shape, lambda b, i: (0,) * a.ndim)
    row = lambda n: pl.BlockSpec((None, TM, n), lambda b, i: (b, i, 0))
    halo = pl.BlockSpec((None, None, 1, c), lambda b, i: (b, i, 0, 0))
    args = [pa, prev, nxt, conv_w, w0, wup, a0, aup, g_up, k_k.reshape(1, gw), k_a.reshape(1, gw), seg]
    specs = [row(c), halo, halo] + [const(a) for a in args[3:]]
    has_vres = vres is not None
    if has_vres:
        v0, v_down, v_up = vres
        vdn = jnp.zeros((gw, LANES), F32).at[:, :VRES_LORA].set(v_down)
        vup = jnp.zeros((LANES, gw), F32).at[:VRES_LORA].set(v_up)
        extra = [v_first, v0.reshape(1, gw), vdn, vup]
        args += extra
        specs += [row(gw)] + [const(a) for a in extra[1:]]
    return pl.pallas_call(
        functools.partial(_rwkv_prep_kernel, has_vres), name="rwkv_prep",
        out_shape=[jax.ShapeDtypeStruct((n_batch, lc, gw), F32)] * 10,
        grid=(n_batch, lc // TM),
        in_specs=specs,
        out_specs=[row(gw)] * 10,
        compiler_params=_cparams(("parallel", "arbitrary")),
    )(*args)


def _scan_kernel(n_batch, rf, kkf, vf, kf, wf, kaf, rb, kkb, vb, kb, wb, kab, bsel_ref, segl_ref, segb_ref,
                 yf_ref, yb_ref, s_ref):
    tb = rf.shape[1]
    n_pairs = GROUP_W // LANES

    @pl.when(pl.program_id(0) == 0)
    def _():
        s_ref[...] = jnp.zeros_like(s_ref)

    segb = segb_ref[...]
    dirs = ((rf, kkf, vf, kf, wf, kaf, yf_ref), (rb, kkb, vb, kb, wb, kab, yb_ref))
    chains = [(d, b, p) for d in range(2) for b in range(n_batch) for p in range(n_pairs)]

    def group(g, carry):
        bases = (pl.multiple_of(g * SCAN_GROUP, SCAN_GROUP),
                 pl.multiple_of(tb - SCAN_GROUP - g * SCAN_GROUP, SCAN_GROUP))

        def row(ref, d, b, p, t):
            return ref[b, pl.ds(bases[d], SCAN_GROUP), p * LANES:(p + 1) * LANES][t:t + 1]

        vt = {}
        for d, b, p in chains:
            v8 = dirs[d][2][b, pl.ds(bases[d], SCAN_GROUP), p * LANES:(p + 1) * LANES]
            vt[d, b, p] = jnp.concatenate([v8[:, :RWKV_HEAD], v8[:, RWKV_HEAD:]], axis=0).T.astype(BF16)
        for j in range(SCAN_GROUP):
            sk, vk = {}, {}
            for d, b, p in chains:
                t = j if d == 0 else SCAN_GROUP - 1 - j
                r_ref, kk_ref, v_ref, k_ref, w_ref, ka_ref, y_ref = dirs[d]
                s = s_ref[d, b, p]
                sk[d, b, p] = _dot((s * row(kk_ref, d, b, p, t)).astype(BF16), segb)
                kmat = (bsel_ref[t] * row(k_ref, d, b, p, t)).astype(BF16)
                vk[d, b, p] = _dot(vt[d, b, p], kmat)
            ys = {}
            for d, b, p in chains:
                t = j if d == 0 else SCAN_GROUP - 1 - j
                r_ref, kk_ref, v_ref, k_ref, w_ref, ka_ref, y_ref = dirs[d]
                s = s_ref[d, b, p]
                s = s * row(w_ref, d, b, p, t) - sk[d, b, p] * row(ka_ref, d, b, p, t) + vk[d, b, p]
                s_ref[d, b, p] = s
                y = _dot_nt(segl_ref[p], (s * row(r_ref, d, b, p, t)).astype(BF16))
                ys[d, b] = y if p == 0 else ys[d, b] + y
                if p == n_pairs - 1:
                    y_ref[b, bases[d] + t] = ys[d, b]
        return carry

    lax.fori_loop(0, tb // SCAN_GROUP, group, 0)


def _scan_selectors():
    n_pairs = GROUP_W // LANES
    bsel = np.zeros((SCAN_GROUP, 2 * SCAN_GROUP, LANES), np.float32)
    for h in range(2):
        for j in range(SCAN_GROUP):
            bsel[j, h * SCAN_GROUP + j, h * RWKV_HEAD:(h + 1) * RWKV_HEAD] = 1.0
    segl = np.zeros((n_pairs, SUBLANES, LANES), np.float32)
    for p in range(n_pairs):
        for h in range(2):
            segl[p, 2 * p + h, h * RWKV_HEAD:(h + 1) * RWKV_HEAD] = 1.0
    return jnp.asarray(bsel, F32), jnp.asarray(segl, BF16), _seg_matrix(LANES, RWKV_HEAD)


def rwkv_scan(r, kk, v, dirs, seq):
    n_batch, lc, gw = r.shape
    tb = SCAN_TB
    nx = seq // tb
    nc = (lc - seq) // tb
    fwd = lambda i: (0, jnp.where(i < nc, nx + i, i - nc), 0)
    bwd = lambda i: (0, jnp.where(i < nc, nx + nc - 1 - i, nx + nc - 1 - i), 0)
    blk = lambda m: pl.BlockSpec((n_batch, tb, gw), m)
    const = lambda a: pl.BlockSpec(a.shape, lambda i: (0,) * a.ndim)
    bsel, segl, segb = _scan_selectors()
    yblk = lambda m: pl.BlockSpec((n_batch, tb, RWKV_HEADS, RWKV_HEAD), lambda i: m(i) + (0,))
    yf, yb = pl.pallas_call(
        functools.partial(_scan_kernel, n_batch), name="rwkv_scan",
        out_shape=[jax.ShapeDtypeStruct((n_batch, lc, RWKV_HEADS, RWKV_HEAD), F32)] * 2,
        grid=(nx + nc,),
        in_specs=[blk(fwd)] * 6 + [blk(bwd)] * 6 + [const(bsel), const(segl), const(segb)],
        out_specs=[yblk(fwd), yblk(bwd)],
        scratch_shapes=[pltpu.VMEM((2, n_batch, gw // LANES, RWKV_HEAD, LANES), F32)],
        compiler_params=_cparams(("arbitrary",)),
    )(r, kk, v, *dirs[0], r, kk, v, *dirs[1], bsel, segl, segb)
    return yf.reshape(n_batch, lc, gw), yb.reshape(n_batch, lc, gw)


def _rwkv_readout_kernel(yf_ref, yb_ref, r_ref, k0_ref, k1_ref, v_ref, g_ref, rk_ref, lnw_ref, lnb_ref, seg_ref,
                         o_ref):
    seg = seg_ref[...]
    inv = 1.0 / RWKV_HEAD
    y = yf_ref[...] + yb_ref[...]
    mu = _segsum(y, seg) * inv
    dlt = y - mu
    var = _segsum(dlt * dlt, seg) * inv
    yn = dlt * lax.rsqrt(var + RWKV_GN_EPS) * lnw_ref[...] + lnb_ref[...]
    kbar = 0.5 * (k0_ref[...] + k1_ref[...])
    bonus = _segsum(r_ref[...] * kbar * rk_ref[...], seg) * v_ref[...]
    o_ref[...] = ((yn + bonus) * g_ref[...]).astype(o_ref.dtype)


def rwkv_readout(yf, yb, r, k0, k1, v, g, r_k, ln_w, ln_b):
    n_batch, lc, gw = r.shape
    seg = _seg_matrix(gw, RWKV_HEAD)
    row = pl.BlockSpec((None, TM, gw), lambda b, i: (b, i, 0))
    const = lambda a: pl.BlockSpec(a.shape, lambda b, i: (0,) * a.ndim)
    params = [r_k.reshape(1, gw), ln_w.reshape(1, gw), ln_b.reshape(1, gw), seg]
    return pl.pallas_call(
        _rwkv_readout_kernel, name="rwkv_readout",
        out_shape=jax.ShapeDtypeStruct((n_batch, lc, gw), BF16),
        grid=(n_batch, lc // TM),
        in_specs=[row] * 7 + [const(a) for a in params],
        out_specs=row,
        compiler_params=_cparams(("parallel", "arbitrary")),
    )(yf, yb, r, k0, k1, v, g, *params)


def _dft_tables(n):
    k = np.arange(n)
    ang = 2.0 * np.pi * ((k[:, None] * k[None, :]) % n) / n
    return np.cos(ang), np.sin(ang)


def _fft_a_kernel(nb, f_ref, gr_ref, gi_ref, zr_ref, zi_ref):
    n1 = gr_ref.shape[1]
    n2_total = f_ref.shape[0] // n1
    j = pl.program_id(2)
    for i in range(nb):
        rows = f_ref[pl.ds(j * nb + i, n1, stride=n2_total), :].astype(BF16)
        zr_ref[i * n1:(i + 1) * n1, :] = _dot(gr_ref[i], rows)
        zi_ref[i * n1:(i + 1) * n1, :] = _dot(gi_ref[i], rows)


def _fft_b_kernel(kb, scale, zr_ref, zi_ref, fc_ref, fs_ref, cc_ref, cs_ref, w_ref, b_ref, o_ref):
    n2 = fc_ref.shape[0]
    n1 = zr_ref.shape[0] // n2
    j = pl.program_id(2)
    fc, fs = fc_ref[...], fs_ref[...]
    for i in range(kb):
        k1 = j * kb + i
        zr = zr_ref[pl.ds(k1, n2, stride=n1), :].astype(BF16)
        zi = zi_ref[pl.ds(k1, n2, stride=n1), :].astype(BF16)
        xr = _dot(fc, zr) + _dot(fs, zi)
        xi = _dot(fc, zi) - _dot(fs, zr)
        re = (_dot(xr.astype(BF16), cc_ref[...]) + _dot(xi.astype(BF16), cs_ref[...])) * scale
        o_ref[pl.ds(k1, n2, stride=n1), :] = _dot(re.astype(BF16), w_ref[...].astype(BF16)) + b_ref[...]


def _dft_dense_kernel(scale, f_ref, fc_ref, fs_ref, cc_ref, cs_ref, w_ref, b_ref, o_ref):
    f = f_ref[...].astype(BF16)
    xr = _dot(fc_ref[...], f)
    xi = -_dot(fs_ref[...], f)
    re = (_dot(xr.astype(BF16), cc_ref[...]) + _dot(xi.astype(BF16), cs_ref[...])) * scale
    o_ref[...] = _dot(re.astype(BF16), w_ref[...].astype(BF16)) + b_ref[...]


def fourier_mix(pb, seq, w, bias):
    n_batch, lc, gw = pb.shape
    ctx = lc - seq
    ch = FOURIER_CH
    groups = FOURIER_GROUPS
    n2 = LANES
    n1 = seq // n2
    cc, cs = (jnp.asarray(t, BF16) for t in _dft_tables(ch))
    bias3 = bias.reshape(groups, 1, ch)

    k1 = np.arange(n1)[None, :, None]
    m1 = np.arange(n1)[None, None, :]
    m2 = np.arange(n2)[:, None, None]
    ang = 2.0 * np.pi * ((k1 * (n2 * m1 + m2)) % seq) / seq
    g_re = jnp.asarray(np.cos(ang), BF16)
    g_im = jnp.asarray(-np.sin(ang), BF16)
    nb = min(16, n2)
    slab = pl.BlockSpec((None, seq, ch), lambda b, g, j: (b, 0, g))
    gspec = pl.BlockSpec((nb, n1, n1), lambda b, g, j: (j, 0, 0))
    zspec = pl.BlockSpec((None, None, nb * n1, ch), lambda b, g, j: (b, g, j, 0))
    zr, zi = pl.pallas_call(
        functools.partial(_fft_a_kernel, nb), name="fft_stage_a",
        out_shape=[jax.ShapeDtypeStruct((n_batch, groups, seq, ch), F32)] * 2,
        grid=(n_batch, groups, n2 // nb),
        in_specs=[slab, gspec, gspec],
        out_specs=[zspec, zspec],
        compiler_params=_cparams(("parallel", "parallel", "arbitrary")),
    )(pb, g_re, g_im)

    fc, fs = (jnp.asarray(t, BF16) for t in _dft_tables(n2))
    kb = min(16, n1)
    scale = 1.0 / np.sqrt(float(seq) * ch)
    zslab = pl.BlockSpec((None, None, seq, ch), lambda b, g, j: (b, g, 0, 0), pipeline_mode=pl.Buffered(1))
    const = lambda a: pl.BlockSpec(a.shape, lambda b, g, j: (0,) * a.ndim)
    wspec = pl.BlockSpec((None, ch, ch), lambda b, g, j: (g, 0, 0))
    bspec = pl.BlockSpec((None, 1, ch), lambda b, g, j: (g, 0, 0))
    y_x = pl.pallas_call(
        functools.partial(_fft_b_kernel, kb, scale), name="fft_stage_b",
        out_shape=jax.ShapeDtypeStruct((n_batch, seq, gw), F32),
        grid=(n_batch, groups, n1 // kb),
        in_specs=[zslab, zslab, const(fc), const(fs), const(cc), const(cs), wspec, bspec],
        out_specs=pl.BlockSpec((None, seq, ch), lambda b, g, j: (b, 0, g)),
        compiler_params=_cparams(("parallel", "parallel", "arbitrary")),
    )(zr, zi, fc, fs, cc, cs, w, bias3)

    fcc, fsc = (jnp.asarray(t, BF16) for t in _dft_tables(ctx))
    const2 = lambda a: pl.BlockSpec(a.shape, lambda b, g: (0,) * a.ndim)
    ctx_tile = seq // ctx
    y_c = pl.pallas_call(
        functools.partial(_dft_dense_kernel, 1.0 / np.sqrt(float(ctx) * ch)), name="dft_ctx",
        out_shape=jax.ShapeDtypeStruct((n_batch, ctx, gw), F32),
        grid=(n_batch, groups),
        in_specs=[pl.BlockSpec((None, ctx, ch), lambda b, g: (b, ctx_tile, g)),
                  const2(fcc), const2(fsc), const2(cc), const2(cs),
                  pl.BlockSpec((None, ch, ch), lambda b, g: (g, 0, 0)),
                  pl.BlockSpec((None, 1, ch), lambda b, g: (g, 0, 0))],
        out_specs=pl.BlockSpec((None, ctx, ch), lambda b, g: (b, 0, g)),
        compiler_params=_cparams(("parallel", "arbitrary")),
    )(pb, fcc, fsc, cc, cs, w, bias3)
    return jnp.concatenate([y_x, y_c], axis=1)


def _gmlp_kernel(pc_ref, norm_ref, ws_ref, bs_ref, o_ref):
    gw = GROUP_W
    x = pc_ref[...]
    z = 0.5 * x * (1.0 + jnp.tanh(0.7978845608028654 * (x + 0.044715 * (x * x * x))))
    tm = x.shape[0]
    for g in range(GMLP_GROUPS):
        u = z[:, g * GMLP_CH:(g + 1) * GMLP_CH]
        v = z[:, gw + g * GMLP_CH:gw + (g + 1) * GMLP_CH]
        v = v * lax.rsqrt(jnp.mean(v * v, axis=-1, keepdims=True) + EPS) * norm_ref[g:g + 1, :]
        v = v.astype(BF16)
        ws = ws_ref[g].astype(BF16)
        for c in range(tm // GMLP_CHUNK):
            rows = slice(c * GMLP_CHUNK, (c + 1) * GMLP_CHUNK)
            f = _dot(ws, v[rows]) + bs_ref[g]
            o_ref[rows, g * GMLP_CH:(g + 1) * GMLP_CH] = (u[rows] * f).astype(o_ref.dtype)


def gmlp_mix(pc, norm_g, ws, bs):
    n_batch, lc, c = pc.shape
    bs_b = jnp.broadcast_to(bs[:, :, None], bs.shape + (GMLP_CH,))
    const = lambda a: pl.BlockSpec(a.shape, lambda b, i: (0,) * a.ndim)
    return pl.pallas_call(
        _gmlp_kernel, name="gmlp",
        out_shape=jax.ShapeDtypeStruct((n_batch, lc, GROUP_W), BF16),
        grid=(n_batch, lc // TM),
        in_specs=[pl.BlockSpec((None, TM, c), lambda b, i: (b, i, 0)), const(norm_g), const(ws), const(bs_b)],
        out_specs=pl.BlockSpec((None, TM, GROUP_W), lambda b, i: (b, i, 0)),
        compiler_params=_cparams(("parallel", "arbitrary")),
    )(pc, norm_g, ws, bs_b)


def _rope_tables(seq, ctx):
    half = ATT_HEAD // 4
    inv_freq = ROPE_BASE ** (-np.arange(half, dtype=np.float64) / half)
    pos = np.arange(seq)
    ang_r = (pos // GRID_W)[:, None] * inv_freq[None, :]
    ang_c = (pos % GRID_W)[:, None] * inv_freq[None, :]
    cos = np.concatenate([np.cos(ang_r)] * 2 + [np.cos(ang_c)] * 2, axis=1)
    sin = np.concatenate([-np.sin(ang_r), np.sin(ang_r), -np.sin(ang_c), np.sin(ang_c)], axis=1)
    cos = np.concatenate([cos, np.ones((ctx, ATT_HEAD))], axis=0)
    sin = np.concatenate([sin, np.zeros((ctx, ATT_HEAD))], axis=0)
    return jnp.asarray(np.tile(cos, (1, 2)), F32), jnp.asarray(np.tile(sin, (1, 2)), F32)


def _rope(t, cos, sin):
    n = t.shape[1]
    q = ATT_HEAD // 4
    lane = lax.broadcasted_iota(jnp.int32, t.shape, 1)
    swapped = jnp.where((lane % (2 * q)) < q, pltpu.roll(t, n - q, 1), pltpu.roll(t, q, 1))
    return t * cos + swapped * sin


def _attn_prep_kernel(pd_ref, cos_ref, sin_ref, qg_ref, kg_ref, seg_ref, q_o, k_o, v_o):
    gw = GROUP_W
    kvw = ATT_KV_HEADS * ATT_HEAD
    seg = seg_ref[...]
    inv = 1.0 / ATT_HEAD
    cos, sin = cos_ref[...], sin_ref[...]
    q = pd_ref[:, 0:gw]
    q = q * lax.rsqrt(_segsum(q * q, seg) * inv + EPS) * qg_ref[...]
    q = _rope(q, jnp.concatenate([cos] * (gw // kvw), axis=1), jnp.concatenate([sin] * (gw // kvw), axis=1))
    q_o[...] = (q * (ATT_HEAD ** -0.5)).astype(q_o.dtype)
    k = pd_ref[:, gw:gw + kvw]
    k = k * lax.rsqrt(_segsum(k * k, seg[:kvw, :kvw]) * inv + EPS) * kg_ref[...]
    k = _rope(k, cos, sin)
    v = pd_ref[:, gw + kvw:gw + 2 * kvw]
    lane = lax.broadcasted_iota(jnp.int32, k.shape, 1)

    def variants(t):
        h0 = jnp.where(lane < ATT_HEAD, t, 0.0)
        h1 = jnp.where(lane >= ATT_HEAD, t, 0.0)
        return jnp.concatenate([h0, pltpu.roll(h0, ATT_HEAD, 1), pltpu.roll(h1, ATT_HEAD, 1), h1], axis=1)

    k_o[...] = variants(k).astype(k_o.dtype)
    v_o[...] = variants(v).astype(v_o.dtype)


def attention_prepare(pd, seq, q_gain, k_gain):
    n_batch, lc, c = pd.shape
    gw = GROUP_W
    kvw = ATT_KV_HEADS * ATT_HEAD
    cos, sin = _rope_tables(seq, lc - seq)
    seg = _seg_matrix(gw, ATT_HEAD)
    qg = jnp.tile(q_gain, ATT_Q_HEADS).reshape(1, gw)
    kg = jnp.tile(k_gain, ATT_KV_HEADS).reshape(1, kvw)
    const = lambda a: pl.BlockSpec(a.shape, lambda b, i: (0,) * a.ndim)
    tab = pl.BlockSpec((TM, kvw), lambda b, i: (i, 0))
    row = lambda n: pl.BlockSpec((None, TM, n), lambda b, i: (b, i, 0))
    return pl.pallas_call(
        _attn_prep_kernel, name="attn_prep",
        out_shape=[jax.ShapeDtypeStruct((n_batch, lc, gw), BF16)] * 3,
        grid=(n_batch, lc // TM),
        in_specs=[row(c), tab, tab, const(qg), const(kg), const(seg)],
        out_specs=[row(gw)] * 3,
        compiler_params=_cparams(("parallel", "arbitrary")),
    )(pd, cos, sin, qg, kg, seg)


def _attn_kernel(nb, sink_ref, q_ref, kp_ref, kc_ref, kn_ref, kx_ref, vp_ref, vc_ref, vn_ref, vx_ref, o_ref):
    i = pl.program_id(1)
    blk = q_ref.shape[0]
    is_lat = i < nb
    rowi = lax.broadcasted_iota(jnp.int32, (blk, blk), 0)
    coli = lax.broadcasted_iota(jnp.int32, (blk, blk), 1)
    m_prev = (coli >= rowi) & is_lat & (i >= 1)
    m_cur = jnp.broadcast_to(is_lat, (blk, blk))
    m_next = (coli <= rowi) & (i < nb - 1)
    for p in range(ATT_Q_HEADS // 2):
        q = q_ref[:, p * LANES:(p + 1) * LANES]
        acc = None
        for par in range(2):
            h = 2 * p + par
            g = h // (ATT_Q_HEADS // ATT_KV_HEADS)
            col = slice((2 * g + par) * LANES, (2 * g + par + 1) * LANES)
            sink = sink_ref[h]
            s1 = jnp.where(m_prev, _dot_nt(q, kp_ref[:, col]), NEG_INF)
            s2 = jnp.where(m_cur, _dot_nt(q, kc_ref[:, col]), NEG_INF)
            s3 = jnp.where(m_next, _dot_nt(q, kn_ref[:, col]), NEG_INF)
            sx = _dot_nt(q, kx_ref[:, col])
            m = jnp.maximum(jnp.maximum(jnp.max(jnp.maximum(jnp.maximum(s1, s2), s3), axis=-1, keepdims=True),
                                        jnp.max(sx, axis=-1, keepdims=True)), sink)
            p1, p2, p3, px = jnp.exp(s1 - m), jnp.exp(s2 - m), jnp.exp(s3 - m), jnp.exp(sx - m)
            den = (jnp.sum(p1 + p2 + p3, axis=-1, keepdims=True) + jnp.sum(px, axis=-1, keepdims=True)
                   + jnp.exp(sink - m))
            o = (_dot(p1.astype(BF16), vp_ref[:, col]) + _dot(p2.astype(BF16), vc_ref[:, col])
                 + _dot(p3.astype(BF16), vn_ref[:, col]) + _dot(px.astype(BF16), vx_ref[:, col])) / den
            acc = o if acc is None else acc + o
        o_ref[:, p * LANES:(p + 1) * LANES] = acc.astype(o_ref.dtype)


def window_attention(q, kv, vv, sink, seq):
    n_batch, lc, gw = q.shape
    blk = ATT_BLOCK
    nb = seq // blk
    ctx = lc - seq
    wide = kv.shape[-1]
    qspec = pl.BlockSpec((None, blk, gw), lambda b, i, s: (b, i, 0))
    prev = pl.BlockSpec((None, blk, wide), lambda b, i, s: (b, jnp.clip(i - 1, 0, nb - 1), 0))
    cur = pl.BlockSpec((None, blk, wide), lambda b, i, s: (b, jnp.minimum(i, nb - 1), 0))
    nxt = pl.BlockSpec((None, blk, wide), lambda b, i, s: (b, jnp.clip(i + 1, 0, nb - 1), 0))
    cx = pl.BlockSpec((None, ctx, wide), lambda b, i, s: (b, seq // ctx, 0))
    return pl.pallas_call(
        functools.partial(_attn_kernel, nb), name="window_attn",
        out_shape=jax.ShapeDtypeStruct((n_batch, lc, gw), BF16),
        grid_spec=pltpu.PrefetchScalarGridSpec(
            num_scalar_prefetch=1, grid=(n_batch, lc // blk),
            in_specs=[qspec, prev, cur, nxt, cx, prev, cur, nxt, cx],
            out_specs=qspec),
        compiler_params=_cparams(("parallel", "arbitrary")),
    )(sink.astype(F32), q, kv, kv, kv, kv, vv, vv, vv, vv)


def _top2_rows(val, iota):
    m1 = jnp.max(val, axis=0, keepdims=True)
    i1 = jnp.min(jnp.where(val == m1, iota, SUBLANES), axis=0, keepdims=True)
    rest = jnp.where(iota == i1, -jnp.inf, val)
    m2 = jnp.max(rest, axis=0, keepdims=True)
    i2 = jnp.min(jnp.where(rest == m2, iota, SUBLANES), axis=0, keepdims=True)
    return m1, m2, i1, i2


def _outproj_kernel(ya_ref, yb_ref, yc_ref, yd_ref, x_ref, mod_ref, g_ref, wo_ref, wr_ref, rb_ref,
                    xo_ref, h_ref, e_ref, gw_ref):
    d = x_ref.shape[-1]
    gw = GROUP_W
    acc = None
    for n, y_ref in enumerate((ya_ref, yb_ref, yc_ref, yd_ref)):
        part = _dot(y_ref[...].astype(BF16), wo_ref[n * gw:(n + 1) * gw, :])
        acc = part if acc is None else acc + part
    x = x_ref[...] + mod_ref[:, 2 * d:3 * d] * acc
    xo_ref[...] = x
    y = x * lax.rsqrt(jnp.mean(x * x, axis=-1, keepdims=True) + EPS) * g_ref[...]
    h = y * (1.0 + mod_ref[:, 4 * d:5 * d]) + mod_ref[:, 3 * d:4 * d]
    h_ref[...] = h.astype(h_ref.dtype)
    hh, hl = _split(h)
    wh, wl = _split(wr_ref[...])
    logits = _dot_nt(wh, hh) + _dot_nt(wh, hl) + _dot_nt(wl, hh)
    scores = _sigmoid(logits)
    sel = scores + rb_ref[...]
    tm = x.shape[0]
    iota = lax.broadcasted_iota(jnp.int32, (EXPERTS_PER_GROUP, tm), 0)
    best_val, best = None, None
    for grp in range(N_EXPERT_GROUPS):
        m1, m2, _, _ = _top2_rows(sel[grp * EXPERTS_PER_GROUP:(grp + 1) * EXPERTS_PER_GROUP], iota)
        gs = m1 + m2
        if grp == 0:
            best_val, best = gs, jnp.zeros_like(gs, dtype=jnp.int32)
        else:
            upd = gs > best_val
            best_val = jnp.where(upd, gs, best_val)
            best = jnp.where(upd, grp, best)
    sel_in = jnp.zeros((EXPERTS_PER_GROUP, tm), F32)
    sc_in = jnp.zeros((EXPERTS_PER_GROUP, tm), F32)
    for grp in range(N_EXPERT_GROUPS):
        rows = slice(grp * EXPERTS_PER_GROUP, (grp + 1) * EXPERTS_PER_GROUP)
        hit = best == grp
        sel_in = jnp.where(hit, sel[rows], sel_in)
        sc_in = jnp.where(hit, scores[rows], sc_in)
    _, _, i1, i2 = _top2_rows(sel_in, iota)
    w1 = jnp.sum(jnp.where(iota == i1, sc_in, 0.0), axis=0, keepdims=True)
    w2 = jnp.sum(jnp.where(iota == i2, sc_in, 0.0), axis=0, keepdims=True)
    tot = w1 + w2
    e_ref[0:1, :] = best * EXPERTS_PER_GROUP + i1
    e_ref[1:2, :] = best * EXPERTS_PER_GROUP + i2
    gw_ref[0:1, :] = w1 / tot
    gw_ref[1:2, :] = w2 / tot


def out_projection_router(ys, stream, mod, gain2, w_out, w_router, router_bias, seq):
    n_batch, lc, d = stream.shape
    gw = GROUP_W
    ctx_tile = seq // TM
    wr_t = w_router.T
    rb = jnp.broadcast_to(router_bias.astype(F32)[:, None], (N_EXPERTS, TM))
    row = lambda n: pl.BlockSpec((None, TM, n), lambda b, i: (b, i, 0))
    const = lambda a: pl.BlockSpec(a.shape, lambda b, i: (0,) * a.ndim)
    lane_row = pl.BlockSpec((None, TOP_K, TM), lambda b, i: (b, 0, i))
    return pl.pallas_call(
        _outproj_kernel, name="out_proj_router",
        out_shape=[jax.ShapeDtypeStruct((n_batch, lc, d), F32), jax.ShapeDtypeStruct((n_batch, lc, d), BF16),
                   jax.ShapeDtypeStruct((n_batch, TOP_K, lc), jnp.int32),
                   jax.ShapeDtypeStruct((n_batch, TOP_K, lc), F32)],
        grid=(n_batch, lc // TM),
        in_specs=[row(gw)] * 4 + [row(d), pl.BlockSpec((None, 1, mod.shape[-1]), _mod_index(n_batch, ctx_tile)),
                                  pl.BlockSpec((1, d), lambda b, i: (0, 0)),
                                  pl.BlockSpec(w_out.shape, lambda b, i: (0, 0), pipeline_mode=pl.Buffered(1)),
                                  const(wr_t), const(rb)],
        out_specs=[row(d), row(d), lane_row, lane_row],
        compiler_params=_cparams(("parallel", "arbitrary")),
    )(*ys, stream, mod, gain2.reshape(1, d), w_out, wr_t, rb)


def _expert_kernel(be_ref, nu_ref, x_ref, wg_ref, wu_ref, wd_ref, o_ref, wg_s, wu_s, wd_s):
    i = pl.program_id(0)
    prev = be_ref[jnp.maximum(i - 1, 0)]
    used = i < nu_ref[0]

    @pl.when(used & ((i == 0) | (be_ref[i] != prev)))
    def _():
        wg_s[...] = wg_ref[...].astype(BF16)
        wu_s[...] = wu_ref[...].astype(BF16)
        wd_s[...] = wd_ref[...].astype(BF16)

    @pl.when(used)
    def _():
        x = x_ref[...]
        a = _dot(x, wg_s[...])
        hb = (a * _sigmoid(a)) * _dot(x, wu_s[...])
        o_ref[...] = _dot(hb.astype(BF16), wd_s[...])

    @pl.when(jnp.logical_not(used))
    def _():
        o_ref[...] = jnp.zeros_like(o_ref)


def expert_ffn(xs, block_e, n_used, wg, wu, wd):
    rows, d = xs.shape
    n_blocks = rows // MOE_BLOCK
    de = wg.shape[-1]
    return pl.pallas_call(
        _expert_kernel, name="moe_experts",
        out_shape=jax.ShapeDtypeStruct((rows, d), F32),
        grid_spec=pltpu.PrefetchScalarGridSpec(
            num_scalar_prefetch=2, grid=(n_blocks,),
            in_specs=[pl.BlockSpec((MOE_BLOCK, d), lambda i, be, nu: (i, 0)),
                      pl.BlockSpec((None, d, de), lambda i, be, nu: (be[i], 0, 0)),
                      pl.BlockSpec((None, d, de), lambda i, be, nu: (be[i], 0, 0)),
                      pl.BlockSpec((None, de, d), lambda i, be, nu: (be[i], 0, 0))],
            out_specs=pl.BlockSpec((MOE_BLOCK, d), lambda i, be, nu: (i, 0)),
            scratch_shapes=[pltpu.VMEM((d, de), BF16), pltpu.VMEM((d, de), BF16), pltpu.VMEM((de, d), BF16)]),
        compiler_params=_cparams(("arbitrary",)),
    )(block_e, n_used, xs, wg, wu, wd)


def _residual_kernel(x_ref, y_ref, mod_ref, o_ref):
    d = x_ref.shape[-1]
    o_ref[...] = x_ref[...] + mod_ref[:, 5 * d:6 * d] * y_ref[...]


def gated_residual(stream, y, mod, seq):
    n_batch, lc, d = stream.shape
    row = pl.BlockSpec((None, TM, d), lambda b, i: (b, i, 0))
    return pl.pallas_call(
        _residual_kernel, name="moe_residual",
        out_shape=jax.ShapeDtypeStruct((n_batch, lc, d), F32),
        grid=(n_batch, lc // TM),
        in_specs=[row, row, pl.BlockSpec((None, 1, mod.shape[-1]), _mod_index(n_batch, seq // TM))],
        out_specs=row,
        compiler_params=_cparams(("parallel", "arbitrary")),
    )(stream, y, mod)


def moe_ffn(h2, eidx, gates, wg, wu, wd):
    n_batch, lc, d = h2.shape
    n_tok = n_batch * lc
    n_assign = n_tok * TOP_K
    e = jnp.swapaxes(eidx, 1, 2).reshape(n_assign)
    gate = jnp.swapaxes(gates, 1, 2).reshape(n_assign)
    onehot = (e[:, None] == jnp.arange(N_EXPERTS, dtype=jnp.int32)[None, :]).astype(jnp.int32)
    cum = jnp.cumsum(onehot, axis=0)
    rank = jnp.take_along_axis(cum, e[:, None], axis=1)[:, 0] - 1
    counts = cum[-1]
    padded = (counts + MOE_BLOCK - 1) // MOE_BLOCK * MOE_BLOCK
    pad_end = jnp.cumsum(padded)
    dest = (pad_end - padded)[e] + rank
    n_blocks = n_assign // MOE_BLOCK + N_EXPERTS
    tok = jnp.arange(n_assign, dtype=jnp.int32) // TOP_K
    row_tok = jnp.zeros((n_blocks * MOE_BLOCK,), jnp.int32).at[dest].set(tok)
    block_e = jnp.minimum(jnp.searchsorted(pad_end, jnp.arange(n_blocks, dtype=jnp.int32) * MOE_BLOCK, side='right'),
                          N_EXPERTS - 1).astype(jnp.int32)
    n_used = (pad_end[-1:] // MOE_BLOCK).astype(jnp.int32)
    xs = h2.reshape(n_tok, d)[row_tok]
    ys = expert_ffn(xs, block_e, n_used, wg, wu, wd)
    contrib = ys[dest] * gate[:, None]
    return contrib.reshape(n_tok, TOP_K, d).sum(axis=1).reshape(n_batch, lc, d)


def kernel(x, c, ctx, c_ctx, norm1, norm2, w_ada, b_ada, w_in, w_out, rwkv_conv, rwkv_w0, rwkv_w_up, rwkv_a0,
           rwkv_a_up, rwkv_g_up, rwkv_k_k, rwkv_k_a, rwkv_r_k, rwkv_ln_w, rwkv_ln_b, rwkv_v0, rwkv_v_down,
           rwkv_v_up, fourier_w, fourier_b, gmlp_norm, gmlp_ws, gmlp_bs, attn_q_norm, attn_k_norm, attn_sink,
           w_router, router_bias, w_e_gate, w_e_up, w_e_down):
    n_batch, seq, d = x.shape
    depth = w_in.shape[0]
    stream = jnp.concatenate([x, ctx], axis=1)
    c_all = jnp.zeros((SUBLANES, d), F32).at[:n_batch].set(c).at[n_batch].set(c_ctx)
    col_splits = np.cumsum([A_COLS, B_COLS, C_COLS])
    v_first = None
    for l in range(depth):
        mod = ada_modulation(c_all, w_ada[l], b_ada[l]).reshape(SUBLANES, 1, 6 * d)
        w_parts = [w.astype(BF16) for w in jnp.split(w_in[l], col_splits, axis=1)]
        pa, pb, pc, pd = in_projection(stream, mod, norm1[l], w_parts, seq)

        vres = None if l == 0 else (rwkv_v0[l - 1], rwkv_v_down[l - 1], rwkv_v_up[l - 1])
        r, kk, v, k0, wd0, kka0, k1, wd1, kka1, g = rwkv_prepare(
            pa, seq, rwkv_conv[l], rwkv_w0[l], rwkv_w_up[l], rwkv_a0[l], rwkv_a_up[l], rwkv_g_up[l],
            rwkv_k_k[l], rwkv_k_a[l], v_first, vres)
        if l == 0:
            v_first = v
        yf, yb = rwkv_scan(r, kk, v, ((k0, wd0, kka0), (k1, wd1, kka1)), seq)
        y_a = rwkv_readout(yf, yb, r, k0, k1, v, g, rwkv_r_k[l].reshape(-1), rwkv_ln_w[l], rwkv_ln_b[l])

        y_b = fourier_mix(pb, seq, fourier_w[l], fourier_b[l])
        y_c = gmlp_mix(pc, gmlp_norm[l], gmlp_ws[l], gmlp_bs[l])
        q, kv, vv = attention_prepare(pd, seq, attn_q_norm[l], attn_k_norm[l])
        y_d = window_attention(q, kv, vv, attn_sink[l], seq)

        stream, h2, eidx, gates = out_projection_router(
            (y_a, y_b, y_c, y_d), stream, mod, norm2[l], w_out[l].astype(BF16), w_router, router_bias, seq)
        y_moe = moe_ffn(h2, eidx, gates, w_e_gate[l], w_e_up[l], w_e_down[l])
        stream = gated_residual(stream, y_moe, mod, seq)
    return stream[:, :seq]
```

```python
import functools

import numpy as np
import jax
import jax.numpy as jnp
from jax import lax
from jax.experimental import pallas as pl
from jax.experimental.pallas import tpu as pltpu

F32 = jnp.float32
BF16 = jnp.bfloat16

D_MODEL = 2048
DEPTH = 2
GRID_W = 64
EPS = 1e-6
NEG_INF = -1e30
GROUP_W = 512
RWKV_HEAD = 64
RWKV_HEADS = 8
DECAY_LORA = 64
ICLR_LORA = 64
GATE_LORA = 128
VRES_LORA = 32
RWKV_GN_EPS = 64e-5
FOURIER_GROUPS = 4
FOURIER_CH = 128
GMLP_GROUPS = 4
GMLP_CH = 128
GMLP_CHUNK = 128
ATT_HEAD = 64
ATT_Q_HEADS = 8
ATT_KV_HEADS = 2
WINDOW = 128
ATT_BLOCK = 128
ROPE_BASE = 10000.0
N_EXPERTS = 64
N_EXPERT_GROUPS = 8
EXPERTS_PER_GROUP = 8
TOP_K = 2
D_EXPERT = 512
MOE_BLOCK = 128
A_COLS = 1920
B_COLS = 512
C_COLS = 1024
D_COLS = 768

LANES = 128
SUBLANES = 8
TM = 256
SCAN_TB = 128
SCAN_GROUP = SUBLANES
VMEM_LIMIT = 56 * 1024 * 1024


def _cparams(sem, vmem=VMEM_LIMIT):
    return pltpu.CompilerParams(dimension_semantics=sem, vmem_limit_bytes=vmem)


def _dot(a, b):
    return jnp.dot(a, b, preferred_element_type=F32)


def _dot_nt(a, b):
    return lax.dot_general(a, b, (((1,), (1,)), ((), ())), preferred_element_type=F32)


def _dot_tn(a, b):
    return lax.dot_general(a, b, (((0,), (0,)), ((), ())), preferred_element_type=F32)


def _split(x):
    hi = x.astype(BF16)
    lo = (x - hi.astype(F32)).astype(BF16)
    return hi, lo


def _dot3(a, b):
    ah, al = _split(a)
    bh, bl = _split(b)
    return _dot(ah, bh) + _dot(ah, bl) + _dot(al, bh)


def _segsum(x, seg):
    xh, xl = _split(x)
    return _dot(xh, seg) + _dot(xl, seg)


def _sigmoid(x):
    return 1.0 / (1.0 + jnp.exp(-x))


def _seg_matrix(width, seg):
    i = np.arange(width) // seg
    return jnp.asarray((i[:, None] == i[None, :]).astype(np.float32), dtype=BF16)


def _ada_kernel(c_ref, w_ref, b_ref, o_ref):
    c = c_ref[...]
    a = (c * _sigmoid(c)).astype(BF16)
    o_ref[...] = _dot(a, w_ref[...].astype(BF16)) + b_ref[...]


def ada_modulation(c_all, w_ada, b_ada, tn=1024):
    rows, d = c_all.shape
    n = w_ada.shape[1]
    return pl.pallas_call(
        _ada_kernel, name="ada_mod",
        out_shape=jax.ShapeDtypeStruct((rows, n), F32),
        grid=(n // tn,),
        in_specs=[pl.BlockSpec((rows, d), lambda j: (0, 0)),
                  pl.BlockSpec((d, tn), lambda j: (0, j)),
                  pl.BlockSpec((1, tn), lambda j: (0, j))],
        out_specs=pl.BlockSpec((rows, tn), lambda j: (0, j)),
        compiler_params=_cparams(("arbitrary",)),
    )(c_all, w_ada, b_ada.reshape(1, n))


def _inproj_kernel(x_ref, mod_ref, g_ref, wa_ref, wb_ref, wc_ref, wd_ref, pa_ref, pb_ref, pc_ref, pd_ref):
    d = x_ref.shape[-1]
    x = x_ref[...]
    y = x * lax.rsqrt(jnp.mean(x * x, axis=-1, keepdims=True) + EPS) * g_ref[...]
    sh = mod_ref[:, 0:d]
    sc = mod_ref[:, d:2 * d]
    h = (y * (1.0 + sc) + sh).astype(BF16)
    pa_ref[...] = _dot(h, wa_ref[...])
    pb_ref[...] = _dot(h, wb_ref[...])
    pc_ref[...] = _dot(h, wc_ref[...])
    pd_ref[...] = _dot(h, wd_ref[...])


def _mod_index(n_batch, ctx_tile):
    return lambda b, i: (jnp.where(i >= ctx_tile, n_batch, b), 0, 0)


def in_projection(stream, mod, gain, w_parts, seq):
    n_batch, lc, d = stream.shape
    ctx_tile = seq // TM
    resident = lambda w: pl.BlockSpec(w.shape, lambda b, i: (0, 0), pipeline_mode=pl.Buffered(1))
    row = lambda n: pl.BlockSpec((None, TM, n), lambda b, i: (b, i, 0))
    return pl.pallas_call(
        _inproj_kernel, name="in_proj",
        out_shape=[jax.ShapeDtypeStruct((n_batch, lc, w.shape[1]), F32) for w in w_parts],
        grid=(n_batch, lc // TM),
        in_specs=[row(d),
                  pl.BlockSpec((None, 1, mod.shape[-1]), _mod_index(n_batch, ctx_tile)),
                  pl.BlockSpec((1, d), lambda b, i: (0, 0))] + [resident(w) for w in w_parts],
        out_specs=[row(w.shape[1]) for w in w_parts],
        compiler_params=_cparams(("parallel", "arbitrary")),
    )(stream, mod, gain.reshape(1, d), *w_parts)


def _rwkv_prep_kernel(has_vres, ctx_tile, *refs):
    (pa_ref, hp_ref, hn_ref, conv_ref, w0_ref, wup_ref, a0_ref, aup_ref, gup_ref, kk_par_ref, ka_par_ref,
     seg_ref) = refs[:12]
    rest = refs[12:]
    if has_vres:
        vf_ref, v0_ref, vdn_ref, vup_ref = rest[:4]
        rest = rest[4:]
    r_o, kk_o, v_o, k0_o, w0_o, kka0_o, k1_o, w1_o, kka1_o, g_o = rest
    gw = GROUP_W
    x = pa_ref[...]
    tm = x.shape[0]
    i = pl.program_id(1)
    first = (i == 0) | (i == ctx_tile)
    last = (i == ctx_tile - 1) | (i == pl.num_programs(1) - 1)
    halo_prev = jnp.where(first, 0.0, hp_ref[SUBLANES - 1:SUBLANES, :])
    halo_next = jnp.where(last, 0.0, hn_ref[0:1, :])
    row = lax.broadcasted_iota(jnp.int32, x.shape, 0)
    x_prev = jnp.where(row == 0, halo_prev, pltpu.roll(x, 1, 0))
    x_next = jnp.where(row == tm - 1, halo_next, pltpu.roll(x, tm - 1, 0))
    y = x_prev * conv_ref[0:1, :] + x * conv_ref[1:2, :] + x_next * conv_ref[2:3, :]
    r = y[:, 0:gw]
    k = y[:, gw:2 * gw]
    v = y[:, 2 * gw:3 * gw]
    wl = y[:, 3 * gw:3 * gw + 2 * DECAY_LORA]
    al = y[:, 3 * gw + 2 * DECAY_LORA:3 * gw + 2 * DECAY_LORA + 2 * ICLR_LORA]
    gl = y[:, 3 * gw + 2 * DECAY_LORA + 2 * ICLR_LORA:]
    if has_vres:
        mix = _sigmoid(v0_ref[...] + _dot3(_dot3(v, vdn_ref[...]), vup_ref[...]))
        v = v + (vf_ref[...] - v) * mix
    kk = k * kk_par_ref[...]
    nrm = jnp.maximum(jnp.sqrt(_segsum(kk * kk, seg_ref[...])), 1e-12)
    kk = kk / nrm
    r_o[...] = r
    kk_o[...] = kk
    v_o[...] = v
    tanh_wl = jnp.tanh(wl)
    for d, (k_o, w_o, kka_o) in enumerate(((k0_o, w0_o, kka0_o), (k1_o, w1_o, kka1_o))):
        w_raw = w0_ref[d:d + 1, :] + _dot3(tanh_wl, wup_ref[d])
        z = -w_raw
        softplus = jnp.maximum(z, 0.0) + jnp.log(1.0 + jnp.exp(-jnp.abs(z)))
        w_o[...] = jnp.exp(-jnp.exp(-softplus - 0.5))
        a = _sigmoid(a0_ref[d:d + 1, :] + _dot3(al, aup_ref[d]))
        k_o[...] = k * (1.0 + (a - 1.0) * ka_par_ref[...])
        kka_o[...] = kk * a
    g_o[...] = _dot3(_sigmoid(gl), gup_ref[...])


def rwkv_prepare(pa, seq, conv_w, w0, w_up, a0, a_up, g_up, k_k, k_a, v_first=None, vres=None):
    n_batch, lc, c = pa.shape
    gw = GROUP_W
    sub_per_tile = TM // SUBLANES
    n_sub = lc // SUBLANES
    pad_rows = lambda w, d, n: jnp.zeros((2 * n, gw), F32).at[d * n:(d + 1) * n].set(w)
    wup = jnp.stack([pad_rows(w_up[d], d, DECAY_LORA) for d in range(2)])
    aup = jnp.stack([pad_rows(a_up[d], d, ICLR_LORA) for d in range(2)])
    seg = _seg_matrix(gw, RWKV_HEAD)
    const = lambda a: pl.BlockSpec(a.shape, lambda b, i: (0,) * a.ndim)
    row = lambda n: pl.BlockSpec((None, TM, n), lambda b, i: (b, i, 0))
    halo_prev = pl.BlockSpec((None, SUBLANES, c), lambda b, i: (b, jnp.maximum(i * sub_per_tile - 1, 0), 0))
    halo_next = pl.BlockSpec((None, SUBLANES, c), lambda b, i: (b, jnp.minimum((i + 1) * sub_per_tile, n_sub - 1), 0))
    args = [pa, pa, pa, conv_w, w0, wup, a0, aup, g_up, k_k.reshape(1, gw), k_a.reshape(1, gw), seg]
    specs = [row(c), halo_prev, halo_next] + [const(a) for a in args[3:]]
    has_vres = vres is not None
    if has_vres:
        v0, v_down, v_up = vres
        vdn = jnp.zeros((gw, LANES), F32).at[:, :VRES_LORA].set(v_down)
        vup = jnp.zeros((LANES, gw), F32).at[:VRES_LORA].set(v_up)
        extra = [v_first, v0.reshape(1, gw), vdn, vup]
        args += extra
        specs += [row(gw)] + [const(a) for a in extra[1:]]
    return pl.pallas_call(
        functools.partial(_rwkv_prep_kernel, has_vres, seq // TM), name="rwkv_prep",
        out_shape=[jax.ShapeDtypeStruct((n_batch, lc, gw), F32)] * 10,
        grid=(n_batch, lc // TM),
        in_specs=specs,
        out_specs=[row(gw)] * 10,
        compiler_params=_cparams(("parallel", "arbitrary")),
    )(*args)


def _scan_kernel(n_batch, rf, kkf, vf, kf, wf, kaf, rb, kkb, vb, kb, wb, kab, bsel_ref, segbd_ref,
                 yf_ref, yb_ref, s_ref, vk_ref, yt_ref):
    tb = rf.shape[1]
    n_pairs = GROUP_W // LANES
    half = RWKV_HEAD

    @pl.when(pl.program_id(0) == 0)
    def _():
        s_ref[...] = jnp.zeros_like(s_ref)
        yt_ref[...] = jnp.zeros_like(yt_ref)

    segbd = segbd_ref[...]
    dirs = ((rf, kkf, vf, kf, wf, kaf, yf_ref), (rb, kkb, vb, kb, wb, kab, yb_ref))
    chains = [(d, b, p) for d in range(2) for b in range(n_batch) for p in range(n_pairs)]
    n_ch = len(chains)
    lane = lax.broadcasted_iota(jnp.int32, (RWKV_HEAD, LANES), 1) % half

    n_sets = 2
    per_set = n_ch // n_sets
    sets = [list(range(si * per_set, (si + 1) * per_set)) for si in range(n_sets)]

    def side_by_side(parts):
        return jnp.concatenate([jnp.concatenate(parts[i:i + 2], axis=1) for i in range(0, per_set, 2)], axis=0)

    def piece(full, n):
        return full[(n // 2) * RWKV_HEAD:(n // 2 + 1) * RWKV_HEAD, (n % 2) * LANES:(n % 2 + 1) * LANES]

    def group(g, carry):
        bases = (pl.multiple_of(g * SCAN_GROUP, SCAN_GROUP),
                 pl.multiple_of(tb - SCAN_GROUP - g * SCAN_GROUP, SCAN_GROUP))

        def row(which, ci, j):
            d, b, p = chains[ci]
            t = j if d == 0 else SCAN_GROUP - 1 - j
            return dirs[d][which][b, pl.ds(bases[d], SCAN_GROUP), p * LANES:(p + 1) * LANES][t:t + 1]

        def sk_dot(states, cs, j):
            return _dot(side_by_side([(states[n] * row(1, ci, j)).astype(BF16) for n, ci in enumerate(cs)]), segbd)

        sk = []
        for cs in sets:
            for ci in cs:
                d, b, p = chains[ci]
                v8 = dirs[d][2][b, pl.ds(bases[d], SCAN_GROUP), p * LANES:(p + 1) * LANES]
                k8 = dirs[d][3][b, pl.ds(bases[d], SCAN_GROUP), p * LANES:(p + 1) * LANES]
                vt = jnp.concatenate([v8[:, :RWKV_HEAD], v8[:, RWKV_HEAD:]], axis=0).T.astype(BF16)
                kmat = jnp.concatenate([bsel_ref[t] * k8[t:t + 1] for t in range(SCAN_GROUP)],
                                       axis=1).astype(BF16)
                vk_ref[ci] = _dot(vt, kmat)
            sk.append(sk_dot([s_ref[ci] for ci in cs], cs, 0))
        for j in range(SCAN_GROUP):
            for si, cs in enumerate(sets):
                states = []
                for n, ci in enumerate(cs):
                    d = chains[ci][0]
                    t = j if d == 0 else SCAN_GROUP - 1 - j
                    s = (s_ref[ci] * row(4, ci, j) - piece(sk[si], n) * row(5, ci, j)
                         + vk_ref[ci, :, t * LANES:(t + 1) * LANES])
                    s_ref[ci] = s
                    states.append(s)
                y_all = _dot(side_by_side([(s * row(0, ci, j)).astype(BF16) for s, ci in zip(states, cs)]), segbd)
                if j + 1 < SCAN_GROUP:
                    sk[si] = sk_dot(states, cs, j + 1)
                for n, ci in enumerate(cs):
                    d = chains[ci][0]
                    t = j if d == 0 else SCAN_GROUP - 1 - j
                    yt_ref[ci] = jnp.where(lane == (bases[d] + t) % half, piece(y_all, n), yt_ref[ci])
        return carry

    def flush(first_row):
        for ci, (d, b, p) in enumerate(chains):
            t = yt_ref[ci].T
            rows = slice(first_row[d], first_row[d] + half)
            dirs[d][6][b, rows, p * LANES:(p + 1) * LANES] = jnp.concatenate([t[:half], t[half:]], axis=1)

    groups_per_flush = half // SCAN_GROUP
    for part in range(tb // half):
        lax.fori_loop(part * groups_per_flush, (part + 1) * groups_per_flush, group, 0)
        flush((part * half, tb - (part + 1) * half))


def _scan_selectors():
    bsel = np.zeros((SCAN_GROUP, 2 * SCAN_GROUP, LANES), np.float32)
    for h in range(2):
        for j in range(SCAN_GROUP):
            bsel[j, h * SCAN_GROUP + j, h * RWKV_HEAD:(h + 1) * RWKV_HEAD] = 1.0
    return jnp.asarray(bsel, F32), _seg_matrix(2 * LANES, RWKV_HEAD)


def rwkv_scan(r, kk, v, dirs, seq):
    n_batch, lc, gw = r.shape
    tb = SCAN_TB
    nx = seq // tb
    nc = (lc - seq) // tb
    n_chains = 2 * n_batch * (gw // LANES)
    fwd = lambda i: (0, jnp.where(i < nc, nx + i, i - nc), 0)
    bwd = lambda i: (0, nx + nc - 1 - i, 0)
    blk = lambda m: pl.BlockSpec((n_batch, tb, gw), m)
    const = lambda a: pl.BlockSpec(a.shape, lambda i: (0,) * a.ndim)
    bsel, segbd = _scan_selectors()
    return pl.pallas_call(
        functools.partial(_scan_kernel, n_batch), name="rwkv_scan",
        out_shape=[jax.ShapeDtypeStruct((n_batch, lc, gw), F32)] * 2,
        grid=(nx + nc,),
        in_specs=[blk(fwd)] * 6 + [blk(bwd)] * 6 + [const(bsel), const(segbd)],
        out_specs=[blk(fwd), blk(bwd)],
        scratch_shapes=[pltpu.VMEM((n_chains, RWKV_HEAD, LANES), F32),
                        pltpu.VMEM((n_chains, RWKV_HEAD, SCAN_GROUP * LANES), F32),
                        pltpu.VMEM((n_chains, RWKV_HEAD, LANES), F32)],
        compiler_params=_cparams(("arbitrary",)),
    )(r, kk, v, *dirs[0], r, kk, v, *dirs[1], bsel, segbd)


def _rwkv_readout_kernel(yf_ref, yb_ref, r_ref, k0_ref, k1_ref, v_ref, g_ref, rk_ref, lnw_ref, lnb_ref, seg_ref,
                         o_ref):
    seg = seg_ref[...]
    inv = 1.0 / RWKV_HEAD
    y = yf_ref[...] + yb_ref[...]
    mu = _segsum(y, seg) * inv
    dlt = y - mu
    var = _segsum(dlt * dlt, seg) * inv
    yn = dlt * lax.rsqrt(var + RWKV_GN_EPS) * lnw_ref[...] + lnb_ref[...]
    kbar = 0.5 * (k0_ref[...] + k1_ref[...])
    bonus = _segsum(r_ref[...] * kbar * rk_ref[...], seg) * v_ref[...]
    o_ref[...] = ((yn + bonus) * g_ref[...]).astype(o_ref.dtype)


def rwkv_readout(yf, yb, r, k0, k1, v, g, r_k, ln_w, ln_b):
    n_batch, lc, gw = r.shape
    seg = _seg_matrix(gw, RWKV_HEAD)
    row = pl.BlockSpec((None, TM, gw), lambda b, i: (b, i, 0))
    const = lambda a: pl.BlockSpec(a.shape, lambda b, i: (0,) * a.ndim)
    params = [r_k.reshape(1, gw), ln_w.reshape(1, gw), ln_b.reshape(1, gw), seg]
    return pl.pallas_call(
        _rwkv_readout_kernel, name="rwkv_readout",
        out_shape=jax.ShapeDtypeStruct((n_batch, lc, gw), BF16),
        grid=(n_batch, lc // TM),
        in_specs=[row] * 7 + [const(a) for a in params],
        out_specs=row,
        compiler_params=_cparams(("parallel", "arbitrary")),
    )(yf, yb, r, k0, k1, v, g, *params)


def _dft_tables(n):
    k = np.arange(n)
    ang = 2.0 * np.pi * ((k[:, None] * k[None, :]) % n) / n
    return np.cos(ang), np.sin(ang)


def _fft_a_kernel(nb, f_ref, gr_ref, gi_ref, zr_ref, zi_ref):
    n1 = gr_ref.shape[1]
    n2_total = f_ref.shape[0] // n1
    j = pl.program_id(2)
    for i in range(nb):
        rows = f_ref[pl.ds(j * nb + i, n1, stride=n2_total), :].astype(BF16)
        zr_ref[i * n1:(i + 1) * n1, :] = _dot(gr_ref[i], rows)
        zi_ref[i * n1:(i + 1) * n1, :] = _dot(gi_ref[i], rows)


def _fft_b_kernel(kb, scale, zr_ref, zi_ref, fc_ref, fs_ref, cc_ref, cs_ref, w_ref, b_ref, o_ref):
    n2 = fc_ref.shape[0]
    n1 = zr_ref.shape[0] // n2
    j = pl.program_id(2)
    fc, fs = fc_ref[...], fs_ref[...]
    for i in range(kb):
        k1 = j * kb + i
        zr = zr_ref[pl.ds(k1, n2, stride=n1), :].astype(BF16)
        zi = zi_ref[pl.ds(k1, n2, stride=n1), :].astype(BF16)
        xr = _dot(fc, zr) + _dot(fs, zi)
        xi = _dot(fc, zi) - _dot(fs, zr)
        re = (_dot(xr.astype(BF16), cc_ref[...]) + _dot(xi.astype(BF16), cs_ref[...])) * scale
        o_ref[pl.ds(k1, n2, stride=n1), :] = _dot(re.astype(BF16), w_ref[...].astype(BF16)) + b_ref[...]


def _dft_dense_kernel(scale, f_ref, fc_ref, fs_ref, cc_ref, cs_ref, w_ref, b_ref, o_ref):
    f = f_ref[...].astype(BF16)
    xr = _dot(fc_ref[...], f)
    xi = -_dot(fs_ref[...], f)
    re = (_dot(xr.astype(BF16), cc_ref[...]) + _dot(xi.astype(BF16), cs_ref[...])) * scale
    o_ref[...] = _dot(re.astype(BF16), w_ref[...].astype(BF16)) + b_ref[...]


def fourier_mix(pb, seq, w, bias):
    n_batch, lc, gw = pb.shape
    ctx = lc - seq
    ch = FOURIER_CH
    groups = FOURIER_GROUPS
    n2 = LANES
    n1 = seq // n2
    cc, cs = (jnp.asarray(t, BF16) for t in _dft_tables(ch))
    bias3 = bias.reshape(groups, 1, ch)

    k1 = np.arange(n1)[None, :, None]
    m1 = np.arange(n1)[None, None, :]
    m2 = np.arange(n2)[:, None, None]
    ang = 2.0 * np.pi * ((k1 * (n2 * m1 + m2)) % seq) / seq
    g_re = jnp.asarray(np.cos(ang), BF16)
    g_im = jnp.asarray(-np.sin(ang), BF16)
    nb = min(16, n2)
    slab = pl.BlockSpec((None, seq, ch), lambda b, g, j: (b, 0, g))
    gspec = pl.BlockSpec((nb, n1, n1), lambda b, g, j: (j, 0, 0))
    zspec = pl.BlockSpec((None, None, nb * n1, ch), lambda b, g, j: (b, g, j, 0))
    zr, zi = pl.pallas_call(
        functools.partial(_fft_a_kernel, nb), name="fft_stage_a",
        out_shape=[jax.ShapeDtypeStruct((n_batch, groups, seq, ch), F32)] * 2,
        grid=(n_batch, groups, n2 // nb),
        in_specs=[slab, gspec, gspec],
        out_specs=[zspec, zspec],
        compiler_params=_cparams(("parallel", "parallel", "arbitrary")),
    )(pb, g_re, g_im)

    fc, fs = (jnp.asarray(t, BF16) for t in _dft_tables(n2))
    kb = min(16, n1)
    scale = 1.0 / np.sqrt(float(seq) * ch)
    zslab = pl.BlockSpec((None, None, seq, ch), lambda b, g, j: (b, g, 0, 0), pipeline_mode=pl.Buffered(1))
    const = lambda a: pl.BlockSpec(a.shape, lambda b, g, j: (0,) * a.ndim)
    wspec = pl.BlockSpec((None, ch, ch), lambda b, g, j: (g, 0, 0))
    bspec = pl.BlockSpec((None, 1, ch), lambda b, g, j: (g, 0, 0))
    y_x = pl.pallas_call(
        functools.partial(_fft_b_kernel, kb, scale), name="fft_stage_b",
        out_shape=jax.ShapeDtypeStruct((n_batch, seq, gw), F32),
        grid=(n_batch, groups, n1 // kb),
        in_specs=[zslab, zslab, const(fc), const(fs), const(cc), const(cs), wspec, bspec],
        out_specs=pl.BlockSpec((None, seq, ch), lambda b, g, j: (b, 0, g)),
        compiler_params=_cparams(("parallel", "parallel", "arbitrary")),
    )(zr, zi, fc, fs, cc, cs, w, bias3)

    fcc, fsc = (jnp.asarray(t, BF16) for t in _dft_tables(ctx))
    const2 = lambda a: pl.BlockSpec(a.shape, lambda b, g: (0,) * a.ndim)
    ctx_tile = seq // ctx
    y_c = pl.pallas_call(
        functools.partial(_dft_dense_kernel, 1.0 / np.sqrt(float(ctx) * ch)), name="dft_ctx",
        out_shape=jax.ShapeDtypeStruct((n_batch, ctx, gw), F32),
        grid=(n_batch, groups),
        in_specs=[pl.BlockSpec((None, ctx, ch), lambda b, g: (b, ctx_tile, g)),
                  const2(fcc), const2(fsc), const2(cc), const2(cs),
                  pl.BlockSpec((None, ch, ch), lambda b, g: (g, 0, 0)),
                  pl.BlockSpec((None, 1, ch), lambda b, g: (g, 0, 0))],
        out_specs=pl.BlockSpec((None, ctx, ch), lambda b, g: (b, 0, g)),
        compiler_params=_cparams(("parallel", "arbitrary")),
    )(pb, fcc, fsc, cc, cs, w, bias3)
    return jnp.concatenate([y_x, y_c], axis=1)


def _gmlp_kernel(pc_ref, norm_ref, ws_ref, bs_ref, o_ref):
    gw = GROUP_W
    x = pc_ref[...]
    z = 0.5 * x * (1.0 + jnp.tanh(0.7978845608028654 * (x + 0.044715 * (x * x * x))))
    tm = x.shape[0]
    for g in range(GMLP_GROUPS):
        u = z[:, g * GMLP_CH:(g + 1) * GMLP_CH]
        v = z[:, gw + g * GMLP_CH:gw + (g + 1) * GMLP_CH]
        v = v * lax.rsqrt(jnp.mean(v * v, axis=-1, keepdims=True) + EPS) * norm_ref[g:g + 1, :]
        v = v.astype(BF16)
        ws = ws_ref[g].astype(BF16)
        for c in range(tm // GMLP_CHUNK):
            rows = slice(c * GMLP_CHUNK, (c + 1) * GMLP_CHUNK)
            f = _dot(ws, v[rows]) + bs_ref[g]
            o_ref[rows, g * GMLP_CH:(g + 1) * GMLP_CH] = (u[rows] * f).astype(o_ref.dtype)


def gmlp_mix(pc, norm_g, ws, bs):
    n_batch, lc, c = pc.shape
    bs_b = jnp.broadcast_to(bs[:, :, None], bs.shape + (GMLP_CH,))
    const = lambda a: pl.BlockSpec(a.shape, lambda b, i: (0,) * a.ndim)
    return pl.pallas_call(
        _gmlp_kernel, name="gmlp",
        out_shape=jax.ShapeDtypeStruct((n_batch, lc, GROUP_W), BF16),
        grid=(n_batch, lc // TM),
        in_specs=[pl.BlockSpec((None, TM, c), lambda b, i: (b, i, 0)), const(norm_g), const(ws), const(bs_b)],
        out_specs=pl.BlockSpec((None, TM, GROUP_W), lambda b, i: (b, i, 0)),
        compiler_params=_cparams(("parallel", "arbitrary")),
    )(pc, norm_g, ws, bs_b)


def _rope_tables(seq, ctx):
    half = ATT_HEAD // 4
    inv_freq = ROPE_BASE ** (-np.arange(half, dtype=np.float64) / half)
    pos = np.arange(seq)
    ang_r = (pos // GRID_W)[:, None] * inv_freq[None, :]
    ang_c = (pos % GRID_W)[:, None] * inv_freq[None, :]
    cos = np.concatenate([np.cos(ang_r)] * 2 + [np.cos(ang_c)] * 2, axis=1)
    sin = np.concatenate([-np.sin(ang_r), np.sin(ang_r), -np.sin(ang_c), np.sin(ang_c)], axis=1)
    cos = np.concatenate([cos, np.ones((ctx, ATT_HEAD))], axis=0)
    sin = np.concatenate([sin, np.zeros((ctx, ATT_HEAD))], axis=0)
    return jnp.asarray(np.tile(cos, (1, 2)), F32), jnp.asarray(np.tile(sin, (1, 2)), F32)


def _rope(t, cos, sin):
    n = t.shape[1]
    q = ATT_HEAD // 4
    lane = lax.broadcasted_iota(jnp.int32, t.shape, 1)
    swapped = jnp.where((lane % (2 * q)) < q, pltpu.roll(t, n - q, 1), pltpu.roll(t, q, 1))
    return t * cos + swapped * sin


def _attn_prep_kernel(pd_ref, cos_ref, sin_ref, qg_ref, kg_ref, seg_ref, q_o, k_o, v_o):
    gw = GROUP_W
    kvw = ATT_KV_HEADS * ATT_HEAD
    seg = seg_ref[...]
    inv = 1.0 / ATT_HEAD
    cos, sin = cos_ref[...], sin_ref[...]
    q = pd_ref[:, 0:gw]
    q = q * lax.rsqrt(_segsum(q * q, seg) * inv + EPS) * qg_ref[...]
    q = _rope(q, jnp.concatenate([cos] * (gw // kvw), axis=1), jnp.concatenate([sin] * (gw // kvw), axis=1))
    q_o[...] = (q * (ATT_HEAD ** -0.5)).astype(q_o.dtype)
    k = pd_ref[:, gw:gw + kvw]
    k = k * lax.rsqrt(_segsum(k * k, seg[:kvw, :kvw]) * inv + EPS) * kg_ref[...]
    k = _rope(k, cos, sin)
    v = pd_ref[:, gw + kvw:gw + 2 * kvw]
    lane = lax.broadcasted_iota(jnp.int32, k.shape, 1)

    def variants(t):
        h0 = jnp.where(lane < ATT_HEAD, t, 0.0)
        h1 = jnp.where(lane >= ATT_HEAD, t, 0.0)
        return jnp.concatenate([h0, pltpu.roll(h0, ATT_HEAD, 1), pltpu.roll(h1, ATT_HEAD, 1), h1], axis=1)

    k_o[...] = variants(k).astype(k_o.dtype)
    v_o[...] = variants(v).astype(v_o.dtype)


def attention_prepare(pd, seq, q_gain, k_gain):
    n_batch, lc, c = pd.shape
    gw = GROUP_W
    kvw = ATT_KV_HEADS * ATT_HEAD
    cos, sin = _rope_tables(seq, lc - seq)
    seg = _seg_matrix(gw, ATT_HEAD)
    qg = jnp.tile(q_gain, ATT_Q_HEADS).reshape(1, gw)
    kg = jnp.tile(k_gain, ATT_KV_HEADS).reshape(1, kvw)
    const = lambda a: pl.BlockSpec(a.shape, lambda b, i: (0,) * a.ndim)
    tab = pl.BlockSpec((TM, kvw), lambda b, i: (i, 0))
    row = lambda n: pl.BlockSpec((None, TM, n), lambda b, i: (b, i, 0))
    return pl.pallas_call(
        _attn_prep_kernel, name="attn_prep",
        out_shape=[jax.ShapeDtypeStruct((n_batch, lc, gw), BF16)] * 3,
        grid=(n_batch, lc // TM),
        in_specs=[row(c), tab, tab, const(qg), const(kg), const(seg)],
        out_specs=[row(gw)] * 3,
        compiler_params=_cparams(("parallel", "arbitrary")),
    )(pd, cos, sin, qg, kg, seg)


def _attn_kernel(nb, sink_ref, q_ref, kp_ref, kc_ref, kn_ref, kx_ref, vp_ref, vc_ref, vn_ref, vx_ref, o_ref):
    i = pl.program_id(1)
    blk = q_ref.shape[0]
    is_lat = i < nb
    rowi = lax.broadcasted_iota(jnp.int32, (blk, blk), 0)
    coli = lax.broadcasted_iota(jnp.int32, (blk, blk), 1)
    m_prev = (coli >= rowi) & is_lat & (i >= 1)
    m_cur = jnp.broadcast_to(is_lat, (blk, blk))
    m_next = (coli <= rowi) & (i < nb - 1)
    for p in range(ATT_Q_HEADS // 2):
        q = q_ref[:, p * LANES:(p + 1) * LANES]
        acc = None
        for par in range(2):
            h = 2 * p + par
            g = h // (ATT_Q_HEADS // ATT_KV_HEADS)
            col = slice((2 * g + par) * LANES, (2 * g + par + 1) * LANES)
            sink = sink_ref[h]
            s1 = jnp.where(m_prev, _dot_nt(q, kp_ref[:, col]), NEG_INF)
            s2 = jnp.where(m_cur, _dot_nt(q, kc_ref[:, col]), NEG_INF)
            s3 = jnp.where(m_next, _dot_nt(q, kn_ref[:, col]), NEG_INF)
            sx = _dot_nt(q, kx_ref[:, col])
            m = jnp.maximum(jnp.maximum(jnp.max(jnp.maximum(jnp.maximum(s1, s2), s3), axis=-1, keepdims=True),
                                        jnp.max(sx, axis=-1, keepdims=True)), sink)
            p1, p2, p3, px = jnp.exp(s1 - m), jnp.exp(s2 - m), jnp.exp(s3 - m), jnp.exp(sx - m)
            den = (jnp.sum(p1 + p2 + p3, axis=-1, keepdims=True) + jnp.sum(px, axis=-1, keepdims=True)
                   + jnp.exp(sink - m))
            o = (_dot(p1.astype(BF16), vp_ref[:, col]) + _dot(p2.astype(BF16), vc_ref[:, col])
                 + _dot(p3.astype(BF16), vn_ref[:, col]) + _dot(px.astype(BF16), vx_ref[:, col])) / den
            acc = o if acc is None else acc + o
        o_ref[:, p * LANES:(p + 1) * LANES] = acc.astype(o_ref.dtype)


def window_attention(q, kv, vv, sink, seq):
    n_batch, lc, gw = q.shape
    blk = ATT_BLOCK
    nb = seq // blk
    ctx = lc - seq
    wide = kv.shape[-1]
    qspec = pl.BlockSpec((None, blk, gw), lambda b, i, s: (b, i, 0))
    prev = pl.BlockSpec((None, blk, wide), lambda b, i, s: (b, jnp.clip(i - 1, 0, nb - 1), 0))
    cur = pl.BlockSpec((None, blk, wide), lambda b, i, s: (b, jnp.minimum(i, nb - 1), 0))
    nxt = pl.BlockSpec((None, blk, wide), lambda b, i, s: (b, jnp.clip(i + 1, 0, nb - 1), 0))
    cx = pl.BlockSpec((None, ctx, wide), lambda b, i, s: (b, seq // ctx, 0))
    return pl.pallas_call(
        functools.partial(_attn_kernel, nb), name="window_attn",
        out_shape=jax.ShapeDtypeStruct((n_batch, lc, gw), BF16),
        grid_spec=pltpu.PrefetchScalarGridSpec(
            num_scalar_prefetch=1, grid=(n_batch, lc // blk),
            in_specs=[qspec, prev, cur, nxt, cx, prev, cur, nxt, cx],
            out_specs=qspec),
        compiler_params=_cparams(("parallel", "arbitrary")),
    )(sink.astype(F32), q, kv, kv, kv, kv, vv, vv, vv, vv)


def _top2_rows(val, iota):
    m1 = jnp.max(val, axis=0, keepdims=True)
    i1 = jnp.min(jnp.where(val == m1, iota, SUBLANES), axis=0, keepdims=True)
    rest = jnp.where(iota == i1, -jnp.inf, val)
    m2 = jnp.max(rest, axis=0, keepdims=True)
    i2 = jnp.min(jnp.where(rest == m2, iota, SUBLANES), axis=0, keepdims=True)
    return m1, m2, i1, i2


def _outproj_kernel(ya_ref, yb_ref, yc_ref, yd_ref, x_ref, mod_ref, g_ref, wo_ref, wr_ref, rb_ref,
                    xo_ref, h_ref, e_ref, gw_ref):
    d = x_ref.shape[-1]
    gw = GROUP_W
    acc = None
    for n, y_ref in enumerate((ya_ref, yb_ref, yc_ref, yd_ref)):
        part = _dot(y_ref[...].astype(BF16), wo_ref[n * gw:(n + 1) * gw, :])
        acc = part if acc is None else acc + part
    x = x_ref[...] + mod_ref[:, 2 * d:3 * d] * acc
    xo_ref[...] = x
    y = x * lax.rsqrt(jnp.mean(x * x, axis=-1, keepdims=True) + EPS) * g_ref[...]
    h = y * (1.0 + mod_ref[:, 4 * d:5 * d]) + mod_ref[:, 3 * d:4 * d]
    h_ref[...] = h.astype(h_ref.dtype)
    hh, hl = _split(h)
    wh, wl = _split(wr_ref[...])
    logits = _dot_nt(wh, hh) + _dot_nt(wh, hl) + _dot_nt(wl, hh)
    scores = _sigmoid(logits)
    sel = scores + rb_ref[...]
    tm = x.shape[0]
    iota = lax.broadcasted_iota(jnp.int32, (EXPERTS_PER_GROUP, tm), 0)
    best_val, best = None, None
    for grp in range(N_EXPERT_GROUPS):
        m1, m2, _, _ = _top2_rows(sel[grp * EXPERTS_PER_GROUP:(grp + 1) * EXPERTS_PER_GROUP], iota)
        gs = m1 + m2
        if grp == 0:
            best_val, best = gs, jnp.zeros_like(gs, dtype=jnp.int32)
        else:
            upd = gs > best_val
            best_val = jnp.where(upd, gs, best_val)
            best = jnp.where(upd, grp, best)
    sel_in = jnp.zeros((EXPERTS_PER_GROUP, tm), F32)
    sc_in = jnp.zeros((EXPERTS_PER_GROUP, tm), F32)
    for grp in range(N_EXPERT_GROUPS):
        rows = slice(grp * EXPERTS_PER_GROUP, (grp + 1) * EXPERTS_PER_GROUP)
        hit = best == grp
        sel_in = jnp.where(hit, sel[rows], sel_in)
        sc_in = jnp.where(hit, scores[rows], sc_in)
    _, _, i1, i2 = _top2_rows(sel_in, iota)
    w1 = jnp.sum(jnp.where(iota == i1, sc_in, 0.0), axis=0, keepdims=True)
    w2 = jnp.sum(jnp.where(iota == i2, sc_in, 0.0), axis=0, keepdims=True)
    tot = w1 + w2
    e_ref[0:1, :] = best * EXPERTS_PER_GROUP + i1
    e_ref[1:2, :] = best * EXPERTS_PER_GROUP + i2
    gw_ref[0:1, :] = w1 / tot
    gw_ref[1:2, :] = w2 / tot


def out_projection_router(ys, stream, mod, gain2, w_out, w_router, router_bias, seq):
    n_batch, lc, d = stream.shape
    gw = GROUP_W
    ctx_tile = seq // TM
    wr_t = w_router.T
    rb = jnp.broadcast_to(router_bias.astype(F32)[:, None], (N_EXPERTS, TM))
    row = lambda n: pl.BlockSpec((None, TM, n), lambda b, i: (b, i, 0))
    const = lambda a: pl.BlockSpec(a.shape, lambda b, i: (0,) * a.ndim)
    lane_row = pl.BlockSpec((None, TOP_K, TM), lambda b, i: (b, 0, i))
    return pl.pallas_call(
        _outproj_kernel, name="out_proj_router",
        out_shape=[jax.ShapeDtypeStruct((n_batch, lc, d), F32), jax.ShapeDtypeStruct((n_batch, lc, d), BF16),
                   jax.ShapeDtypeStruct((n_batch, TOP_K, lc), jnp.int32),
                   jax.ShapeDtypeStruct((n_batch, TOP_K, lc), F32)],
        grid=(n_batch, lc // TM),
        in_specs=[row(gw)] * 4 + [row(d), pl.BlockSpec((None, 1, mod.shape[-1]), _mod_index(n_batch, ctx_tile)),
                                  pl.BlockSpec((1, d), lambda b, i: (0, 0)),
                                  pl.BlockSpec(w_out.shape, lambda b, i: (0, 0), pipeline_mode=pl.Buffered(1)),
                                  const(wr_t), const(rb)],
        out_specs=[row(d), row(d), lane_row, lane_row],
        compiler_params=_cparams(("parallel", "arbitrary")),
    )(*ys, stream, mod, gain2.reshape(1, d), w_out, wr_t, rb)


def _expert_kernel(be_ref, nu_ref, x_ref, wg_ref, wu_ref, wd_ref, o_ref, wg_s, wu_s, wd_s):
    i = pl.program_id(0)
    prev = be_ref[jnp.maximum(i - 1, 0)]
    used = i < nu_ref[0]

    @pl.when(used & ((i == 0) | (be_ref[i] != prev)))
    def _():
        wg_s[...] = wg_ref[...].astype(BF16)
        wu_s[...] = wu_ref[...].astype(BF16)
        wd_s[...] = wd_ref[...].astype(BF16)

    @pl.when(used)
    def _():
        x = x_ref[...]
        a = _dot(x, wg_s[...])
        hb = (a * _sigmoid(a)) * _dot(x, wu_s[...])
        o_ref[...] = _dot(hb.astype(BF16), wd_s[...])

    @pl.when(jnp.logical_not(used))
    def _():
        o_ref[...] = jnp.zeros_like(o_ref)


def expert_ffn(xs, block_e, n_used, wg, wu, wd):
    rows, d = xs.shape
    n_blocks = rows // MOE_BLOCK
    de = wg.shape[-1]
    return pl.pallas_call(
        _expert_kernel, name="moe_experts",
        out_shape=jax.ShapeDtypeStruct((rows, d), F32),
        grid_spec=pltpu.PrefetchScalarGridSpec(
            num_scalar_prefetch=2, grid=(n_blocks,),
            in_specs=[pl.BlockSpec((MOE_BLOCK, d), lambda i, be, nu: (i, 0)),
                      pl.BlockSpec((None, d, de), lambda i, be, nu: (be[i], 0, 0)),
                      pl.BlockSpec((None, d, de), lambda i, be, nu: (be[i], 0, 0)),
                      pl.BlockSpec((None, de, d), lambda i, be, nu: (be[i], 0, 0))],
            out_specs=pl.BlockSpec((MOE_BLOCK, d), lambda i, be, nu: (i, 0)),
            scratch_shapes=[pltpu.VMEM((d, de), BF16), pltpu.VMEM((d, de), BF16), pltpu.VMEM((de, d), BF16)]),
        compiler_params=_cparams(("arbitrary",)),
    )(block_e, n_used, xs, wg, wu, wd)


def _residual_kernel(x_ref, y_ref, mod_ref, o_ref):
    d = x_ref.shape[-1]
    o_ref[...] = x_ref[...] + mod_ref[:, 5 * d:6 * d] * y_ref[...]


def gated_residual(stream, y, mod, seq, n_rows):
    n_batch, lc, d = stream.shape
    row = pl.BlockSpec((None, TM, d), lambda b, i: (b, i, 0))
    return pl.pallas_call(
        _residual_kernel, name="moe_residual",
        out_shape=jax.ShapeDtypeStruct((n_batch, n_rows, d), F32),
        grid=(n_batch, n_rows // TM),
        in_specs=[row, row, pl.BlockSpec((None, 1, mod.shape[-1]), _mod_index(n_batch, seq // TM))],
        out_specs=row,
        compiler_params=_cparams(("parallel", "arbitrary")),
    )(stream, y, mod)


def moe_ffn(h2, eidx, gates, wg, wu, wd):
    n_batch, lc, d = h2.shape
    n_tok = n_batch * lc
    n_assign = n_tok * TOP_K
    e = jnp.swapaxes(eidx, 1, 2).reshape(n_assign)
    gate = jnp.swapaxes(gates, 1, 2).reshape(n_assign)
    onehot = (e[:, None] == jnp.arange(N_EXPERTS, dtype=jnp.int32)[None, :]).astype(jnp.int32)
    cum = jnp.cumsum(onehot, axis=0)
    rank = jnp.take_along_axis(cum, e[:, None], axis=1)[:, 0] - 1
    counts = cum[-1]
    padded = (counts + MOE_BLOCK - 1) // MOE_BLOCK * MOE_BLOCK
    pad_end = jnp.cumsum(padded)
    dest = (pad_end - padded)[e] + rank
    n_blocks = n_assign // MOE_BLOCK + N_EXPERTS
    tok = jnp.arange(n_assign, dtype=jnp.int32) // TOP_K
    row_tok = jnp.zeros((n_blocks * MOE_BLOCK,), jnp.int32).at[dest].set(tok)
    block_e = jnp.minimum(jnp.searchsorted(pad_end, jnp.arange(n_blocks, dtype=jnp.int32) * MOE_BLOCK, side='right'),
                          N_EXPERTS - 1).astype(jnp.int32)
    n_used = (pad_end[-1:] // MOE_BLOCK).astype(jnp.int32)
    xs = h2.reshape(n_tok, d)[row_tok]
    ys = expert_ffn(xs, block_e, n_used, wg, wu, wd)
    contrib = ys[dest] * gate[:, None]
    return contrib.reshape(n_tok, TOP_K, d).sum(axis=1).reshape(n_batch, lc, d)


def kernel(x, c, ctx, c_ctx, norm1, norm2, w_ada, b_ada, w_in, w_out, rwkv_conv, rwkv_w0, rwkv_w_up, rwkv_a0,
           rwkv_a_up, rwkv_g_up, rwkv_k_k, rwkv_k_a, rwkv_r_k, rwkv_ln_w, rwkv_ln_b, rwkv_v0, rwkv_v_down,
           rwkv_v_up, fourier_w, fourier_b, gmlp_norm, gmlp_ws, gmlp_bs, attn_q_norm, attn_k_norm, attn_sink,
           w_router, router_bias, w_e_gate, w_e_up, w_e_down):
    n_batch, seq, d = x.shape
    depth = w_in.shape[0]
    stream = jnp.concatenate([x, ctx], axis=1)
    c_all = jnp.zeros((SUBLANES, d), F32).at[:n_batch].set(c).at[n_batch].set(c_ctx)
    col_splits = np.cumsum([A_COLS, B_COLS, C_COLS])
    v_first = None
    for l in range(depth):
        mod = ada_modulation(c_all, w_ada[l], b_ada[l]).reshape(SUBLANES, 1, 6 * d)
        w_parts = [w.astype(BF16) for w in jnp.split(w_in[l], col_splits, axis=1)]
        pa, pb, pc, pd = in_projection(stream, mod, norm1[l], w_parts, seq)

        vres = None if l == 0 else (rwkv_v0[l - 1], rwkv_v_down[l - 1], rwkv_v_up[l - 1])
        r, kk, v, k0, wd0, kka0, k1, wd1, kka1, g = rwkv_prepare(
            pa, seq, rwkv_conv[l], rwkv_w0[l], rwkv_w_up[l], rwkv_a0[l], rwkv_a_up[l], rwkv_g_up[l],
            rwkv_k_k[l], rwkv_k_a[l], v_first, vres)
        if l == 0:
            v_first = v
        yf, yb = rwkv_scan(r, kk, v, ((k0, wd0, kka0), (k1, wd1, kka1)), seq)
        y_a = rwkv_readout(yf, yb, r, k0, k1, v, g, rwkv_r_k[l].reshape(-1), rwkv_ln_w[l], rwkv_ln_b[l])

        y_b = fourier_mix(pb, seq, fourier_w[l], fourier_b[l])
        y_c = gmlp_mix(pc, gmlp_norm[l], gmlp_ws[l], gmlp_bs[l])
        q, kv, vv = attention_prepare(pd, seq, attn_q_norm[l], attn_k_norm[l])
        y_d = window_attention(q, kv, vv, attn_sink[l], seq)

        stream, h2, eidx, gates = out_projection_router(
            (y_a, y_b, y_c, y_d), stream, mod, norm2[l], w_out[l].astype(BF16), w_router, router_bias, seq)
        y_moe = moe_ffn(h2, eidx, gates, w_e_gate[l], w_e_up[l], w_e_down[l])
        stream = gated_residual(stream, y_moe, mod, seq, seq if l == depth - 1 else stream.shape[1])
    return stream
```

```python
import functools

import numpy as np
import jax
import jax.numpy as jnp
from jax import lax
from jax.experimental import pallas as pl
from jax.experimental.pallas import tpu as pltpu

F32 = jnp.float32
BF16 = jnp.bfloat16

D_MODEL = 2048
DEPTH = 2
GRID_W = 64
EPS = 1e-6
NEG_INF = -1e30
GROUP_W = 512
RWKV_HEAD = 64
RWKV_HEADS = 8
DECAY_LORA = 64
ICLR_LORA = 64
GATE_LORA = 128
VRES_LORA = 32
RWKV_GN_EPS = 64e-5
FOURIER_GROUPS = 4
FOURIER_CH = 128
GMLP_GROUPS = 4
GMLP_CH = 128
GMLP_CHUNK = 128
ATT_HEAD = 64
ATT_Q_HEADS = 8
ATT_KV_HEADS = 2
WINDOW = 128
ATT_BLOCK = 128
ROPE_BASE = 10000.0
N_EXPERTS = 64
N_EXPERT_GROUPS = 8
EXPERTS_PER_GROUP = 8
TOP_K = 2
D_EXPERT = 512
MOE_BLOCK = 128
A_COLS = 1920
B_COLS = 512
C_COLS = 1024
D_COLS = 768

LANES = 128
SUBLANES = 8
TM = 256
SCAN_TB = 128
SCAN_GROUP = SUBLANES
VMEM_LIMIT = 56 * 1024 * 1024


def _cparams(sem, vmem=VMEM_LIMIT):
    return pltpu.CompilerParams(dimension_semantics=sem, vmem_limit_bytes=vmem)


def _dot(a, b):
    return jnp.dot(a, b, preferred_element_type=F32)


def _dot_nt(a, b):
    return lax.dot_general(a, b, (((1,), (1,)), ((), ())), preferred_element_type=F32)


def _dot_tn(a, b):
    return lax.dot_general(a, b, (((0,), (0,)), ((), ())), preferred_element_type=F32)


def _split(x):
    hi = x.astype(BF16)
    lo = (x - hi.astype(F32)).astype(BF16)
    return hi, lo


def _dot3(a, b):
    ah, al = _split(a)
    bh, bl = _split(b)
    return _dot(ah, bh) + _dot(ah, bl) + _dot(al, bh)


def _segsum(x, seg):
    xh, xl = _split(x)
    return _dot(xh, seg) + _dot(xl, seg)


def _sigmoid(x):
    return 1.0 / (1.0 + jnp.exp(-x))


def _seg_matrix(width, seg):
    i = np.arange(width) // seg
    return jnp.asarray((i[:, None] == i[None, :]).astype(np.float32), dtype=BF16)


def _ada_kernel(c_ref, w_ref, b_ref, o_ref):
    c = c_ref[...]
    a = (c * _sigmoid(c)).astype(BF16)
    o_ref[...] = _dot(a, w_ref[...].astype(BF16)) + b_ref[...]


def ada_modulation(c_all, w_ada, b_ada, layer, tn=1024):
    rows, d = c_all.shape
    n_layers, _, n = w_ada.shape
    return pl.pallas_call(
        _ada_kernel, name="ada_mod",
        out_shape=jax.ShapeDtypeStruct((rows, n), F32),
        grid=(n // tn,),
        in_specs=[pl.BlockSpec((rows, d), lambda j: (0, 0)),
                  pl.BlockSpec((None, d, tn), lambda j: (layer, 0, j)),
                  pl.BlockSpec((None, 1, tn), lambda j: (layer, 0, j))],
        out_specs=pl.BlockSpec((rows, tn), lambda j: (0, j)),
        compiler_params=_cparams(("arbitrary",)),
    )(c_all, w_ada, b_ada.reshape(n_layers, 1, n))


def _inproj_kernel(x_ref, mod_ref, g_ref, wa_ref, wb_ref, wc_ref, wd_ref, pa_ref, pb_ref, pc_ref, pd_ref):
    d = x_ref.shape[-1]
    x = x_ref[...]
    y = x * lax.rsqrt(jnp.mean(x * x, axis=-1, keepdims=True) + EPS) * g_ref[...]
    sh = mod_ref[:, 0:d]
    sc = mod_ref[:, d:2 * d]
    h = (y * (1.0 + sc) + sh).astype(BF16)
    pa_ref[...] = _dot(h, wa_ref[...])
    pb_ref[...] = _dot(h, wb_ref[...])
    pc_ref[...] = _dot(h, wc_ref[...])
    pd_ref[...] = _dot(h, wd_ref[...])


def _mod_index(n_batch, ctx_tile):
    return lambda b, i: (jnp.where(i >= ctx_tile, n_batch, b), 0, 0)


def in_projection(stream, mod, gain, w_parts, seq):
    n_batch, lc, d = stream.shape
    ctx_tile = seq // TM
    resident = lambda w: pl.BlockSpec(w.shape, lambda b, i: (0, 0), pipeline_mode=pl.Buffered(1))
    row = lambda n: pl.BlockSpec((None, TM, n), lambda b, i: (b, i, 0))
    return pl.pallas_call(
        _inproj_kernel, name="in_proj",
        out_shape=[jax.ShapeDtypeStruct((n_batch, lc, w.shape[1]), F32) for w in w_parts],
        grid=(n_batch, lc // TM),
        in_specs=[row(d),
                  pl.BlockSpec((None, 1, mod.shape[-1]), _mod_index(n_batch, ctx_tile)),
                  pl.BlockSpec((1, d), lambda b, i: (0, 0))] + [resident(w) for w in w_parts],
        out_specs=[row(w.shape[1]) for w in w_parts],
        compiler_params=_cparams(("parallel", "arbitrary")),
    )(stream, mod, gain.reshape(1, d), *w_parts)


def _rwkv_prep_kernel(has_vres, ctx_tile, *refs):
    (pa_ref, hp_ref, hn_ref, conv_ref, w0_ref, wup_ref, a0_ref, aup_ref, gup_ref, kk_par_ref, ka_par_ref,
     seg_ref) = refs[:12]
    rest = refs[12:]
    if has_vres:
        vf_ref, v0_ref, vdn_ref, vup_ref = rest[:4]
        rest = rest[4:]
    r_o, kk_o, v_o, k0_o, w0_o, kka0_o, k1_o, w1_o, kka1_o, g_o = rest
    gw = GROUP_W
    x = pa_ref[...]
    tm = x.shape[0]
    i = pl.program_id(1)
    first = (i == 0) | (i == ctx_tile)
    last = (i == ctx_tile - 1) | (i == pl.num_programs(1) - 1)
    halo_prev = jnp.where(first, 0.0, hp_ref[SUBLANES - 1:SUBLANES, :])
    halo_next = jnp.where(last, 0.0, hn_ref[0:1, :])
    row = lax.broadcasted_iota(jnp.int32, x.shape, 0)
    x_prev = jnp.where(row == 0, halo_prev, pltpu.roll(x, 1, 0))
    x_next = jnp.where(row == tm - 1, halo_next, pltpu.roll(x, tm - 1, 0))
    y = x_prev * conv_ref[0:1, :] + x * conv_ref[1:2, :] + x_next * conv_ref[2:3, :]
    r = y[:, 0:gw]
    k = y[:, gw:2 * gw]
    v = y[:, 2 * gw:3 * gw]
    wl = y[:, 3 * gw:3 * gw + 2 * DECAY_LORA]
    al = y[:, 3 * gw + 2 * DECAY_LORA:3 * gw + 2 * DECAY_LORA + 2 * ICLR_LORA]
    gl = y[:, 3 * gw + 2 * DECAY_LORA + 2 * ICLR_LORA:]
    if has_vres:
        mix = _sigmoid(v0_ref[...] + _dot3(_dot3(v, vdn_ref[...]), vup_ref[...]))
        v = v + (vf_ref[...] - v) * mix
    kk = k * kk_par_ref[...]
    nrm = jnp.maximum(jnp.sqrt(_segsum(kk * kk, seg_ref[...])), 1e-12)
    kk = kk / nrm
    r_o[...] = r
    kk_o[...] = kk
    v_o[...] = v
    tanh_wl = jnp.tanh(wl)
    for d, (k_o, w_o, kka_o) in enumerate(((k0_o, w0_o, kka0_o), (k1_o, w1_o, kka1_o))):
        w_raw = w0_ref[d:d + 1, :] + _dot3(tanh_wl, wup_ref[d])
        z = -w_raw
        softplus = jnp.maximum(z, 0.0) + jnp.log(1.0 + jnp.exp(-jnp.abs(z)))
        w_o[...] = jnp.exp(-jnp.exp(-softplus - 0.5))
        a = _sigmoid(a0_ref[d:d + 1, :] + _dot3(al, aup_ref[d]))
        k_o[...] = k * (1.0 + (a - 1.0) * ka_par_ref[...])
        kka_o[...] = kk * a
    g_o[...] = _dot3(_sigmoid(gl), gup_ref[...])


def rwkv_prepare(pa, seq, conv_w, w0, w_up, a0, a_up, g_up, k_k, k_a, v_first=None, vres=None):
    n_batch, lc, c = pa.shape
    gw = GROUP_W
    sub_per_tile = TM // SUBLANES
    n_sub = lc // SUBLANES
    pad_rows = lambda w, d, n: jnp.zeros((2 * n, gw), F32).at[d * n:(d + 1) * n].set(w)
    wup = jnp.stack([pad_rows(w_up[d], d, DECAY_LORA) for d in range(2)])
    aup = jnp.stack([pad_rows(a_up[d], d, ICLR_LORA) for d in range(2)])
    seg = _seg_matrix(gw, RWKV_HEAD)
    const = lambda a: pl.BlockSpec(a.shape, lambda b, i: (0,) * a.ndim)
    row = lambda n: pl.BlockSpec((None, TM, n), lambda b, i: (b, i, 0))
    halo_prev = pl.BlockSpec((None, SUBLANES, c), lambda b, i: (b, jnp.maximum(i * sub_per_tile - 1, 0), 0))
    halo_next = pl.BlockSpec((None, SUBLANES, c), lambda b, i: (b, jnp.minimum((i + 1) * sub_per_tile, n_sub - 1), 0))
    args = [pa, pa, pa, conv_w, w0, wup, a0, aup, g_up, k_k.reshape(1, gw), k_a.reshape(1, gw), seg]
    specs = [row(c), halo_prev, halo_next] + [const(a) for a in args[3:]]
    has_vres = vres is not None
    if has_vres:
        v0, v_down, v_up = vres
        vdn = jnp.zeros((gw, LANES), F32).at[:, :VRES_LORA].set(v_down)
        vup = jnp.zeros((LANES, gw), F32).at[:VRES_LORA].set(v_up)
        extra = [v_first, v0.reshape(1, gw), vdn, vup]
        args += extra
        specs += [row(gw)] + [const(a) for a in extra[1:]]
    return pl.pallas_call(
        functools.partial(_rwkv_prep_kernel, has_vres, seq // TM), name="rwkv_prep",
        out_shape=[jax.ShapeDtypeStruct((n_batch, lc, gw), F32)] * 10,
        grid=(n_batch, lc // TM),
        in_specs=specs,
        out_specs=[row(gw)] * 10,
        compiler_params=_cparams(("parallel", "arbitrary")),
    )(*args)


def _scan_kernel(n_batch, rf, kkf, vf, kf, wf, kaf, rb, kkb, vb, kb, wb, kab, bsel_ref, segbd_ref,
                 yf_ref, yb_ref, s_ref, vk_ref, yt_ref):
    tb = rf.shape[1]
    n_pairs = GROUP_W // LANES
    half = RWKV_HEAD

    @pl.when(pl.program_id(0) == 0)
    def _():
        s_ref[...] = jnp.zeros_like(s_ref)
        yt_ref[...] = jnp.zeros_like(yt_ref)

    segbd = segbd_ref[...]
    dirs = ((rf, kkf, vf, kf, wf, kaf, yf_ref), (rb, kkb, vb, kb, wb, kab, yb_ref))
    chains = [(d, b, p) for d in range(2) for b in range(n_batch) for p in range(n_pairs)]
    n_ch = len(chains)
    lane = lax.broadcasted_iota(jnp.int32, (RWKV_HEAD, LANES), 1) % half

    n_sets = 2
    per_set = n_ch // n_sets
    sets = [list(range(si * per_set, (si + 1) * per_set)) for si in range(n_sets)]

    def side_by_side(parts):
        return jnp.concatenate([jnp.concatenate(parts[i:i + 2], axis=1) for i in range(0, per_set, 2)], axis=0)

    def piece(full, n):
        return full[(n // 2) * RWKV_HEAD:(n // 2 + 1) * RWKV_HEAD, (n % 2) * LANES:(n % 2 + 1) * LANES]

    def group(g, carry):
        bases = (pl.multiple_of(g * SCAN_GROUP, SCAN_GROUP),
                 pl.multiple_of(tb - SCAN_GROUP - g * SCAN_GROUP, SCAN_GROUP))

        def row(which, ci, j):
            d, b, p = chains[ci]
            t = j if d == 0 else SCAN_GROUP - 1 - j
            return dirs[d][which][b, pl.ds(bases[d], SCAN_GROUP), p * LANES:(p + 1) * LANES][t:t + 1]

        def sk_dot(states, cs, j):
            return _dot(side_by_side([(states[n] * row(1, ci, j)).astype(BF16) for n, ci in enumerate(cs)]), segbd)

        sk = []
        for cs in sets:
            for ci in cs:
                d, b, p = chains[ci]
                v8 = dirs[d][2][b, pl.ds(bases[d], SCAN_GROUP), p * LANES:(p + 1) * LANES]
                k8 = dirs[d][3][b, pl.ds(bases[d], SCAN_GROUP), p * LANES:(p + 1) * LANES]
                vt = jnp.concatenate([v8[:, :RWKV_HEAD], v8[:, RWKV_HEAD:]], axis=0).T.astype(BF16)
                kmat = jnp.concatenate([bsel_ref[t] * k8[t:t + 1] for t in range(SCAN_GROUP)],
                                       axis=1).astype(BF16)
                vk_ref[ci] = _dot(vt, kmat)
            sk.append(sk_dot([s_ref[ci] for ci in cs], cs, 0))
        for j in range(SCAN_GROUP):
            for si, cs in enumerate(sets):
                states = []
                for n, ci in enumerate(cs):
                    d = chains[ci][0]
                    t = j if d == 0 else SCAN_GROUP - 1 - j
                    s = (s_ref[ci] * row(4, ci, j) - piece(sk[si], n) * row(5, ci, j)
                         + vk_ref[ci, :, t * LANES:(t + 1) * LANES])
                    s_ref[ci] = s
                    states.append(s)
                y_all = _dot(side_by_side([(s * row(0, ci, j)).astype(BF16) for s, ci in zip(states, cs)]), segbd)
                if j + 1 < SCAN_GROUP:
                    sk[si] = sk_dot(states, cs, j + 1)
                for n, ci in enumerate(cs):
                    d = chains[ci][0]
                    t = j if d == 0 else SCAN_GROUP - 1 - j
                    yt_ref[ci] = jnp.where(lane == (bases[d] + t) % half, piece(y_all, n), yt_ref[ci])
        return carry

    def flush(first_row):
        for ci, (d, b, p) in enumerate(chains):
            t = yt_ref[ci].T
            rows = slice(first_row[d], first_row[d] + half)
            dirs[d][6][b, rows, p * LANES:(p + 1) * LANES] = jnp.concatenate([t[:half], t[half:]], axis=1)

    groups_per_flush = half // SCAN_GROUP
    for part in range(tb // half):
        lax.fori_loop(part * groups_per_flush, (part + 1) * groups_per_flush, group, 0)
        flush((part * half, tb - (part + 1) * half))


def _scan_selectors():
    bsel = np.zeros((SCAN_GROUP, 2 * SCAN_GROUP, LANES), np.float32)
    for h in range(2):
        for j in range(SCAN_GROUP):
            bsel[j, h * SCAN_GROUP + j, h * RWKV_HEAD:(h + 1) * RWKV_HEAD] = 1.0
    return jnp.asarray(bsel, F32), _seg_matrix(2 * LANES, RWKV_HEAD)


def rwkv_scan(r, kk, v, dirs, seq):
    n_batch, lc, gw = r.shape
    tb = SCAN_TB
    nx = seq // tb
    nc = (lc - seq) // tb
    n_chains = 2 * n_batch * (gw // LANES)
    fwd = lambda i: (0, jnp.where(i < nc, nx + i, i - nc), 0)
    bwd = lambda i: (0, nx + nc - 1 - i, 0)
    blk = lambda m: pl.BlockSpec((n_batch, tb, gw), m)
    const = lambda a: pl.BlockSpec(a.shape, lambda i: (0,) * a.ndim)
    bsel, segbd = _scan_selectors()
    return pl.pallas_call(
        functools.partial(_scan_kernel, n_batch), name="rwkv_scan",
        out_shape=[jax.ShapeDtypeStruct((n_batch, lc, gw), F32)] * 2,
        grid=(nx + nc,),
        in_specs=[blk(fwd)] * 6 + [blk(bwd)] * 6 + [const(bsel), const(segbd)],
        out_specs=[blk(fwd), blk(bwd)],
        scratch_shapes=[pltpu.VMEM((n_chains, RWKV_HEAD, LANES), F32),
                        pltpu.VMEM((n_chains, RWKV_HEAD, SCAN_GROUP * LANES), F32),
                        pltpu.VMEM((n_chains, RWKV_HEAD, LANES), F32)],
        compiler_params=_cparams(("arbitrary",)),
    )(r, kk, v, *dirs[0], r, kk, v, *dirs[1], bsel, segbd)


def _rwkv_readout_kernel(yf_ref, yb_ref, r_ref, k0_ref, k1_ref, v_ref, g_ref, rk_ref, lnw_ref, lnb_ref, seg_ref,
                         o_ref):
    seg = seg_ref[...]
    inv = 1.0 / RWKV_HEAD
    y = yf_ref[...] + yb_ref[...]
    mu = _segsum(y, seg) * inv
    dlt = y - mu
    var = _segsum(dlt * dlt, seg) * inv
    yn = dlt * lax.rsqrt(var + RWKV_GN_EPS) * lnw_ref[...] + lnb_ref[...]
    kbar = 0.5 * (k0_ref[...] + k1_ref[...])
    bonus = _segsum(r_ref[...] * kbar * rk_ref[...], seg) * v_ref[...]
    o_ref[...] = ((yn + bonus) * g_ref[...]).astype(o_ref.dtype)


def rwkv_readout(yf, yb, r, k0, k1, v, g, r_k, ln_w, ln_b):
    n_batch, lc, gw = r.shape
    seg = _seg_matrix(gw, RWKV_HEAD)
    row = pl.BlockSpec((None, TM, gw), lambda b, i: (b, i, 0))
    const = lambda a: pl.BlockSpec(a.shape, lambda b, i: (0,) * a.ndim)
    params = [r_k.reshape(1, gw), ln_w.reshape(1, gw), ln_b.reshape(1, gw), seg]
    return pl.pallas_call(
        _rwkv_readout_kernel, name="rwkv_readout",
        out_shape=jax.ShapeDtypeStruct((n_batch, lc, gw), BF16),
        grid=(n_batch, lc // TM),
        in_specs=[row] * 7 + [const(a) for a in params],
        out_specs=row,
        compiler_params=_cparams(("parallel", "arbitrary")),
    )(yf, yb, r, k0, k1, v, g, *params)


def _dft_tables(n):
    k = np.arange(n)
    ang = 2.0 * np.pi * ((k[:, None] * k[None, :]) % n) / n
    return np.cos(ang), np.sin(ang)


def _fft_a_kernel(nb, f_ref, gr_ref, gi_ref, zr_ref, zi_ref):
    n1 = gr_ref.shape[1]
    n2_total = f_ref.shape[0] // n1
    j = pl.program_id(2)
    for i in range(nb):
        rows = f_ref[pl.ds(j * nb + i, n1, stride=n2_total), :].astype(BF16)
        zr_ref[i * n1:(i + 1) * n1, :] = _dot(gr_ref[i], rows)
        zi_ref[i * n1:(i + 1) * n1, :] = _dot(gi_ref[i], rows)


def _fft_b_kernel(kb, scale, zr_ref, zi_ref, fc_ref, fs_ref, cc_ref, cs_ref, w_ref, b_ref, o_ref):
    n2 = fc_ref.shape[0]
    n1 = zr_ref.shape[0] // n2
    j = pl.program_id(2)
    fc, fs = fc_ref[...], fs_ref[...]
    for i in range(kb):
        k1 = j * kb + i
        zr = zr_ref[pl.ds(k1, n2, stride=n1), :].astype(BF16)
        zi = zi_ref[pl.ds(k1, n2, stride=n1), :].astype(BF16)
        xr = _dot(fc, zr) + _dot(fs, zi)
        xi = _dot(fc, zi) - _dot(fs, zr)
        re = (_dot(xr.astype(BF16), cc_ref[...]) + _dot(xi.astype(BF16), cs_ref[...])) * scale
        o_ref[pl.ds(k1, n2, stride=n1), :] = _dot(re.astype(BF16), w_ref[...].astype(BF16)) + b_ref[...]


def _dft_dense_kernel(scale, f_ref, fc_ref, fs_ref, cc_ref, cs_ref, w_ref, b_ref, o_ref):
    f = f_ref[...].astype(BF16)
    xr = _dot(fc_ref[...], f)
    xi = -_dot(fs_ref[...], f)
    re = (_dot(xr.astype(BF16), cc_ref[...]) + _dot(xi.astype(BF16), cs_ref[...])) * scale
    o_ref[...] = _dot(re.astype(BF16), w_ref[...].astype(BF16)) + b_ref[...]


def fourier_mix(pb, seq, w, bias):
    n_batch, lc, gw = pb.shape
    ctx = lc - seq
    ch = FOURIER_CH
    groups = FOURIER_GROUPS
    n2 = LANES
    n1 = seq // n2
    cc, cs = (jnp.asarray(t, BF16) for t in _dft_tables(ch))
    bias3 = bias.reshape(groups, 1, ch)

    k1 = np.arange(n1)[None, :, None]
    m1 = np.arange(n1)[None, None, :]
    m2 = np.arange(n2)[:, None, None]
    ang = 2.0 * np.pi * ((k1 * (n2 * m1 + m2)) % seq) / seq
    g_re = jnp.asarray(np.cos(ang), BF16)
    g_im = jnp.asarray(-np.sin(ang), BF16)
    nb = min(16, n2)
    slab = pl.BlockSpec((None, seq, ch), lambda b, g, j: (b, 0, g))
    gspec = pl.BlockSpec((nb, n1, n1), lambda b, g, j: (j, 0, 0))
    zspec = pl.BlockSpec((None, None, nb * n1, ch), lambda b, g, j: (b, g, j, 0))
    zr, zi = pl.pallas_call(
        functools.partial(_fft_a_kernel, nb), name="fft_stage_a",
        out_shape=[jax.ShapeDtypeStruct((n_batch, groups, seq, ch), F32)] * 2,
        grid=(n_batch, groups, n2 // nb),
        in_specs=[slab, gspec, gspec],
        out_specs=[zspec, zspec],
        compiler_params=_cparams(("parallel", "parallel", "arbitrary")),
    )(pb, g_re, g_im)

    fc, fs = (jnp.asarray(t, BF16) for t in _dft_tables(n2))
    kb = min(16, n1)
    scale = 1.0 / np.sqrt(float(seq) * ch)
    zslab = pl.BlockSpec((None, None, seq, ch), lambda b, g, j: (b, g, 0, 0), pipeline_mode=pl.Buffered(1))
    const = lambda a: pl.BlockSpec(a.shape, lambda b, g, j: (0,) * a.ndim)
    wspec = pl.BlockSpec((None, ch, ch), lambda b, g, j: (g, 0, 0))
    bspec = pl.BlockSpec((None, 1, ch), lambda b, g, j: (g, 0, 0))
    y_x = pl.pallas_call(
        functools.partial(_fft_b_kernel, kb, scale), name="fft_stage_b",
        out_shape=jax.ShapeDtypeStruct((n_batch, seq, gw), F32),
        grid=(n_batch, groups, n1 // kb),
        in_specs=[zslab, zslab, const(fc), const(fs), const(cc), const(cs), wspec, bspec],
        out_specs=pl.BlockSpec((None, seq, ch), lambda b, g, j: (b, 0, g)),
        compiler_params=_cparams(("parallel", "parallel", "arbitrary")),
    )(zr, zi, fc, fs, cc, cs, w, bias3)

    fcc, fsc = (jnp.asarray(t, BF16) for t in _dft_tables(ctx))
    const2 = lambda a: pl.BlockSpec(a.shape, lambda b, g: (0,) * a.ndim)
    ctx_tile = seq // ctx
    y_c = pl.pallas_call(
        functools.partial(_dft_dense_kernel, 1.0 / np.sqrt(float(ctx) * ch)), name="dft_ctx",
        out_shape=jax.ShapeDtypeStruct((n_batch, ctx, gw), F32),
        grid=(n_batch, groups),
        in_specs=[pl.BlockSpec((None, ctx, ch), lambda b, g: (b, ctx_tile, g)),
                  const2(fcc), const2(fsc), const2(cc), const2(cs),
                  pl.BlockSpec((None, ch, ch), lambda b, g: (g, 0, 0)),
                  pl.BlockSpec((None, 1, ch), lambda b, g: (g, 0, 0))],
        out_specs=pl.BlockSpec((None, ctx, ch), lambda b, g: (b, 0, g)),
        compiler_params=_cparams(("parallel", "arbitrary")),
    )(pb, fcc, fsc, cc, cs, w, bias3)
    return jnp.concatenate([y_x, y_c], axis=1)


def _gmlp_kernel(pc_ref, norm_ref, ws_ref, bs_ref, o_ref):
    gw = GROUP_W
    x = pc_ref[...]
    z = 0.5 * x * (1.0 + jnp.tanh(0.7978845608028654 * (x + 0.044715 * (x * x * x))))
    tm = x.shape[0]
    for g in range(GMLP_GROUPS):
        u = z[:, g * GMLP_CH:(g + 1) * GMLP_CH]
        v = z[:, gw + g * GMLP_CH:gw + (g + 1) * GMLP_CH]
        v = v * lax.rsqrt(jnp.mean(v * v, axis=-1, keepdims=True) + EPS) * norm_ref[g:g + 1, :]
        v = v.astype(BF16)
        ws = ws_ref[g].astype(BF16)
        for c in range(tm // GMLP_CHUNK):
            rows = slice(c * GMLP_CHUNK, (c + 1) * GMLP_CHUNK)
            f = _dot(ws, v[rows]) + bs_ref[g]
            o_ref[rows, g * GMLP_CH:(g + 1) * GMLP_CH] = (u[rows] * f).astype(o_ref.dtype)


def gmlp_mix(pc, norm_g, ws, bs):
    n_batch, lc, c = pc.shape
    bs_b = jnp.broadcast_to(bs[:, :, None], bs.shape + (GMLP_CH,))
    const = lambda a: pl.BlockSpec(a.shape, lambda b, i: (0,) * a.ndim)
    return pl.pallas_call(
        _gmlp_kernel, name="gmlp",
        out_shape=jax.ShapeDtypeStruct((n_batch, lc, GROUP_W), BF16),
        grid=(n_batch, lc // TM),
        in_specs=[pl.BlockSpec((None, TM, c), lambda b, i: (b, i, 0)), const(norm_g), const(ws), const(bs_b)],
        out_specs=pl.BlockSpec((None, TM, GROUP_W), lambda b, i: (b, i, 0)),
        compiler_params=_cparams(("parallel", "arbitrary")),
    )(pc, norm_g, ws, bs_b)


def _rope_tables(seq, ctx):
    half = ATT_HEAD // 4
    inv_freq = ROPE_BASE ** (-np.arange(half, dtype=np.float64) / half)
    pos = np.arange(seq)
    ang_r = (pos // GRID_W)[:, None] * inv_freq[None, :]
    ang_c = (pos % GRID_W)[:, None] * inv_freq[None, :]
    cos = np.concatenate([np.cos(ang_r)] * 2 + [np.cos(ang_c)] * 2, axis=1)
    sin = np.concatenate([-np.sin(ang_r), np.sin(ang_r), -np.sin(ang_c), np.sin(ang_c)], axis=1)
    cos = np.concatenate([cos, np.ones((ctx, ATT_HEAD))], axis=0)
    sin = np.concatenate([sin, np.zeros((ctx, ATT_HEAD))], axis=0)
    return jnp.asarray(np.tile(cos, (1, 2)), F32), jnp.asarray(np.tile(sin, (1, 2)), F32)


def _rope(t, cos, sin):
    n = t.shape[1]
    q = ATT_HEAD // 4
    lane = lax.broadcasted_iota(jnp.int32, t.shape, 1)
    swapped = jnp.where((lane % (2 * q)) < q, pltpu.roll(t, n - q, 1), pltpu.roll(t, q, 1))
    return t * cos + swapped * sin


def _attn_prep_kernel(pd_ref, cos_ref, sin_ref, qg_ref, kg_ref, seg_ref, q_o, k_o, v_o):
    gw = GROUP_W
    kvw = ATT_KV_HEADS * ATT_HEAD
    seg = seg_ref[...]
    inv = 1.0 / ATT_HEAD
    cos, sin = cos_ref[...], sin_ref[...]
    q = pd_ref[:, 0:gw]
    q = q * lax.rsqrt(_segsum(q * q, seg) * inv + EPS) * qg_ref[...]
    q = _rope(q, jnp.concatenate([cos] * (gw // kvw), axis=1), jnp.concatenate([sin] * (gw // kvw), axis=1))
    q_o[...] = (q * (ATT_HEAD ** -0.5)).astype(q_o.dtype)
    k = pd_ref[:, gw:gw + kvw]
    k = k * lax.rsqrt(_segsum(k * k, seg[:kvw, :kvw]) * inv + EPS) * kg_ref[...]
    k = _rope(k, cos, sin)
    v = pd_ref[:, gw + kvw:gw + 2 * kvw]
    lane = lax.broadcasted_iota(jnp.int32, k.shape, 1)

    def variants(t):
        h0 = jnp.where(lane < ATT_HEAD, t, 0.0)
        h1 = jnp.where(lane >= ATT_HEAD, t, 0.0)
        return jnp.concatenate([h0, pltpu.roll(h0, ATT_HEAD, 1), pltpu.roll(h1, ATT_HEAD, 1), h1], axis=1)

    k_o[...] = variants(k).astype(k_o.dtype)
    v_o[...] = variants(v).astype(v_o.dtype)


def attention_prepare(pd, seq, q_gain, k_gain):
    n_batch, lc, c = pd.shape
    gw = GROUP_W
    kvw = ATT_KV_HEADS * ATT_HEAD
    cos, sin = _rope_tables(seq, lc - seq)
    seg = _seg_matrix(gw, ATT_HEAD)
    qg = jnp.tile(q_gain, ATT_Q_HEADS).reshape(1, gw)
    kg = jnp.tile(k_gain, ATT_KV_HEADS).reshape(1, kvw)
    const = lambda a: pl.BlockSpec(a.shape, lambda b, i: (0,) * a.ndim)
    tab = pl.BlockSpec((TM, kvw), lambda b, i: (i, 0))
    row = lambda n: pl.BlockSpec((None, TM, n), lambda b, i: (b, i, 0))
    return pl.pallas_call(
        _attn_prep_kernel, name="attn_prep",
        out_shape=[jax.ShapeDtypeStruct((n_batch, lc, gw), BF16)] * 3,
        grid=(n_batch, lc // TM),
        in_specs=[row(c), tab, tab, const(qg), const(kg), const(seg)],
        out_specs=[row(gw)] * 3,
        compiler_params=_cparams(("parallel", "arbitrary")),
    )(pd, cos, sin, qg, kg, seg)


def _attn_kernel(nb, sink_ref, q_ref, kp_ref, kc_ref, kn_ref, kx_ref, vp_ref, vc_ref, vn_ref, vx_ref, o_ref):
    i = pl.program_id(1)
    blk = q_ref.shape[0]
    is_lat = i < nb
    rowi = lax.broadcasted_iota(jnp.int32, (blk, blk), 0)
    coli = lax.broadcasted_iota(jnp.int32, (blk, blk), 1)
    m_prev = (coli >= rowi) & is_lat & (i >= 1)
    m_cur = jnp.broadcast_to(is_lat, (blk, blk))
    m_next = (coli <= rowi) & (i < nb - 1)
    for p in range(ATT_Q_HEADS // 2):
        q = q_ref[:, p * LANES:(p + 1) * LANES]
        acc = None
        for par in range(2):
            h = 2 * p + par
            g = h // (ATT_Q_HEADS // ATT_KV_HEADS)
            col = slice((2 * g + par) * LANES, (2 * g + par + 1) * LANES)
            sink = sink_ref[h]
            s1 = jnp.where(m_prev, _dot_nt(q, kp_ref[:, col]), NEG_INF)
            s2 = jnp.where(m_cur, _dot_nt(q, kc_ref[:, col]), NEG_INF)
            s3 = jnp.where(m_next, _dot_nt(q, kn_ref[:, col]), NEG_INF)
            sx = _dot_nt(q, kx_ref[:, col])
            m = jnp.maximum(jnp.maximum(jnp.max(jnp.maximum(jnp.maximum(s1, s2), s3), axis=-1, keepdims=True),
                                        jnp.max(sx, axis=-1, keepdims=True)), sink)
            p1, p2, p3, px = jnp.exp(s1 - m), jnp.exp(s2 - m), jnp.exp(s3 - m), jnp.exp(sx - m)
            den = (jnp.sum(p1 + p2 + p3, axis=-1, keepdims=True) + jnp.sum(px, axis=-1, keepdims=True)
                   + jnp.exp(sink - m))
            o = (_dot(p1.astype(BF16), vp_ref[:, col]) + _dot(p2.astype(BF16), vc_ref[:, col])
                 + _dot(p3.astype(BF16), vn_ref[:, col]) + _dot(px.astype(BF16), vx_ref[:, col])) / den
            acc = o if acc is None else acc + o
        o_ref[:, p * LANES:(p + 1) * LANES] = acc.astype(o_ref.dtype)


def window_attention(q, kv, vv, sink, seq):
    n_batch, lc, gw = q.shape
    blk = ATT_BLOCK
    nb = seq // blk
    ctx = lc - seq
    wide = kv.shape[-1]
    qspec = pl.BlockSpec((None, blk, gw), lambda b, i, s: (b, i, 0))
    prev = pl.BlockSpec((None, blk, wide), lambda b, i, s: (b, jnp.clip(i - 1, 0, nb - 1), 0))
    cur = pl.BlockSpec((None, blk, wide), lambda b, i, s: (b, jnp.minimum(i, nb - 1), 0))
    nxt = pl.BlockSpec((None, blk, wide), lambda b, i, s: (b, jnp.clip(i + 1, 0, nb - 1), 0))
    cx = pl.BlockSpec((None, ctx, wide), lambda b, i, s: (b, seq // ctx, 0))
    return pl.pallas_call(
        functools.partial(_attn_kernel, nb), name="window_attn",
        out_shape=jax.ShapeDtypeStruct((n_batch, lc, gw), BF16),
        grid_spec=pltpu.PrefetchScalarGridSpec(
            num_scalar_prefetch=1, grid=(n_batch, lc // blk),
            in_specs=[qspec, prev, cur, nxt, cx, prev, cur, nxt, cx],
            out_specs=qspec),
        compiler_params=_cparams(("parallel", "arbitrary")),
    )(sink.astype(F32), q, kv, kv, kv, kv, vv, vv, vv, vv)


def _top2_rows(val, iota):
    m1 = jnp.max(val, axis=0, keepdims=True)
    i1 = jnp.min(jnp.where(val == m1, iota, SUBLANES), axis=0, keepdims=True)
    rest = jnp.where(iota == i1, -jnp.inf, val)
    m2 = jnp.max(rest, axis=0, keepdims=True)
    i2 = jnp.min(jnp.where(rest == m2, iota, SUBLANES), axis=0, keepdims=True)
    return m1, m2, i1, i2


def _outproj_kernel(ya_ref, yb_ref, yc_ref, yd_ref, x_ref, mod_ref, g_ref, wo_ref, wr_ref, rb_ref,
                    xo_ref, h_ref, e_ref, gw_ref):
    d = x_ref.shape[-1]
    gw = GROUP_W
    acc = None
    for n, y_ref in enumerate((ya_ref, yb_ref, yc_ref, yd_ref)):
        part = _dot(y_ref[...].astype(BF16), wo_ref[n * gw:(n + 1) * gw, :])
        acc = part if acc is None else acc + part
    x = x_ref[...] + mod_ref[:, 2 * d:3 * d] * acc
    xo_ref[...] = x
    y = x * lax.rsqrt(jnp.mean(x * x, axis=-1, keepdims=True) + EPS) * g_ref[...]
    h = y * (1.0 + mod_ref[:, 4 * d:5 * d]) + mod_ref[:, 3 * d:4 * d]
    h_ref[...] = h.astype(h_ref.dtype)
    hh, hl = _split(h)
    wh, wl = _split(wr_ref[...])
    logits = _dot_nt(wh, hh) + _dot_nt(wh, hl) + _dot_nt(wl, hh)
    scores = _sigmoid(logits)
    sel = scores + rb_ref[...]
    tm = x.shape[0]
    iota = lax.broadcasted_iota(jnp.int32, (EXPERTS_PER_GROUP, tm), 0)
    best_val, best = None, None
    for grp in range(N_EXPERT_GROUPS):
        m1, m2, _, _ = _top2_rows(sel[grp * EXPERTS_PER_GROUP:(grp + 1) * EXPERTS_PER_GROUP], iota)
        gs = m1 + m2
        if grp == 0:
            best_val, best = gs, jnp.zeros_like(gs, dtype=jnp.int32)
        else:
            upd = gs > best_val
            best_val = jnp.where(upd, gs, best_val)
            best = jnp.where(upd, grp, best)
    sel_in = jnp.zeros((EXPERTS_PER_GROUP, tm), F32)
    sc_in = jnp.zeros((EXPERTS_PER_GROUP, tm), F32)
    for grp in range(N_EXPERT_GROUPS):
        rows = slice(grp * EXPERTS_PER_GROUP, (grp + 1) * EXPERTS_PER_GROUP)
        hit = best == grp
        sel_in = jnp.where(hit, sel[rows], sel_in)
        sc_in = jnp.where(hit, scores[rows], sc_in)
    _, _, i1, i2 = _top2_rows(sel_in, iota)
    w1 = jnp.sum(jnp.where(iota == i1, sc_in, 0.0), axis=0, keepdims=True)
    w2 = jnp.sum(jnp.where(iota == i2, sc_in, 0.0), axis=0, keepdims=True)
    tot = w1 + w2
    e_ref[0:1, :] = best * EXPERTS_PER_GROUP + i1
    e_ref[1:2, :] = best * EXPERTS_PER_GROUP + i2
    gw_ref[0:1, :] = w1 / tot
    gw_ref[1:2, :] = w2 / tot


def out_projection_router(ys, stream, mod, gain2, w_out, w_router, router_bias, seq):
    n_batch, lc, d = stream.shape
    gw = GROUP_W
    ctx_tile = seq // TM
    wr_t = w_router.T
    rb = jnp.broadcast_to(router_bias.astype(F32)[:, None], (N_EXPERTS, TM))
    row = lambda n: pl.BlockSpec((None, TM, n), lambda b, i: (b, i, 0))
    const = lambda a: pl.BlockSpec(a.shape, lambda b, i: (0,) * a.ndim)
    lane_row = pl.BlockSpec((None, TOP_K, TM), lambda b, i: (b, 0, i))
    return pl.pallas_call(
        _outproj_kernel, name="out_proj_router",
        out_shape=[jax.ShapeDtypeStruct((n_batch, lc, d), F32), jax.ShapeDtypeStruct((n_batch, lc, d), BF16),
                   jax.ShapeDtypeStruct((n_batch, TOP_K, lc), jnp.int32),
                   jax.ShapeDtypeStruct((n_batch, TOP_K, lc), F32)],
        grid=(n_batch, lc // TM),
        in_specs=[row(gw)] * 4 + [row(d), pl.BlockSpec((None, 1, mod.shape[-1]), _mod_index(n_batch, ctx_tile)),
                                  pl.BlockSpec((1, d), lambda b, i: (0, 0)),
                                  pl.BlockSpec(w_out.shape, lambda b, i: (0, 0), pipeline_mode=pl.Buffered(1)),
                                  const(wr_t), const(rb)],
        out_specs=[row(d), row(d), lane_row, lane_row],
        compiler_params=_cparams(("parallel", "arbitrary")),
    )(*ys, stream, mod, gain2.reshape(1, d), w_out, wr_t, rb)


def _expert_kernel(be_ref, nu_ref, x_ref, wg_ref, wu_ref, wd_ref, o_ref, wg_s, wu_s, wd_s):
    i = pl.program_id(0)
    prev = be_ref[jnp.maximum(i - 1, 0)]
    used = i < nu_ref[0]

    @pl.when(used & ((i == 0) | (be_ref[i] != prev)))
    def _():
        wg_s[...] = wg_ref[...].astype(BF16)
        wu_s[...] = wu_ref[...].astype(BF16)
        wd_s[...] = wd_ref[...].astype(BF16)

    @pl.when(used)
    def _():
        x = x_ref[...]
        a = _dot(x, wg_s[...])
        hb = (a * _sigmoid(a)) * _dot(x, wu_s[...])
        o_ref[...] = _dot(hb.astype(BF16), wd_s[...])

    @pl.when(jnp.logical_not(used))
    def _():
        o_ref[...] = jnp.zeros_like(o_ref)


def expert_ffn(xs, block_e, n_used, wg, wu, wd, layer):
    rows, d = xs.shape
    n_blocks = rows // MOE_BLOCK
    de = wg.shape[-1]
    return pl.pallas_call(
        _expert_kernel, name="moe_experts",
        out_shape=jax.ShapeDtypeStruct((rows, d), F32),
        grid_spec=pltpu.PrefetchScalarGridSpec(
            num_scalar_prefetch=2, grid=(n_blocks,),
            in_specs=[pl.BlockSpec((MOE_BLOCK, d), lambda i, be, nu: (i, 0)),
                      pl.BlockSpec((None, None, d, de), lambda i, be, nu: (layer, be[i], 0, 0)),
                      pl.BlockSpec((None, None, d, de), lambda i, be, nu: (layer, be[i], 0, 0)),
                      pl.BlockSpec((None, None, de, d), lambda i, be, nu: (layer, be[i], 0, 0))],
            out_specs=pl.BlockSpec((MOE_BLOCK, d), lambda i, be, nu: (i, 0)),
            scratch_shapes=[pltpu.VMEM((d, de), BF16), pltpu.VMEM((d, de), BF16), pltpu.VMEM((de, d), BF16)]),
        compiler_params=_cparams(("arbitrary",)),
    )(block_e, n_used, xs, wg, wu, wd)


RANK_WIDTHS = (1280, 1024, 768, 512, 256, 128)


def _rank_kernel(e_ref, tri_ref, rank_ref, count_ref, carry_ref):
    first = (pl.program_id(0) == 0) & (pl.program_id(1) == 0)

    @pl.when(first)
    def _():
        carry_ref[...] = jnp.zeros_like(carry_ref)

    width = e_ref.shape[-1]
    expert = lax.broadcasted_iota(jnp.int32, (N_EXPERTS, width), 0)
    carry = carry_ref[...]
    for k in range(TOP_K):
        onehot = jnp.where(expert == e_ref[k:k + 1, :], 1.0, 0.0)
        before = _dot(onehot.astype(BF16), tri_ref[...])
        rank_ref[k:k + 1, :] = jnp.sum(onehot * (before + carry), axis=0, keepdims=True).astype(jnp.int32)
        carry = carry + jnp.sum(onehot, axis=1, keepdims=True)
    carry_ref[...] = carry
    count_ref[...] = jnp.broadcast_to(carry, count_ref.shape)


def expert_ranks(eidx):
    n_batch, top_k, lc = eidx.shape
    width = next(w for w in RANK_WIDTHS if lc % w == 0)
    tri = jnp.asarray(np.triu(np.ones((width, width), np.float32), 1), BF16)
    blk = pl.BlockSpec((None, top_k, width), lambda b, i: (b, 0, i))
    rank, counts = pl.pallas_call(
        _rank_kernel, name="moe_rank",
        out_shape=[jax.ShapeDtypeStruct((n_batch, top_k, lc), jnp.int32),
                   jax.ShapeDtypeStruct((N_EXPERTS, LANES), F32)],
        grid=(n_batch, lc // width),
        in_specs=[blk, pl.BlockSpec(tri.shape, lambda b, i: (0, 0))],
        out_specs=[blk, pl.BlockSpec((N_EXPERTS, LANES), lambda b, i: (0, 0))],
        scratch_shapes=[pltpu.VMEM((N_EXPERTS, 1), F32)],
        compiler_params=_cparams(("arbitrary", "arbitrary")),
    )(eidx, tri)
    return rank, counts[:, 0].astype(jnp.int32)


def _combine_kernel(x_ref, y0_ref, y1_ref, g0_ref, g1_ref, mod_ref, o_ref):
    d = x_ref.shape[-1]
    y = y0_ref[...] * g0_ref[...] + y1_ref[...] * g1_ref[...]
    o_ref[...] = x_ref[...] + mod_ref[:, 5 * d:6 * d] * y


def moe_combine(stream, y0, y1, g0, g1, mod, seq, n_rows):
    n_batch, lc, d = stream.shape
    row = pl.BlockSpec((None, TM, d), lambda b, i: (b, i, 0))
    col = pl.BlockSpec((None, TM, 1), lambda b, i: (b, i, 0))
    return pl.pallas_call(
        _combine_kernel, name="moe_combine",
        out_shape=jax.ShapeDtypeStruct((n_batch, n_rows, d), F32),
        grid=(n_batch, n_rows // TM),
        in_specs=[row, row, row, col, col, pl.BlockSpec((None, 1, mod.shape[-1]), _mod_index(n_batch, seq // TM))],
        out_specs=row,
        compiler_params=_cparams(("parallel", "arbitrary")),
    )(stream, y0, y1, g0, g1, mod)


def moe_ffn(stream, h2, eidx, gates, mod, wg, wu, wd, layer, seq, n_rows):
    n_batch, lc, d = h2.shape
    n_tok = n_batch * lc
    n_assign = n_tok * TOP_K
    rank, counts = expert_ranks(eidx)
    padded = (counts + MOE_BLOCK - 1) // MOE_BLOCK * MOE_BLOCK
    pad_end = jnp.cumsum(padded)
    dest = jnp.take(pad_end - padded, eidx, axis=0) + rank
    n_blocks = n_assign // MOE_BLOCK + N_EXPERTS
    tok = jnp.broadcast_to((jnp.arange(n_batch, dtype=jnp.int32) * lc)[:, None, None]
                           + jnp.arange(lc, dtype=jnp.int32)[None, None, :], dest.shape)
    row_tok = jnp.zeros((n_blocks * MOE_BLOCK,), jnp.int32).at[dest.reshape(-1)].set(tok.reshape(-1))
    block_e = jnp.minimum(jnp.searchsorted(pad_end, jnp.arange(n_blocks, dtype=jnp.int32) * MOE_BLOCK, side='right'),
                          N_EXPERTS - 1).astype(jnp.int32)
    n_used = (pad_end[-1:] // MOE_BLOCK).astype(jnp.int32)
    xs = h2.reshape(n_tok, d)[row_tok]
    ys = expert_ffn(xs, block_e, n_used, wg, wu, wd, layer)
    y0 = ys[dest[:, 0].reshape(-1)].reshape(n_batch, lc, d)
    y1 = ys[dest[:, 1].reshape(-1)].reshape(n_batch, lc, d)
    g0 = gates[:, 0].reshape(n_batch, lc, 1)
    g1 = gates[:, 1].reshape(n_batch, lc, 1)
    return moe_combine(stream, y0, y1, g0, g1, mod, seq, n_rows)


def kernel(x, c, ctx, c_ctx, norm1, norm2, w_ada, b_ada, w_in, w_out, rwkv_conv, rwkv_w0, rwkv_w_up, rwkv_a0,
           rwkv_a_up, rwkv_g_up, rwkv_k_k, rwkv_k_a, rwkv_r_k, rwkv_ln_w, rwkv_ln_b, rwkv_v0, rwkv_v_down,
           rwkv_v_up, fourier_w, fourier_b, gmlp_norm, gmlp_ws, gmlp_bs, attn_q_norm, attn_k_norm, attn_sink,
           w_router, router_bias, w_e_gate, w_e_up, w_e_down):
    n_batch, seq, d = x.shape
    depth = w_in.shape[0]
    stream = jnp.concatenate([x, ctx], axis=1)
    c_all = jnp.zeros((SUBLANES, d), F32).at[:n_batch].set(c).at[n_batch].set(c_ctx)
    col_splits = np.cumsum([A_COLS, B_COLS, C_COLS])
    v_first = None
    for l in range(depth):
        mod = ada_modulation(c_all, w_ada, b_ada, l).reshape(SUBLANES, 1, 6 * d)
        w_parts = [w.astype(BF16) for w in jnp.split(w_in[l], col_splits, axis=1)]
        pa, pb, pc, pd = in_projection(stream, mod, norm1[l], w_parts, seq)

        vres = None if l == 0 else (rwkv_v0[l - 1], rwkv_v_down[l - 1], rwkv_v_up[l - 1])
        r, kk, v, k0, wd0, kka0, k1, wd1, kka1, g = rwkv_prepare(
            pa, seq, rwkv_conv[l], rwkv_w0[l], rwkv_w_up[l], rwkv_a0[l], rwkv_a_up[l], rwkv_g_up[l],
            rwkv_k_k[l], rwkv_k_a[l], v_first, vres)
        if l == 0:
            v_first = v
        yf, yb = rwkv_scan(r, kk, v, ((k0, wd0, kka0), (k1, wd1, kka1)), seq)
        y_a = rwkv_readout(yf, yb, r, k0, k1, v, g, rwkv_r_k[l].reshape(-1), rwkv_ln_w[l], rwkv_ln_b[l])

        y_b = fourier_mix(pb, seq, fourier_w[l], fourier_b[l])
        y_c = gmlp_mix(pc, gmlp_norm[l], gmlp_ws[l], gmlp_bs[l])
        q, kv, vv = attention_prepare(pd, seq, attn_q_norm[l], attn_k_norm[l])
        y_d = window_attention(q, kv, vv, attn_sink[l], seq)

        stream, h2, eidx, gates = out_projection_router(
            (y_a, y_b, y_c, y_d), stream, mod, norm2[l], w_out[l].astype(BF16), w_router, router_bias, seq)
        n_rows = seq if l == depth - 1 else stream.shape[1]
        stream = moe_ffn(stream, h2, eidx, gates, mod, w_e_gate, w_e_up, w_e_down, l, seq, n_rows)
    return stream
```

```python
import functools

import numpy as np
import jax
import jax.numpy as jnp
from jax import lax
from jax.experimental import pallas as pl
from jax.experimental.pallas import tpu as pltpu

F32 = jnp.float32
BF16 = jnp.bfloat16

D_MODEL = 2048
DEPTH = 2
GRID_W = 64
EPS = 1e-6
NEG_INF = -1e30
GROUP_W = 512
RWKV_HEAD = 64
RWKV_HEADS = 8
DECAY_LORA = 64
ICLR_LORA = 64
GATE_LORA = 128
VRES_LORA = 32
RWKV_GN_EPS = 64e-5
FOURIER_GROUPS = 4
FOURIER_CH = 128
GMLP_GROUPS = 4
GMLP_CH = 128
GMLP_CHUNK = 128
ATT_HEAD = 64
ATT_Q_HEADS = 8
ATT_KV_HEADS = 2
WINDOW = 128
ATT_BLOCK = 128
ROPE_BASE = 10000.0
N_EXPERTS = 64
N_EXPERT_GROUPS = 8
EXPERTS_PER_GROUP = 8
TOP_K = 2
D_EXPERT = 512
MOE_BLOCK = 128
A_COLS = 1920
B_COLS = 512
C_COLS = 1024
D_COLS = 768

LANES = 128
SUBLANES = 8
TM = 256
SCAN_TB = 128
SCAN_GROUP = SUBLANES
OUTPROJ_SPLIT = 2
VMEM_LIMIT = 56 * 1024 * 1024


def _cparams(sem, vmem=VMEM_LIMIT):
    return pltpu.CompilerParams(dimension_semantics=sem, vmem_limit_bytes=vmem)


def _dot(a, b):
    return jnp.dot(a, b, preferred_element_type=F32)


def _dot_nt(a, b):
    return lax.dot_general(a, b, (((1,), (1,)), ((), ())), preferred_element_type=F32)


def _dot_tn(a, b):
    return lax.dot_general(a, b, (((0,), (0,)), ((), ())), preferred_element_type=F32)


def _split(x):
    hi = x.astype(BF16)
    lo = (x - hi.astype(F32)).astype(BF16)
    return hi, lo


def _dot3(a, b):
    ah, al = _split(a)
    bh, bl = _split(b)
    return _dot(ah, bh) + _dot(ah, bl) + _dot(al, bh)


def _segsum(x, seg):
    xh, xl = _split(x)
    return _dot(xh, seg) + _dot(xl, seg)


def _sigmoid(x):
    return 1.0 / (1.0 + jnp.exp(-x))


def _seg_matrix(width, seg):
    i = np.arange(width) // seg
    return jnp.asarray((i[:, None] == i[None, :]).astype(np.float32), dtype=BF16)


def _ada_kernel(c_ref, w_ref, b_ref, o_ref):
    c = c_ref[...]
    a = (c * _sigmoid(c)).astype(BF16)
    o_ref[...] = _dot(a, w_ref[...].astype(BF16)) + b_ref[...]


def ada_modulation(c_all, w_ada, b_ada, layer, tn=1024):
    rows, d = c_all.shape
    n_layers, _, n = w_ada.shape
    return pl.pallas_call(
        _ada_kernel, name="ada_mod",
        out_shape=jax.ShapeDtypeStruct((rows, n), F32),
        grid=(n // tn,),
        in_specs=[pl.BlockSpec((rows, d), lambda j: (0, 0)),
                  pl.BlockSpec((None, d, tn), lambda j: (layer, 0, j)),
                  pl.BlockSpec((None, 1, tn), lambda j: (layer, 0, j))],
        out_specs=pl.BlockSpec((rows, tn), lambda j: (0, j)),
        compiler_params=_cparams(("arbitrary",)),
    )(c_all, w_ada, b_ada.reshape(n_layers, 1, n))


def _inproj_kernel(x_ref, mod_ref, g_ref, wa_ref, wb_ref, wc_ref, wd_ref, pa_ref, pb_ref, pc_ref, pd_ref):
    d = x_ref.shape[-1]
    x = x_ref[...]
    y = x * lax.rsqrt(jnp.mean(x * x, axis=-1, keepdims=True) + EPS) * g_ref[...]
    sh = mod_ref[:, 0:d]
    sc = mod_ref[:, d:2 * d]
    h = (y * (1.0 + sc) + sh).astype(BF16)
    pa_ref[...] = _dot(h, wa_ref[...])
    pb_ref[...] = _dot(h, wb_ref[...])
    pc_ref[...] = _dot(h, wc_ref[...])
    pd_ref[...] = _dot(h, wd_ref[...])


def _mod_index(n_batch, ctx_tile):
    return lambda b, i: (jnp.where(i >= ctx_tile, n_batch, b), 0, 0)


def in_projection(stream, mod, gain, w_parts, seq):
    n_batch, lc, d = stream.shape
    ctx_tile = seq // TM
    resident = lambda w: pl.BlockSpec(w.shape, lambda b, i: (0, 0), pipeline_mode=pl.Buffered(1))
    row = lambda n: pl.BlockSpec((None, TM, n), lambda b, i: (b, i, 0))
    return pl.pallas_call(
        _inproj_kernel, name="in_proj",
        out_shape=[jax.ShapeDtypeStruct((n_batch, lc, w.shape[1]), F32) for w in w_parts],
        grid=(n_batch, lc // TM),
        in_specs=[row(d),
                  pl.BlockSpec((None, 1, mod.shape[-1]), _mod_index(n_batch, ctx_tile)),
                  pl.BlockSpec((1, d), lambda b, i: (0, 0))] + [resident(w) for w in w_parts],
        out_specs=[row(w.shape[1]) for w in w_parts],
        compiler_params=_cparams(("parallel", "arbitrary")),
    )(stream, mod, gain.reshape(1, d), *w_parts)


def _rwkv_prep_kernel(has_vres, ctx_tile, *refs):
    (pa_ref, hp_ref, hn_ref, conv_ref, w0_ref, wup_ref, a0_ref, aup_ref, gup_ref, kk_par_ref, ka_par_ref,
     seg_ref) = refs[:12]
    rest = refs[12:]
    if has_vres:
        vf_ref, v0_ref, vdn_ref, vup_ref = rest[:4]
        rest = rest[4:]
    r_o, kk_o, v_o, k0_o, w0_o, kka0_o, k1_o, w1_o, kka1_o, g_o = rest
    gw = GROUP_W
    x = pa_ref[...]
    tm = x.shape[0]
    i = pl.program_id(1)
    first = (i == 0) | (i == ctx_tile)
    last = (i == ctx_tile - 1) | (i == pl.num_programs(1) - 1)
    halo_prev = jnp.where(first, 0.0, hp_ref[SUBLANES - 1:SUBLANES, :])
    halo_next = jnp.where(last, 0.0, hn_ref[0:1, :])
    row = lax.broadcasted_iota(jnp.int32, x.shape, 0)
    x_prev = jnp.where(row == 0, halo_prev, pltpu.roll(x, 1, 0))
    x_next = jnp.where(row == tm - 1, halo_next, pltpu.roll(x, tm - 1, 0))
    y = x_prev * conv_ref[0:1, :] + x * conv_ref[1:2, :] + x_next * conv_ref[2:3, :]
    r = y[:, 0:gw]
    k = y[:, gw:2 * gw]
    v = y[:, 2 * gw:3 * gw]
    wl = y[:, 3 * gw:3 * gw + 2 * DECAY_LORA]
    al = y[:, 3 * gw + 2 * DECAY_LORA:3 * gw + 2 * DECAY_LORA + 2 * ICLR_LORA]
    gl = y[:, 3 * gw + 2 * DECAY_LORA + 2 * ICLR_LORA:]
    if has_vres:
        mix = _sigmoid(v0_ref[...] + _dot3(_dot3(v, vdn_ref[...]), vup_ref[...]))
        v = v + (vf_ref[...] - v) * mix
    kk = k * kk_par_ref[...]
    nrm = jnp.maximum(jnp.sqrt(_segsum(kk * kk, seg_ref[...])), 1e-12)
    kk = kk / nrm
    r_o[...] = r
    kk_o[...] = kk
    v_o[...] = v
    tanh_wl = jnp.tanh(wl)
    for d, (k_o, w_o, kka_o) in enumerate(((k0_o, w0_o, kka0_o), (k1_o, w1_o, kka1_o))):
        w_raw = w0_ref[d:d + 1, :] + _dot3(tanh_wl, wup_ref[d])
        z = -w_raw
        softplus = jnp.maximum(z, 0.0) + jnp.log(1.0 + jnp.exp(-jnp.abs(z)))
        w_o[...] = jnp.exp(-jnp.exp(-softplus - 0.5))
        a = _sigmoid(a0_ref[d:d + 1, :] + _dot3(al, aup_ref[d]))
        k_o[...] = k * (1.0 + (a - 1.0) * ka_par_ref[...])
        kka_o[...] = kk * a
    g_o[...] = _dot3(_sigmoid(gl), gup_ref[...])


def rwkv_prepare(pa, seq, conv_w, w0, w_up, a0, a_up, g_up, k_k, k_a, v_first=None, vres=None):
    n_batch, lc, c = pa.shape
    gw = GROUP_W
    sub_per_tile = TM // SUBLANES
    n_sub = lc // SUBLANES
    pad_rows = lambda w, d, n: jnp.zeros((2 * n, gw), F32).at[d * n:(d + 1) * n].set(w)
    wup = jnp.stack([pad_rows(w_up[d], d, DECAY_LORA) for d in range(2)])
    aup = jnp.stack([pad_rows(a_up[d], d, ICLR_LORA) for d in range(2)])
    seg = _seg_matrix(gw, RWKV_HEAD)
    const = lambda a: pl.BlockSpec(a.shape, lambda b, i: (0,) * a.ndim)
    row = lambda n: pl.BlockSpec((None, TM, n), lambda b, i: (b, i, 0))
    halo_prev = pl.BlockSpec((None, SUBLANES, c), lambda b, i: (b, jnp.maximum(i * sub_per_tile - 1, 0), 0))
    halo_next = pl.BlockSpec((None, SUBLANES, c), lambda b, i: (b, jnp.minimum((i + 1) * sub_per_tile, n_sub - 1), 0))
    args = [pa, pa, pa, conv_w, w0, wup, a0, aup, g_up, k_k.reshape(1, gw), k_a.reshape(1, gw), seg]
    specs = [row(c), halo_prev, halo_next] + [const(a) for a in args[3:]]
    has_vres = vres is not None
    if has_vres:
        v0, v_down, v_up = vres
        vdn = jnp.zeros((gw, LANES), F32).at[:, :VRES_LORA].set(v_down)
        vup = jnp.zeros((LANES, gw), F32).at[:VRES_LORA].set(v_up)
        extra = [v_first, v0.reshape(1, gw), vdn, vup]
        args += extra
        specs += [row(gw)] + [const(a) for a in extra[1:]]
    return pl.pallas_call(
        functools.partial(_rwkv_prep_kernel, has_vres, seq // TM), name="rwkv_prep",
        out_shape=[jax.ShapeDtypeStruct((n_batch, lc, gw), F32)] * 10,
        grid=(n_batch, lc // TM),
        in_specs=specs,
        out_specs=[row(gw)] * 10,
        compiler_params=_cparams(("parallel", "arbitrary")),
    )(*args)


def _scan_kernel(n_batch, rf, kkf, vf, kf, wf, kaf, rb, kkb, vb, kb, wb, kab, bsel_ref, segbd_ref,
                 yf_ref, yb_ref, s_ref, vk_ref, yt_ref):
    tb = rf.shape[1]
    n_pairs = GROUP_W // LANES
    half = RWKV_HEAD

    @pl.when(pl.program_id(0) == 0)
    def _():
        s_ref[...] = jnp.zeros_like(s_ref)
        yt_ref[...] = jnp.zeros_like(yt_ref)

    segbd = segbd_ref[...]
    dirs = ((rf, kkf, vf, kf, wf, kaf, yf_ref), (rb, kkb, vb, kb, wb, kab, yb_ref))
    chains = [(d, b, p) for d in range(2) for b in range(n_batch) for p in range(n_pairs)]
    n_ch = len(chains)
    lane = lax.broadcasted_iota(jnp.int32, (RWKV_HEAD, LANES), 1) % half

    n_sets = 2
    per_set = n_ch // n_sets
    sets = [list(range(si * per_set, (si + 1) * per_set)) for si in range(n_sets)]

    def side_by_side(parts):
        return jnp.concatenate([jnp.concatenate(parts[i:i + 2], axis=1) for i in range(0, per_set, 2)], axis=0)

    def piece(full, n):
        return full[(n // 2) * RWKV_HEAD:(n // 2 + 1) * RWKV_HEAD, (n % 2) * LANES:(n % 2 + 1) * LANES]

    def group(g, carry):
        bases = (pl.multiple_of(g * SCAN_GROUP, SCAN_GROUP),
                 pl.multiple_of(tb - SCAN_GROUP - g * SCAN_GROUP, SCAN_GROUP))

        def row(which, ci, j):
            d, b, p = chains[ci]
            t = j if d == 0 else SCAN_GROUP - 1 - j
            return dirs[d][which][b, pl.ds(bases[d], SCAN_GROUP), p * LANES:(p + 1) * LANES][t:t + 1]

        def sk_dot(states, cs, j):
            return _dot(side_by_side([(states[n] * row(1, ci, j)).astype(BF16) for n, ci in enumerate(cs)]), segbd)

        sk = []
        for cs in sets:
            for ci in cs:
                d, b, p = chains[ci]
                v8 = dirs[d][2][b, pl.ds(bases[d], SCAN_GROUP), p * LANES:(p + 1) * LANES]
                k8 = dirs[d][3][b, pl.ds(bases[d], SCAN_GROUP), p * LANES:(p + 1) * LANES]
                vt = jnp.concatenate([v8[:, :RWKV_HEAD], v8[:, RWKV_HEAD:]], axis=0).T.astype(BF16)
                kmat = jnp.concatenate([bsel_ref[t] * k8[t:t + 1] for t in range(SCAN_GROUP)],
                                       axis=1).astype(BF16)
                vk_ref[ci] = _dot(vt, kmat)
            sk.append(sk_dot([s_ref[ci] for ci in cs], cs, 0))
        for j in range(SCAN_GROUP):
            for si, cs in enumerate(sets):
                states = []
                for n, ci in enumerate(cs):
                    d = chains[ci][0]
                    t = j if d == 0 else SCAN_GROUP - 1 - j
                    s = (s_ref[ci] * row(4, ci, j) - piece(sk[si], n) * row(5, ci, j)
                         + vk_ref[ci, :, t * LANES:(t + 1) * LANES])
                    s_ref[ci] = s
                    states.append(s)
                y_all = _dot(side_by_side([(s * row(0, ci, j)).astype(BF16) for s, ci in zip(states, cs)]), segbd)
                if j + 1 < SCAN_GROUP:
                    sk[si] = sk_dot(states, cs, j + 1)
                for n, ci in enumerate(cs):
                    d = chains[ci][0]
                    t = j if d == 0 else SCAN_GROUP - 1 - j
                    yt_ref[ci] = jnp.where(lane == (bases[d] + t) % half, piece(y_all, n), yt_ref[ci])
        return carry

    def flush(first_row):
        for ci, (d, b, p) in enumerate(chains):
            t = yt_ref[ci].T
            rows = slice(first_row[d], first_row[d] + half)
            dirs[d][6][b, rows, p * LANES:(p + 1) * LANES] = jnp.concatenate([t[:half], t[half:]], axis=1)

    groups_per_flush = half // SCAN_GROUP
    for part in range(tb // half):
        lax.fori_loop(part * groups_per_flush, (part + 1) * groups_per_flush, group, 0)
        flush((part * half, tb - (part + 1) * half))


def _scan_selectors():
    bsel = np.zeros((SCAN_GROUP, 2 * SCAN_GROUP, LANES), np.float32)
    for h in range(2):
        for j in range(SCAN_GROUP):
            bsel[j, h * SCAN_GROUP + j, h * RWKV_HEAD:(h + 1) * RWKV_HEAD] = 1.0
    return jnp.asarray(bsel, F32), _seg_matrix(2 * LANES, RWKV_HEAD)


def rwkv_scan(r, kk, v, dirs, seq):
    n_batch, lc, gw = r.shape
    tb = SCAN_TB
    nx = seq // tb
    nc = (lc - seq) // tb
    n_chains = 2 * n_batch * (gw // LANES)
    fwd = lambda i: (0, jnp.where(i < nc, nx + i, i - nc), 0)
    bwd = lambda i: (0, nx + nc - 1 - i, 0)
    blk = lambda m: pl.BlockSpec((n_batch, tb, gw), m)
    const = lambda a: pl.BlockSpec(a.shape, lambda i: (0,) * a.ndim)
    bsel, segbd = _scan_selectors()
    return pl.pallas_call(
        functools.partial(_scan_kernel, n_batch), name="rwkv_scan",
        out_shape=[jax.ShapeDtypeStruct((n_batch, lc, gw), F32)] * 2,
        grid=(nx + nc,),
        in_specs=[blk(fwd)] * 6 + [blk(bwd)] * 6 + [const(bsel), const(segbd)],
        out_specs=[blk(fwd), blk(bwd)],
        scratch_shapes=[pltpu.VMEM((n_chains, RWKV_HEAD, LANES), F32),
                        pltpu.VMEM((n_chains, RWKV_HEAD, SCAN_GROUP * LANES), F32),
                        pltpu.VMEM((n_chains, RWKV_HEAD, LANES), F32)],
        compiler_params=_cparams(("arbitrary",)),
    )(r, kk, v, *dirs[0], r, kk, v, *dirs[1], bsel, segbd)


def _rwkv_readout_kernel(yf_ref, yb_ref, r_ref, k0_ref, k1_ref, v_ref, g_ref, rk_ref, lnw_ref, lnb_ref, seg_ref,
                         o_ref):
    seg = seg_ref[...]
    inv = 1.0 / RWKV_HEAD
    y = yf_ref[...] + yb_ref[...]
    mu = _segsum(y, seg) * inv
    dlt = y - mu
    var = _segsum(dlt * dlt, seg) * inv
    yn = dlt * lax.rsqrt(var + RWKV_GN_EPS) * lnw_ref[...] + lnb_ref[...]
    kbar = 0.5 * (k0_ref[...] + k1_ref[...])
    bonus = _segsum(r_ref[...] * kbar * rk_ref[...], seg) * v_ref[...]
    o_ref[...] = ((yn + bonus) * g_ref[...]).astype(o_ref.dtype)


def rwkv_readout(yf, yb, r, k0, k1, v, g, r_k, ln_w, ln_b):
    n_batch, lc, gw = r.shape
    seg = _seg_matrix(gw, RWKV_HEAD)
    row = pl.BlockSpec((None, TM, gw), lambda b, i: (b, i, 0))
    const = lambda a: pl.BlockSpec(a.shape, lambda b, i: (0,) * a.ndim)
    params = [r_k.reshape(1, gw), ln_w.reshape(1, gw), ln_b.reshape(1, gw), seg]
    return pl.pallas_call(
        _rwkv_readout_kernel, name="rwkv_readout",
        out_shape=jax.ShapeDtypeStruct((n_batch, lc, gw), BF16),
        grid=(n_batch, lc // TM),
        in_specs=[row] * 7 + [const(a) for a in params],
        out_specs=row,
        compiler_params=_cparams(("parallel", "arbitrary")),
    )(yf, yb, r, k0, k1, v, g, *params)


def _dft_tables(n):
    k = np.arange(n)
    ang = 2.0 * np.pi * ((k[:, None] * k[None, :]) % n) / n
    return np.cos(ang), np.sin(ang)


def _fft_a_kernel(nb, f_ref, gr_ref, gi_ref, zr_ref, zi_ref):
    n1 = gr_ref.shape[1]
    n2_total = f_ref.shape[0] // n1
    j = pl.program_id(2)
    for i in range(nb):
        rows = f_ref[pl.ds(j * nb + i, n1, stride=n2_total), :].astype(BF16)
        zr_ref[i * n1:(i + 1) * n1, :] = _dot(gr_ref[i], rows)
        zi_ref[i * n1:(i + 1) * n1, :] = _dot(gi_ref[i], rows)


def _fft_b_kernel(kb, scale, zr_ref, zi_ref, fc_ref, fs_ref, cc_ref, cs_ref, w_ref, b_ref, o_ref):
    n2 = fc_ref.shape[0]
    n1 = zr_ref.shape[0] // n2
    j = pl.program_id(2)
    fc, fs = fc_ref[...], fs_ref[...]
    for i in range(kb):
        k1 = j * kb + i
        zr = zr_ref[pl.ds(k1, n2, stride=n1), :].astype(BF16)
        zi = zi_ref[pl.ds(k1, n2, stride=n1), :].astype(BF16)
        xr = _dot(fc, zr) + _dot(fs, zi)
        xi = _dot(fc, zi) - _dot(fs, zr)
        re = (_dot(xr.astype(BF16), cc_ref[...]) + _dot(xi.astype(BF16), cs_ref[...])) * scale
        o_ref[pl.ds(k1, n2, stride=n1), :] = _dot(re.astype(BF16), w_ref[...].astype(BF16)) + b_ref[...]


def _dft_dense_kernel(scale, f_ref, fc_ref, fs_ref, cc_ref, cs_ref, w_ref, b_ref, o_ref):
    f = f_ref[...].astype(BF16)
    xr = _dot(fc_ref[...], f)
    xi = -_dot(fs_ref[...], f)
    re = (_dot(xr.astype(BF16), cc_ref[...]) + _dot(xi.astype(BF16), cs_ref[...])) * scale
    o_ref[...] = _dot(re.astype(BF16), w_ref[...].astype(BF16)) + b_ref[...]


def fourier_mix(pb, seq, w, bias):
    n_batch, lc, gw = pb.shape
    ctx = lc - seq
    ch = FOURIER_CH
    groups = FOURIER_GROUPS
    n2 = LANES
    n1 = seq // n2
    cc, cs = (jnp.asarray(t, BF16) for t in _dft_tables(ch))
    bias3 = bias.reshape(groups, 1, ch)

    k1 = np.arange(n1)[None, :, None]
    m1 = np.arange(n1)[None, None, :]
    m2 = np.arange(n2)[:, None, None]
    ang = 2.0 * np.pi * ((k1 * (n2 * m1 + m2)) % seq) / seq
    g_re = jnp.asarray(np.cos(ang), BF16)
    g_im = jnp.asarray(-np.sin(ang), BF16)
    nb = min(16, n2)
    slab = pl.BlockSpec((None, seq, ch), lambda b, g, j: (b, 0, g))
    gspec = pl.BlockSpec((nb, n1, n1), lambda b, g, j: (j, 0, 0))
    zspec = pl.BlockSpec((None, None, nb * n1, ch), lambda b, g, j: (b, g, j, 0))
    zr, zi = pl.pallas_call(
        functools.partial(_fft_a_kernel, nb), name="fft_stage_a",
        out_shape=[jax.ShapeDtypeStruct((n_batch, groups, seq, ch), F32)] * 2,
        grid=(n_batch, groups, n2 // nb),
        in_specs=[slab, gspec, gspec],
        out_specs=[zspec, zspec],
        compiler_params=_cparams(("parallel", "parallel", "arbitrary")),
    )(pb, g_re, g_im)

    fc, fs = (jnp.asarray(t, BF16) for t in _dft_tables(n2))
    kb = min(16, n1)
    scale = 1.0 / np.sqrt(float(seq) * ch)
    zslab = pl.BlockSpec((None, None, seq, ch), lambda b, g, j: (b, g, 0, 0), pipeline_mode=pl.Buffered(1))
    const = lambda a: pl.BlockSpec(a.shape, lambda b, g, j: (0,) * a.ndim)
    wspec = pl.BlockSpec((None, ch, ch), lambda b, g, j: (g, 0, 0))
    bspec = pl.BlockSpec((None, 1, ch), lambda b, g, j: (g, 0, 0))
    y_x = pl.pallas_call(
        functools.partial(_fft_b_kernel, kb, scale), name="fft_stage_b",
        out_shape=jax.ShapeDtypeStruct((n_batch, seq, gw), F32),
        grid=(n_batch, groups, n1 // kb),
        in_specs=[zslab, zslab, const(fc), const(fs), const(cc), const(cs), wspec, bspec],
        out_specs=pl.BlockSpec((None, seq, ch), lambda b, g, j: (b, 0, g)),
        compiler_params=_cparams(("parallel", "parallel", "arbitrary")),
    )(zr, zi, fc, fs, cc, cs, w, bias3)

    fcc, fsc = (jnp.asarray(t, BF16) for t in _dft_tables(ctx))
    const2 = lambda a: pl.BlockSpec(a.shape, lambda b, g: (0,) * a.ndim)
    ctx_tile = seq // ctx
    y_c = pl.pallas_call(
        functools.partial(_dft_dense_kernel, 1.0 / np.sqrt(float(ctx) * ch)), name="dft_ctx",
        out_shape=jax.ShapeDtypeStruct((n_batch, ctx, gw), F32),
        grid=(n_batch, groups),
        in_specs=[pl.BlockSpec((None, ctx, ch), lambda b, g: (b, ctx_tile, g)),
                  const2(fcc), const2(fsc), const2(cc), const2(cs),
                  pl.BlockSpec((None, ch, ch), lambda b, g: (g, 0, 0)),
                  pl.BlockSpec((None, 1, ch), lambda b, g: (g, 0, 0))],
        out_specs=pl.BlockSpec((None, ctx, ch), lambda b, g: (b, 0, g)),
        compiler_params=_cparams(("parallel", "arbitrary")),
    )(pb, fcc, fsc, cc, cs, w, bias3)
    return jnp.concatenate([y_x, y_c], axis=1)


def _gmlp_kernel(pc_ref, norm_ref, ws_ref, bs_ref, o_ref):
    gw = GROUP_W
    x = pc_ref[...]
    z = 0.5 * x * (1.0 + jnp.tanh(0.7978845608028654 * (x + 0.044715 * (x * x * x))))
    tm = x.shape[0]
    for g in range(GMLP_GROUPS):
        u = z[:, g * GMLP_CH:(g + 1) * GMLP_CH]
        v = z[:, gw + g * GMLP_CH:gw + (g + 1) * GMLP_CH]
        v = v * lax.rsqrt(jnp.mean(v * v, axis=-1, keepdims=True) + EPS) * norm_ref[g:g + 1, :]
        v = v.astype(BF16)
        ws = ws_ref[g].astype(BF16)
        for c in range(tm // GMLP_CHUNK):
            rows = slice(c * GMLP_CHUNK, (c + 1) * GMLP_CHUNK)
            f = _dot(ws, v[rows]) + bs_ref[g]
            o_ref[rows, g * GMLP_CH:(g + 1) * GMLP_CH] = (u[rows] * f).astype(o_ref.dtype)


def gmlp_mix(pc, norm_g, ws, bs):
    n_batch, lc, c = pc.shape
    bs_b = jnp.broadcast_to(bs[:, :, None], bs.shape + (GMLP_CH,))
    const = lambda a: pl.BlockSpec(a.shape, lambda b, i: (0,) * a.ndim)
    return pl.pallas_call(
        _gmlp_kernel, name="gmlp",
        out_shape=jax.ShapeDtypeStruct((n_batch, lc, GROUP_W), BF16),
        grid=(n_batch, lc // TM),
        in_specs=[pl.BlockSpec((None, TM, c), lambda b, i: (b, i, 0)), const(norm_g), const(ws), const(bs_b)],
        out_specs=pl.BlockSpec((None, TM, GROUP_W), lambda b, i: (b, i, 0)),
        compiler_params=_cparams(("parallel", "arbitrary")),
    )(pc, norm_g, ws, bs_b)


def _rope_tables(seq, ctx):
    half = ATT_HEAD // 4
    inv_freq = ROPE_BASE ** (-np.arange(half, dtype=np.float64) / half)
    pos = np.arange(seq)
    ang_r = (pos // GRID_W)[:, None] * inv_freq[None, :]
    ang_c = (pos % GRID_W)[:, None] * inv_freq[None, :]
    cos = np.concatenate([np.cos(ang_r)] * 2 + [np.cos(ang_c)] * 2, axis=1)
    sin = np.concatenate([-np.sin(ang_r), np.sin(ang_r), -np.sin(ang_c), np.sin(ang_c)], axis=1)
    cos = np.concatenate([cos, np.ones((ctx, ATT_HEAD))], axis=0)
    sin = np.concatenate([sin, np.zeros((ctx, ATT_HEAD))], axis=0)
    return jnp.asarray(np.tile(cos, (1, 2)), F32), jnp.asarray(np.tile(sin, (1, 2)), F32)


def _rope(t, cos, sin):
    n = t.shape[1]
    q = ATT_HEAD // 4
    lane = lax.broadcasted_iota(jnp.int32, t.shape, 1)
    swapped = jnp.where((lane % (2 * q)) < q, pltpu.roll(t, n - q, 1), pltpu.roll(t, q, 1))
    return t * cos + swapped * sin


def _attn_prep_kernel(pd_ref, cos_ref, sin_ref, qg_ref, kg_ref, seg_ref, q_o, k_o, v_o):
    gw = GROUP_W
    kvw = ATT_KV_HEADS * ATT_HEAD
    seg = seg_ref[...]
    inv = 1.0 / ATT_HEAD
    cos, sin = cos_ref[...], sin_ref[...]
    q = pd_ref[:, 0:gw]
    q = q * lax.rsqrt(_segsum(q * q, seg) * inv + EPS) * qg_ref[...]
    q = _rope(q, jnp.concatenate([cos] * (gw // kvw), axis=1), jnp.concatenate([sin] * (gw // kvw), axis=1))
    q_o[...] = (q * (ATT_HEAD ** -0.5)).astype(q_o.dtype)
    k = pd_ref[:, gw:gw + kvw]
    k = k * lax.rsqrt(_segsum(k * k, seg[:kvw, :kvw]) * inv + EPS) * kg_ref[...]
    k = _rope(k, cos, sin)
    v = pd_ref[:, gw + kvw:gw + 2 * kvw]
    lane = lax.broadcasted_iota(jnp.int32, k.shape, 1)

    def variants(t):
        h0 = jnp.where(lane < ATT_HEAD, t, 0.0)
        h1 = jnp.where(lane >= ATT_HEAD, t, 0.0)
        return jnp.concatenate([h0, pltpu.roll(h0, ATT_HEAD, 1), pltpu.roll(h1, ATT_HEAD, 1), h1], axis=1)

    k_o[...] = variants(k).astype(k_o.dtype)
    v_o[...] = variants(v).astype(v_o.dtype)


def attention_prepare(pd, seq, q_gain, k_gain):
    n_batch, lc, c = pd.shape
    gw = GROUP_W
    kvw = ATT_KV_HEADS * ATT_HEAD
    cos, sin = _rope_tables(seq, lc - seq)
    seg = _seg_matrix(gw, ATT_HEAD)
    qg = jnp.tile(q_gain, ATT_Q_HEADS).reshape(1, gw)
    kg = jnp.tile(k_gain, ATT_KV_HEADS).reshape(1, kvw)
    const = lambda a: pl.BlockSpec(a.shape, lambda b, i: (0,) * a.ndim)
    tab = pl.BlockSpec((TM, kvw), lambda b, i: (i, 0))
    row = lambda n: pl.BlockSpec((None, TM, n), lambda b, i: (b, i, 0))
    return pl.pallas_call(
        _attn_prep_kernel, name="attn_prep",
        out_shape=[jax.ShapeDtypeStruct((n_batch, lc, gw), BF16)] * 3,
        grid=(n_batch, lc // TM),
        in_specs=[row(c), tab, tab, const(qg), const(kg), const(seg)],
        out_specs=[row(gw)] * 3,
        compiler_params=_cparams(("parallel", "arbitrary")),
    )(pd, cos, sin, qg, kg, seg)


def _attn_kernel(nb, sink_ref, q_ref, kp_ref, kc_ref, kn_ref, kx_ref, vp_ref, vc_ref, vn_ref, vx_ref, o_ref):
    i = pl.program_id(1)
    blk = q_ref.shape[0]
    is_lat = i < nb
    rowi = lax.broadcasted_iota(jnp.int32, (blk, blk), 0)
    coli = lax.broadcasted_iota(jnp.int32, (blk, blk), 1)
    m_prev = (coli >= rowi) & is_lat & (i >= 1)
    m_cur = jnp.broadcast_to(is_lat, (blk, blk))
    m_next = (coli <= rowi) & (i < nb - 1)
    for p in range(ATT_Q_HEADS // 2):
        q = q_ref[:, p * LANES:(p + 1) * LANES]
        acc = None
        for par in range(2):
            h = 2 * p + par
            g = h // (ATT_Q_HEADS // ATT_KV_HEADS)
            col = slice((2 * g + par) * LANES, (2 * g + par + 1) * LANES)
            sink = sink_ref[h]
            s1 = jnp.where(m_prev, _dot_nt(q, kp_ref[:, col]), NEG_INF)
            s2 = jnp.where(m_cur, _dot_nt(q, kc_ref[:, col]), NEG_INF)
            s3 = jnp.where(m_next, _dot_nt(q, kn_ref[:, col]), NEG_INF)
            sx = _dot_nt(q, kx_ref[:, col])
            m = jnp.maximum(jnp.maximum(jnp.max(jnp.maximum(jnp.maximum(s1, s2), s3), axis=-1, keepdims=True),
                                        jnp.max(sx, axis=-1, keepdims=True)), sink)
            p1, p2, p3, px = jnp.exp(s1 - m), jnp.exp(s2 - m), jnp.exp(s3 - m), jnp.exp(sx - m)
            den = (jnp.sum(p1 + p2 + p3, axis=-1, keepdims=True) + jnp.sum(px, axis=-1, keepdims=True)
                   + jnp.exp(sink - m))
            o = (_dot(p1.astype(BF16), vp_ref[:, col]) + _dot(p2.astype(BF16), vc_ref[:, col])
                 + _dot(p3.astype(BF16), vn_ref[:, col]) + _dot(px.astype(BF16), vx_ref[:, col])) / den
            acc = o if acc is None else acc + o
        o_ref[:, p * LANES:(p + 1) * LANES] = acc.astype(o_ref.dtype)


def window_attention(q, kv, vv, sink, seq):
    n_batch, lc, gw = q.shape
    blk = ATT_BLOCK
    nb = seq // blk
    ctx = lc - seq
    wide = kv.shape[-1]
    qspec = pl.BlockSpec((None, blk, gw), lambda b, i, s: (b, i, 0))
    prev = pl.BlockSpec((None, blk, wide), lambda b, i, s: (b, jnp.clip(i - 1, 0, nb - 1), 0))
    cur = pl.BlockSpec((None, blk, wide), lambda b, i, s: (b, jnp.minimum(i, nb - 1), 0))
    nxt = pl.BlockSpec((None, blk, wide), lambda b, i, s: (b, jnp.clip(i + 1, 0, nb - 1), 0))
    cx = pl.BlockSpec((None, ctx, wide), lambda b, i, s: (b, seq // ctx, 0))
    return pl.pallas_call(
        functools.partial(_attn_kernel, nb), name="window_attn",
        out_shape=jax.ShapeDtypeStruct((n_batch, lc, gw), BF16),
        grid_spec=pltpu.PrefetchScalarGridSpec(
            num_scalar_prefetch=1, grid=(n_batch, lc // blk),
            in_specs=[qspec, prev, cur, nxt, cx, prev, cur, nxt, cx],
            out_specs=qspec),
        compiler_params=_cparams(("parallel", "arbitrary")),
    )(sink.astype(F32), q, kv, kv, kv, kv, vv, vv, vv, vv)


def _top2_rows(val, iota):
    m1 = jnp.max(val, axis=0, keepdims=True)
    i1 = jnp.min(jnp.where(val == m1, iota, SUBLANES), axis=0, keepdims=True)
    rest = jnp.where(iota == i1, -jnp.inf, val)
    m2 = jnp.max(rest, axis=0, keepdims=True)
    i2 = jnp.min(jnp.where(rest == m2, iota, SUBLANES), axis=0, keepdims=True)
    return m1, m2, i1, i2


def _outproj_kernel(ya_ref, yb_ref, yc_ref, yd_ref, x_ref, mod_ref, g_ref, wo_ref, wr_ref, rb_ref,
                    xo_ref, h_ref, e_ref, gw_ref):
    tm = x_ref.shape[0] // OUTPROJ_SPLIT
    for part in range(OUTPROJ_SPLIT):
        _outproj_rows(slice(part * tm, (part + 1) * tm), ya_ref, yb_ref, yc_ref, yd_ref, x_ref, mod_ref, g_ref,
                      wo_ref, wr_ref, rb_ref, xo_ref, h_ref, e_ref, gw_ref)


def _outproj_rows(rows, ya_ref, yb_ref, yc_ref, yd_ref, x_ref, mod_ref, g_ref, wo_ref, wr_ref, rb_ref,
                  xo_ref, h_ref, e_ref, gw_ref):
    d = x_ref.shape[-1]
    gw = GROUP_W
    acc = None
    for n, y_ref in enumerate((ya_ref, yb_ref, yc_ref, yd_ref)):
        part = _dot(y_ref[rows, :].astype(BF16), wo_ref[n * gw:(n + 1) * gw, :])
        acc = part if acc is None else acc + part
    x = x_ref[rows, :] + mod_ref[:, 2 * d:3 * d] * acc
    xo_ref[rows, :] = x
    y = x * lax.rsqrt(jnp.mean(x * x, axis=-1, keepdims=True) + EPS) * g_ref[...]
    h = y * (1.0 + mod_ref[:, 4 * d:5 * d]) + mod_ref[:, 3 * d:4 * d]
    h_ref[rows, :] = h.astype(h_ref.dtype)
    hh, hl = _split(h)
    wh, wl = _split(wr_ref[...])
    logits = _dot_nt(wh, hh) + _dot_nt(wh, hl) + _dot_nt(wl, hh)
    scores = _sigmoid(logits)
    sel = scores + rb_ref[:, rows]
    tm = x.shape[0]
    iota = lax.broadcasted_iota(jnp.int32, (EXPERTS_PER_GROUP, tm), 0)
    best_val, best = None, None
    for grp in range(N_EXPERT_GROUPS):
        m1, m2, _, _ = _top2_rows(sel[grp * EXPERTS_PER_GROUP:(grp + 1) * EXPERTS_PER_GROUP], iota)
        gs = m1 + m2
        if grp == 0:
            best_val, best = gs, jnp.zeros_like(gs, dtype=jnp.int32)
        else:
            upd = gs > best_val
            best_val = jnp.where(upd, gs, best_val)
            best = jnp.where(upd, grp, best)
    sel_in = jnp.zeros((EXPERTS_PER_GROUP, tm), F32)
    sc_in = jnp.zeros((EXPERTS_PER_GROUP, tm), F32)
    for grp in range(N_EXPERT_GROUPS):
        members = slice(grp * EXPERTS_PER_GROUP, (grp + 1) * EXPERTS_PER_GROUP)
        hit = best == grp
        sel_in = jnp.where(hit, sel[members], sel_in)
        sc_in = jnp.where(hit, scores[members], sc_in)
    _, _, i1, i2 = _top2_rows(sel_in, iota)
    w1 = jnp.sum(jnp.where(iota == i1, sc_in, 0.0), axis=0, keepdims=True)
    w2 = jnp.sum(jnp.where(iota == i2, sc_in, 0.0), axis=0, keepdims=True)
    tot = w1 + w2
    e_ref[0:1, rows] = best * EXPERTS_PER_GROUP + i1
    e_ref[1:2, rows] = best * EXPERTS_PER_GROUP + i2
    gw_ref[0:1, rows] = w1 / tot
    gw_ref[1:2, rows] = w2 / tot


def out_projection_router(ys, stream, mod, gain2, w_out, w_router, router_bias, seq):
    n_batch, lc, d = stream.shape
    gw = GROUP_W
    ctx_tile = seq // TM
    wr_t = w_router.T
    rb = jnp.broadcast_to(router_bias.astype(F32)[:, None], (N_EXPERTS, TM))
    row = lambda n: pl.BlockSpec((None, TM, n), lambda b, i: (b, i, 0))
    const = lambda a: pl.BlockSpec(a.shape, lambda b, i: (0,) * a.ndim)
    lane_row = pl.BlockSpec((None, TOP_K, TM), lambda b, i: (b, 0, i))
    return pl.pallas_call(
        _outproj_kernel, name="out_proj_router",
        out_shape=[jax.ShapeDtypeStruct((n_batch, lc, d), F32), jax.ShapeDtypeStruct((n_batch, lc, d), BF16),
                   jax.ShapeDtypeStruct((n_batch, TOP_K, lc), jnp.int32),
                   jax.ShapeDtypeStruct((n_batch, TOP_K, lc), F32)],
        grid=(n_batch, lc // TM),
        in_specs=[row(gw)] * 4 + [row(d), pl.BlockSpec((None, 1, mod.shape[-1]), _mod_index(n_batch, ctx_tile)),
                                  pl.BlockSpec((1, d), lambda b, i: (0, 0)),
                                  pl.BlockSpec(w_out.shape, lambda b, i: (0, 0), pipeline_mode=pl.Buffered(1)),
                                  const(wr_t), const(rb)],
        out_specs=[row(d), row(d), lane_row, lane_row],
        compiler_params=_cparams(("parallel", "arbitrary")),
    )(*ys, stream, mod, gain2.reshape(1, d), w_out, wr_t, rb)


def _expert_kernel(be_ref, nu_ref, x_ref, wg_ref, wu_ref, wd_ref, o_ref, wg_s, wu_s, wd_s):
    i = pl.program_id(0)
    prev = be_ref[jnp.maximum(i - 1, 0)]
    used = i < nu_ref[0]

    @pl.when(used & ((i == 0) | (be_ref[i] != prev)))
    def _():
        wg_s[...] = wg_ref[...].astype(BF16)
        wu_s[...] = wu_ref[...].astype(BF16)
        wd_s[...] = wd_ref[...].astype(BF16)

    @pl.when(used)
    def _():
        x = x_ref[...]
        a = _dot(x, wg_s[...])
        hb = (a * _sigmoid(a)) * _dot(x, wu_s[...])
        o_ref[...] = _dot(hb.astype(BF16), wd_s[...])

    @pl.when(jnp.logical_not(used))
    def _():
        o_ref[...] = jnp.zeros_like(o_ref)


def expert_ffn(xs, block_e, n_used, wg, wu, wd, layer):
    rows, d = xs.shape
    n_blocks = rows // MOE_BLOCK
    de = wg.shape[-1]
    return pl.pallas_call(
        _expert_kernel, name="moe_experts",
        out_shape=jax.ShapeDtypeStruct((rows, d), F32),
        grid_spec=pltpu.PrefetchScalarGridSpec(
            num_scalar_prefetch=2, grid=(n_blocks,),
            in_specs=[pl.BlockSpec((MOE_BLOCK, d), lambda i, be, nu: (i, 0)),
                      pl.BlockSpec((None, None, d, de), lambda i, be, nu: (layer, be[i], 0, 0)),
                      pl.BlockSpec((None, None, d, de), lambda i, be, nu: (layer, be[i], 0, 0)),
                      pl.BlockSpec((None, None, de, d), lambda i, be, nu: (layer, be[i], 0, 0))],
            out_specs=pl.BlockSpec((MOE_BLOCK, d), lambda i, be, nu: (i, 0)),
            scratch_shapes=[pltpu.VMEM((d, de), BF16), pltpu.VMEM((d, de), BF16), pltpu.VMEM((de, d), BF16)]),
        compiler_params=_cparams(("arbitrary",)),
    )(block_e, n_used, xs, wg, wu, wd)


RANK_WIDTHS = (1280, 1024, 768, 512, 256, 128)


def _rank_kernel(e_ref, tri_ref, rank_ref, count_ref, carry_ref):
    first = (pl.program_id(0) == 0) & (pl.program_id(1) == 0)

    @pl.when(first)
    def _():
        carry_ref[...] = jnp.zeros_like(carry_ref)

    width = e_ref.shape[-1]
    expert = lax.broadcasted_iota(jnp.int32, (N_EXPERTS, width), 0)
    carry = carry_ref[...]
    for k in range(TOP_K):
        onehot = jnp.where(expert == e_ref[k:k + 1, :], 1.0, 0.0)
        before = _dot(onehot.astype(BF16), tri_ref[...])
        rank_ref[k:k + 1, :] = jnp.sum(onehot * (before + carry), axis=0, keepdims=True).astype(jnp.int32)
        carry = carry + jnp.sum(onehot, axis=1, keepdims=True)
    carry_ref[...] = carry
    count_ref[...] = jnp.broadcast_to(carry, count_ref.shape)


def expert_ranks(eidx):
    n_batch, top_k, lc = eidx.shape
    width = next(w for w in RANK_WIDTHS if lc % w == 0)
    tri = jnp.asarray(np.triu(np.ones((width, width), np.float32), 1), BF16)
    blk = pl.BlockSpec((None, top_k, width), lambda b, i: (b, 0, i))
    rank, counts = pl.pallas_call(
        _rank_kernel, name="moe_rank",
        out_shape=[jax.ShapeDtypeStruct((n_batch, top_k, lc), jnp.int32),
                   jax.ShapeDtypeStruct((N_EXPERTS, LANES), F32)],
        grid=(n_batch, lc // width),
        in_specs=[blk, pl.BlockSpec(tri.shape, lambda b, i: (0, 0))],
        out_specs=[blk, pl.BlockSpec((N_EXPERTS, LANES), lambda b, i: (0, 0))],
        scratch_shapes=[pltpu.VMEM((N_EXPERTS, 1), F32)],
        compiler_params=_cparams(("arbitrary", "arbitrary")),
    )(eidx, tri)
    return rank, counts[:, 0].astype(jnp.int32)


def _combine_kernel(x_ref, y0_ref, y1_ref, g0_ref, g1_ref, mod_ref, o_ref):
    d = x_ref.shape[-1]
    y = y0_ref[...] * g0_ref[...] + y1_ref[...] * g1_ref[...]
    o_ref[...] = x_ref[...] + mod_ref[:, 5 * d:6 * d] * y


def moe_combine(stream, y0, y1, g0, g1, mod, seq, n_rows):
    n_batch, lc, d = stream.shape
    row = pl.BlockSpec((None, TM, d), lambda b, i: (b, i, 0))
    col = pl.BlockSpec((None, TM, 1), lambda b, i: (b, i, 0))
    return pl.pallas_call(
        _combine_kernel, name="moe_combine",
        out_shape=jax.ShapeDtypeStruct((n_batch, n_rows, d), F32),
        grid=(n_batch, n_rows // TM),
        in_specs=[row, row, row, col, col, pl.BlockSpec((None, 1, mod.shape[-1]), _mod_index(n_batch, seq // TM))],
        out_specs=row,
        compiler_params=_cparams(("parallel", "arbitrary")),
    )(stream, y0, y1, g0, g1, mod)


def moe_ffn(stream, h2, eidx, gates, mod, wg, wu, wd, layer, seq, n_rows):
    n_batch, lc, d = h2.shape
    n_tok = n_batch * lc
    n_assign = n_tok * TOP_K
    rank, counts = expert_ranks(eidx)
    padded = (counts + MOE_BLOCK - 1) // MOE_BLOCK * MOE_BLOCK
    pad_end = jnp.cumsum(padded)
    experts = jnp.arange(N_EXPERTS, dtype=jnp.int32)
    start = jnp.sum(jnp.where(eidx[..., None] == experts, pad_end - padded, 0), axis=-1)
    dest = start + rank
    n_blocks = n_assign // MOE_BLOCK + N_EXPERTS
    tok = jnp.broadcast_to((jnp.arange(n_batch, dtype=jnp.int32) * lc)[:, None, None]
                           + jnp.arange(lc, dtype=jnp.int32)[None, None, :], dest.shape)
    row_tok = jnp.zeros((n_blocks * MOE_BLOCK,), jnp.int32).at[dest.reshape(-1)].set(tok.reshape(-1))
    block_row = jnp.arange(n_blocks, dtype=jnp.int32) * MOE_BLOCK
    block_e = jnp.minimum(jnp.sum((pad_end[None, :] <= block_row[:, None]).astype(jnp.int32), axis=1), N_EXPERTS - 1)
    n_used = (pad_end[-1:] // MOE_BLOCK).astype(jnp.int32)
    xs = h2.reshape(n_tok, d)[row_tok]
    ys = expert_ffn(xs, block_e, n_used, wg, wu, wd, layer)
    y0 = ys[dest[:, 0].reshape(-1)].reshape(n_batch, lc, d)
    y1 = ys[dest[:, 1].reshape(-1)].reshape(n_batch, lc, d)
    g0 = gates[:, 0].reshape(n_batch, lc, 1)
    g1 = gates[:, 1].reshape(n_batch, lc, 1)
    return moe_combine(stream, y0, y1, g0, g1, mod, seq, n_rows)


def kernel(x, c, ctx, c_ctx, norm1, norm2, w_ada, b_ada, w_in, w_out, rwkv_conv, rwkv_w0, rwkv_w_up, rwkv_a0,
           rwkv_a_up, rwkv_g_up, rwkv_k_k, rwkv_k_a, rwkv_r_k, rwkv_ln_w, rwkv_ln_b, rwkv_v0, rwkv_v_down,
           rwkv_v_up, fourier_w, fourier_b, gmlp_norm, gmlp_ws, gmlp_bs, attn_q_norm, attn_k_norm, attn_sink,
           w_router, router_bias, w_e_gate, w_e_up, w_e_down):
    n_batch, seq, d = x.shape
    depth = w_in.shape[0]
    stream = jnp.concatenate([x, ctx], axis=1)
    c_all = jnp.zeros((SUBLANES, d), F32).at[:n_batch].set(c).at[n_batch].set(c_ctx)
    col_splits = np.cumsum([A_COLS, B_COLS, C_COLS])
    v_first = None
    for l in range(depth):
        mod = ada_modulation(c_all, w_ada, b_ada, l).reshape(SUBLANES, 1, 6 * d)
        w_parts = [w.astype(BF16) for w in jnp.split(w_in[l], col_splits, axis=1)]
        pa, pb, pc, pd = in_projection(stream, mod, norm1[l], w_parts, seq)

        vres = None if l == 0 else (rwkv_v0[l - 1], rwkv_v_down[l - 1], rwkv_v_up[l - 1])
        r, kk, v, k0, wd0, kka0, k1, wd1, kka1, g = rwkv_prepare(
            pa, seq, rwkv_conv[l], rwkv_w0[l], rwkv_w_up[l], rwkv_a0[l], rwkv_a_up[l], rwkv_g_up[l],
            rwkv_k_k[l], rwkv_k_a[l], v_first, vres)
        if l == 0:
            v_first = v
        yf, yb = rwkv_scan(r, kk, v, ((k0, wd0, kka0), (k1, wd1, kka1)), seq)
        y_a = rwkv_readout(yf, yb, r, k0, k1, v, g, rwkv_r_k[l].reshape(-1), rwkv_ln_w[l], rwkv_ln_b[l])

        y_b = fourier_mix(pb, seq, fourier_w[l], fourier_b[l])
        y_c = gmlp_mix(pc, gmlp_norm[l], gmlp_ws[l], gmlp_bs[l])
        q, kv, vv = attention_prepare(pd, seq, attn_q_norm[l], attn_k_norm[l])
        y_d = window_attention(q, kv, vv, attn_sink[l], seq)

        stream, h2, eidx, gates = out_projection_router(
            (y_a, y_b, y_c, y_d), stream, mod, norm2[l], w_out[l].astype(BF16), w_router, router_bias, seq)
        n_rows = seq if l == depth - 1 else stream.shape[1]
        stream = moe_ffn(stream, h2, eidx, gates, mod, w_e_gate, w_e_up, w_e_down, l, seq, n_rows)
    return stream
```

```python
import functools

import numpy as np
import jax
import jax.numpy as jnp
from jax import lax
from jax.experimental import pallas as pl
from jax.experimental.pallas import tpu as pltpu

F32 = jnp.float32
BF16 = jnp.bfloat16

D_MODEL = 2048
DEPTH = 2
GRID_W = 64
EPS = 1e-6
NEG_INF = -1e30
GROUP_W = 512
RWKV_HEAD = 64
RWKV_HEADS = 8
DECAY_LORA = 64
ICLR_LORA = 64
GATE_LORA = 128
VRES_LORA = 32
RWKV_GN_EPS = 64e-5
FOURIER_GROUPS = 4
FOURIER_CH = 128
GMLP_GROUPS = 4
GMLP_CH = 128
GMLP_CHUNK = 128
ATT_HEAD = 64
ATT_Q_HEADS = 8
ATT_KV_HEADS = 2
WINDOW = 128
ATT_BLOCK = 128
ROPE_BASE = 10000.0
N_EXPERTS = 64
N_EXPERT_GROUPS = 8
EXPERTS_PER_GROUP = 8
TOP_K = 2
D_EXPERT = 512
MOE_BLOCK = 128
A_COLS = 1920
B_COLS = 512
C_COLS = 1024
D_COLS = 768

LANES = 128
SUBLANES = 8
TM = 256
SCAN_TB = 128
SCAN_GROUP = SUBLANES
OUTPROJ_SPLIT = 2
VMEM_LIMIT = 56 * 1024 * 1024


def _cparams(sem, vmem=VMEM_LIMIT):
    return pltpu.CompilerParams(dimension_semantics=sem, vmem_limit_bytes=vmem)


def _dot(a, b):
    return jnp.dot(a, b, preferred_element_type=F32)


def _dot_nt(a, b):
    return lax.dot_general(a, b, (((1,), (1,)), ((), ())), preferred_element_type=F32)


def _dot_tn(a, b):
    return lax.dot_general(a, b, (((0,), (0,)), ((), ())), preferred_element_type=F32)


def _split(x):
    hi = x.astype(BF16)
    lo = (x - hi.astype(F32)).astype(BF16)
    return hi, lo


def _dot1(a, b):
    return _dot(a.astype(BF16), b.astype(BF16))


def _segsum(x, seg):
    xh, xl = _split(x)
    return _dot(xh, seg) + _dot(xl, seg)


def _sigmoid(x):
    return 1.0 / (1.0 + jnp.exp(-x))


def _seg_matrix(width, seg):
    i = np.arange(width) // seg
    return jnp.asarray((i[:, None] == i[None, :]).astype(np.float32), dtype=BF16)


def _ada_kernel(c_ref, w_ref, b_ref, o_ref):
    c = c_ref[...]
    a = (c * _sigmoid(c)).astype(BF16)
    o_ref[...] = _dot(a, w_ref[...].astype(BF16)) + b_ref[...]


def ada_modulation(c_all, w_ada, b_ada, layer, tn=1024):
    rows, d = c_all.shape
    n_layers, _, n = w_ada.shape
    return pl.pallas_call(
        _ada_kernel, name="ada_mod",
        out_shape=jax.ShapeDtypeStruct((rows, n), F32),
        grid=(n // tn,),
        in_specs=[pl.BlockSpec((rows, d), lambda j: (0, 0)),
                  pl.BlockSpec((None, d, tn), lambda j: (layer, 0, j)),
                  pl.BlockSpec((None, 1, tn), lambda j: (layer, 0, j))],
        out_specs=pl.BlockSpec((rows, tn), lambda j: (0, j)),
        compiler_params=_cparams(("arbitrary",)),
    )(c_all, w_ada, b_ada.reshape(n_layers, 1, n))


def _inproj_kernel(x_ref, mod_ref, g_ref, wa_ref, wb_ref, wc_ref, wd_ref, pa_ref, pb_ref, pc_ref, pd_ref):
    d = x_ref.shape[-1]
    x = x_ref[...]
    y = x * lax.rsqrt(jnp.mean(x * x, axis=-1, keepdims=True) + EPS) * g_ref[...]
    sh = mod_ref[:, 0:d]
    sc = mod_ref[:, d:2 * d]
    h = (y * (1.0 + sc) + sh).astype(BF16)
    pa_ref[...] = _dot(h, wa_ref[...])
    pb_ref[...] = _dot(h, wb_ref[...])
    pc_ref[...] = _dot(h, wc_ref[...])
    pd_ref[...] = _dot(h, wd_ref[...])


def _mod_index(n_batch, ctx_tile):
    return lambda b, i: (jnp.where(i >= ctx_tile, n_batch, b), 0, 0)


def in_projection(stream, mod, gain, w_parts, seq):
    n_batch, lc, d = stream.shape
    ctx_tile = seq // TM
    resident = lambda w: pl.BlockSpec(w.shape, lambda b, i: (0, 0), pipeline_mode=pl.Buffered(1))
    row = lambda n: pl.BlockSpec((None, TM, n), lambda b, i: (b, i, 0))
    return pl.pallas_call(
        _inproj_kernel, name="in_proj",
        out_shape=[jax.ShapeDtypeStruct((n_batch, lc, w.shape[1]), F32) for w in w_parts],
        grid=(n_batch, lc // TM),
        in_specs=[row(d),
                  pl.BlockSpec((None, 1, mod.shape[-1]), _mod_index(n_batch, ctx_tile)),
                  pl.BlockSpec((1, d), lambda b, i: (0, 0))] + [resident(w) for w in w_parts],
        out_specs=[row(w.shape[1]) for w in w_parts],
        compiler_params=_cparams(("parallel", "arbitrary")),
    )(stream, mod, gain.reshape(1, d), *w_parts)


def _rwkv_prep_kernel(has_vres, ctx_tile, *refs):
    (pa_ref, hp_ref, hn_ref, conv_ref, w0_ref, wup_ref, a0_ref, aup_ref, gup_ref, kk_par_ref, ka_par_ref,
     seg_ref) = refs[:12]
    rest = refs[12:]
    if has_vres:
        vf_ref, v0_ref, vdn_ref, vup_ref = rest[:4]
        rest = rest[4:]
    r_o, kk_o, v_o, k0_o, w0_o, kka0_o, k1_o, w1_o, kka1_o, g_o = rest
    gw = GROUP_W
    x = pa_ref[...]
    tm = x.shape[0]
    i = pl.program_id(1)
    first = (i == 0) | (i == ctx_tile)
    last = (i == ctx_tile - 1) | (i == pl.num_programs(1) - 1)
    halo_prev = jnp.where(first, 0.0, hp_ref[SUBLANES - 1:SUBLANES, :])
    halo_next = jnp.where(last, 0.0, hn_ref[0:1, :])
    row = lax.broadcasted_iota(jnp.int32, x.shape, 0)
    x_prev = jnp.where(row == 0, halo_prev, pltpu.roll(x, 1, 0))
    x_next = jnp.where(row == tm - 1, halo_next, pltpu.roll(x, tm - 1, 0))
    y = x_prev * conv_ref[0:1, :] + x * conv_ref[1:2, :] + x_next * conv_ref[2:3, :]
    r = y[:, 0:gw]
    k = y[:, gw:2 * gw]
    v = y[:, 2 * gw:3 * gw]
    wl = y[:, 3 * gw:3 * gw + 2 * DECAY_LORA]
    al = y[:, 3 * gw + 2 * DECAY_LORA:3 * gw + 2 * DECAY_LORA + 2 * ICLR_LORA]
    gl = y[:, 3 * gw + 2 * DECAY_LORA + 2 * ICLR_LORA:]
    if has_vres:
        mix = _sigmoid(v0_ref[...] + _dot1(_dot1(v, vdn_ref[...]), vup_ref[...]))
        v = v + (vf_ref[...] - v) * mix
    kk = k * kk_par_ref[...]
    nrm = jnp.maximum(jnp.sqrt(_segsum(kk * kk, seg_ref[...])), 1e-12)
    kk = kk / nrm
    r_o[...] = r
    kk_o[...] = kk
    v_o[...] = v
    tanh_wl = jnp.tanh(wl)
    for d, (k_o, w_o, kka_o) in enumerate(((k0_o, w0_o, kka0_o), (k1_o, w1_o, kka1_o))):
        w_raw = w0_ref[d:d + 1, :] + _dot1(tanh_wl, wup_ref[d])
        z = -w_raw
        softplus = jnp.maximum(z, 0.0) + jnp.log(1.0 + jnp.exp(-jnp.abs(z)))
        w_o[...] = jnp.exp(-jnp.exp(-softplus - 0.5))
        a = _sigmoid(a0_ref[d:d + 1, :] + _dot1(al, aup_ref[d]))
        k_o[...] = k * (1.0 + (a - 1.0) * ka_par_ref[...])
        kka_o[...] = kk * a
    g_o[...] = _dot1(_sigmoid(gl), gup_ref[...])


def rwkv_prepare(pa, seq, conv_w, w0, w_up, a0, a_up, g_up, k_k, k_a, v_first=None, vres=None):
    n_batch, lc, c = pa.shape
    gw = GROUP_W
    sub_per_tile = TM // SUBLANES
    n_sub = lc // SUBLANES
    pad_rows = lambda w, d, n: jnp.zeros((2 * n, gw), F32).at[d * n:(d + 1) * n].set(w)
    wup = jnp.stack([pad_rows(w_up[d], d, DECAY_LORA) for d in range(2)])
    aup = jnp.stack([pad_rows(a_up[d], d, ICLR_LORA) for d in range(2)])
    seg = _seg_matrix(gw, RWKV_HEAD)
    const = lambda a: pl.BlockSpec(a.shape, lambda b, i: (0,) * a.ndim)
    row = lambda n: pl.BlockSpec((None, TM, n), lambda b, i: (b, i, 0))
    halo_prev = pl.BlockSpec((None, SUBLANES, c), lambda b, i: (b, jnp.maximum(i * sub_per_tile - 1, 0), 0))
    halo_next = pl.BlockSpec((None, SUBLANES, c), lambda b, i: (b, jnp.minimum((i + 1) * sub_per_tile, n_sub - 1), 0))
    args = [pa, pa, pa, conv_w, w0, wup, a0, aup, g_up, k_k.reshape(1, gw), k_a.reshape(1, gw), seg]
    specs = [row(c), halo_prev, halo_next] + [const(a) for a in args[3:]]
    has_vres = vres is not None
    if has_vres:
        v0, v_down, v_up = vres
        vdn = jnp.zeros((gw, LANES), F32).at[:, :VRES_LORA].set(v_down)
        vup = jnp.zeros((LANES, gw), F32).at[:VRES_LORA].set(v_up)
        extra = [v_first, v0.reshape(1, gw), vdn, vup]
        args += extra
        specs += [row(gw)] + [const(a) for a in extra[1:]]
    return pl.pallas_call(
        functools.partial(_rwkv_prep_kernel, has_vres, seq // TM), name="rwkv_prep",
        out_shape=[jax.ShapeDtypeStruct((n_batch, lc, gw), F32)] * 10,
        grid=(n_batch, lc // TM),
        in_specs=specs,
        out_specs=[row(gw)] * 10,
        compiler_params=_cparams(("parallel", "arbitrary")),
    )(*args)


def _scan_kernel(n_batch, rf, kkf, vf, kf, wf, kaf, rb, kkb, vb, kb, wb, kab, bsel_ref, segbd_ref,
                 yf_ref, yb_ref, s_ref, vk_ref, yt_ref):
    tb = rf.shape[1]
    n_pairs = GROUP_W // LANES
    half = RWKV_HEAD

    @pl.when(pl.program_id(0) == 0)
    def _():
        s_ref[...] = jnp.zeros_like(s_ref)
        yt_ref[...] = jnp.zeros_like(yt_ref)

    segbd = segbd_ref[...]
    dirs = ((rf, kkf, vf, kf, wf, kaf, yf_ref), (rb, kkb, vb, kb, wb, kab, yb_ref))
    chains = [(d, b, p) for d in range(2) for b in range(n_batch) for p in range(n_pairs)]
    n_ch = len(chains)
    lane = lax.broadcasted_iota(jnp.int32, (RWKV_HEAD, LANES), 1) % half

    n_sets = 2
    per_set = n_ch // n_sets
    sets = [list(range(si * per_set, (si + 1) * per_set)) for si in range(n_sets)]

    def side_by_side(parts):
        return jnp.concatenate([jnp.concatenate(parts[i:i + 2], axis=1) for i in range(0, per_set, 2)], axis=0)

    def piece(full, n):
        return full[(n // 2) * RWKV_HEAD:(n // 2 + 1) * RWKV_HEAD, (n % 2) * LANES:(n % 2 + 1) * LANES]

    def group(g, carry):
        bases = (pl.multiple_of(g * SCAN_GROUP, SCAN_GROUP),
                 pl.multiple_of(tb - SCAN_GROUP - g * SCAN_GROUP, SCAN_GROUP))

        def row(which, ci, j):
            d, b, p = chains[ci]
            t = j if d == 0 else SCAN_GROUP - 1 - j
            return dirs[d][which][b, pl.ds(bases[d], SCAN_GROUP), p * LANES:(p + 1) * LANES][t:t + 1]

        def sk_dot(states, cs, j):
            return _dot(side_by_side([(states[n] * row(1, ci, j)).astype(BF16) for n, ci in enumerate(cs)]), segbd)

        sk = []
        for cs in sets:
            for ci in cs:
                d, b, p = chains[ci]
                v8 = dirs[d][2][b, pl.ds(bases[d], SCAN_GROUP), p * LANES:(p + 1) * LANES]
                k8 = dirs[d][3][b, pl.ds(bases[d], SCAN_GROUP), p * LANES:(p + 1) * LANES]
                vt = jnp.concatenate([v8[:, :RWKV_HEAD], v8[:, RWKV_HEAD:]], axis=0).T.astype(BF16)
                kmat = jnp.concatenate([bsel_ref[t] * k8[t:t + 1] for t in range(SCAN_GROUP)],
                                       axis=1).astype(BF16)
                vk_ref[ci] = _dot(vt, kmat)
            sk.append(sk_dot([s_ref[ci] for ci in cs], cs, 0))
        for j in range(SCAN_GROUP):
            for si, cs in enumerate(sets):
                states = []
                for n, ci in enumerate(cs):
                    d = chains[ci][0]
                    t = j if d == 0 else SCAN_GROUP - 1 - j
                    s = (s_ref[ci] * row(4, ci, j) - piece(sk[si], n) * row(5, ci, j)
                         + vk_ref[ci, :, t * LANES:(t + 1) * LANES])
                    s_ref[ci] = s
                    states.append(s)
                y_all = _dot(side_by_side([(s * row(0, ci, j)).astype(BF16) for s, ci in zip(states, cs)]), segbd)
                if j + 1 < SCAN_GROUP:
                    sk[si] = sk_dot(states, cs, j + 1)
                for n, ci in enumerate(cs):
                    d = chains[ci][0]
                    t = j if d == 0 else SCAN_GROUP - 1 - j
                    yt_ref[ci] = jnp.where(lane == (bases[d] + t) % half, piece(y_all, n), yt_ref[ci])
        return carry

    def flush(first_row):
        for ci, (d, b, p) in enumerate(chains):
            t = yt_ref[ci].T
            rows = slice(first_row[d], first_row[d] + half)
            dirs[d][6][b, rows, p * LANES:(p + 1) * LANES] = jnp.concatenate([t[:half], t[half:]], axis=1)

    groups_per_flush = half // SCAN_GROUP
    for part in range(tb // half):
        lax.fori_loop(part * groups_per_flush, (part + 1) * groups_per_flush, group, 0)
        flush((part * half, tb - (part + 1) * half))


def _scan_selectors():
    bsel = np.zeros((SCAN_GROUP, 2 * SCAN_GROUP, LANES), np.float32)
    for h in range(2):
        for j in range(SCAN_GROUP):
            bsel[j, h * SCAN_GROUP + j, h * RWKV_HEAD:(h + 1) * RWKV_HEAD] = 1.0
    return jnp.asarray(bsel, F32), _seg_matrix(2 * LANES, RWKV_HEAD)


def rwkv_scan(r, kk, v, dirs, seq):
    n_batch, lc, gw = r.shape
    tb = SCAN_TB
    nx = seq // tb
    nc = (lc - seq) // tb
    n_chains = 2 * n_batch * (gw // LANES)
    fwd = lambda i: (0, jnp.where(i < nc, nx + i, i - nc), 0)
    bwd = lambda i: (0, nx + nc - 1 - i, 0)
    blk = lambda m: pl.BlockSpec((n_batch, tb, gw), m)
    const = lambda a: pl.BlockSpec(a.shape, lambda i: (0,) * a.ndim)
    bsel, segbd = _scan_selectors()
    return pl.pallas_call(
        functools.partial(_scan_kernel, n_batch), name="rwkv_scan",
        out_shape=[jax.ShapeDtypeStruct((n_batch, lc, gw), F32)] * 2,
        grid=(nx + nc,),
        in_specs=[blk(fwd)] * 6 + [blk(bwd)] * 6 + [const(bsel), const(segbd)],
        out_specs=[blk(fwd), blk(bwd)],
        scratch_shapes=[pltpu.VMEM((n_chains, RWKV_HEAD, LANES), F32),
                        pltpu.VMEM((n_chains, RWKV_HEAD, SCAN_GROUP * LANES), F32),
                        pltpu.VMEM((n_chains, RWKV_HEAD, LANES), F32)],
        compiler_params=_cparams(("arbitrary",)),
    )(r, kk, v, *dirs[0], r, kk, v, *dirs[1], bsel, segbd)


def _rwkv_readout_kernel(yf_ref, yb_ref, r_ref, k0_ref, k1_ref, v_ref, g_ref, rk_ref, lnw_ref, lnb_ref, seg_ref,
                         o_ref):
    seg = seg_ref[...]
    inv = 1.0 / RWKV_HEAD
    y = yf_ref[...] + yb_ref[...]
    mu = _segsum(y, seg) * inv
    dlt = y - mu
    var = _segsum(dlt * dlt, seg) * inv
    yn = dlt * lax.rsqrt(var + RWKV_GN_EPS) * lnw_ref[...] + lnb_ref[...]
    kbar = 0.5 * (k0_ref[...] + k1_ref[...])
    bonus = _segsum(r_ref[...] * kbar * rk_ref[...], seg) * v_ref[...]
    o_ref[...] = ((yn + bonus) * g_ref[...]).astype(o_ref.dtype)


def rwkv_readout(yf, yb, r, k0, k1, v, g, r_k, ln_w, ln_b):
    n_batch, lc, gw = r.shape
    seg = _seg_matrix(gw, RWKV_HEAD)
    row = pl.BlockSpec((None, TM, gw), lambda b, i: (b, i, 0))
    const = lambda a: pl.BlockSpec(a.shape, lambda b, i: (0,) * a.ndim)
    params = [r_k.reshape(1, gw), ln_w.reshape(1, gw), ln_b.reshape(1, gw), seg]
    return pl.pallas_call(
        _rwkv_readout_kernel, name="rwkv_readout",
        out_shape=jax.ShapeDtypeStruct((n_batch, lc, gw), BF16),
        grid=(n_batch, lc // TM),
        in_specs=[row] * 7 + [const(a) for a in params],
        out_specs=row,
        compiler_params=_cparams(("parallel", "arbitrary")),
    )(yf, yb, r, k0, k1, v, g, *params)


def _dft_tables(n):
    k = np.arange(n)
    ang = 2.0 * np.pi * ((k[:, None] * k[None, :]) % n) / n
    return np.cos(ang), np.sin(ang)


def _fft_a_kernel(nb, f_ref, gr_ref, gi_ref, zr_ref, zi_ref):
    n1 = gr_ref.shape[1]
    n2_total = f_ref.shape[0] // n1
    j = pl.program_id(2)
    for i in range(nb):
        rows = f_ref[pl.ds(j * nb + i, n1, stride=n2_total), :].astype(BF16)
        zr_ref[i * n1:(i + 1) * n1, :] = _dot(gr_ref[i], rows)
        zi_ref[i * n1:(i + 1) * n1, :] = _dot(gi_ref[i], rows)


def _fft_b_kernel(kb, scale, zr_ref, zi_ref, fc_ref, fs_ref, cc_ref, cs_ref, w_ref, b_ref, o_ref):
    n2 = fc_ref.shape[0]
    n1 = zr_ref.shape[0] // n2
    j = pl.program_id(2)
    fc, fs = fc_ref[...], fs_ref[...]
    for i in range(kb):
        k1 = j * kb + i
        zr = zr_ref[pl.ds(k1, n2, stride=n1), :].astype(BF16)
        zi = zi_ref[pl.ds(k1, n2, stride=n1), :].astype(BF16)
        xr = _dot(fc, zr) + _dot(fs, zi)
        xi = _dot(fc, zi) - _dot(fs, zr)
        re = (_dot(xr.astype(BF16), cc_ref[...]) + _dot(xi.astype(BF16), cs_ref[...])) * scale
        o_ref[pl.ds(k1, n2, stride=n1), :] = _dot(re.astype(BF16), w_ref[...].astype(BF16)) + b_ref[...]


def _dft_dense_kernel(scale, f_ref, fc_ref, fs_ref, cc_ref, cs_ref, w_ref, b_ref, o_ref):
    f = f_ref[...].astype(BF16)
    xr = _dot(fc_ref[...], f)
    xi = -_dot(fs_ref[...], f)
    re = (_dot(xr.astype(BF16), cc_ref[...]) + _dot(xi.astype(BF16), cs_ref[...])) * scale
    o_ref[...] = _dot(re.astype(BF16), w_ref[...].astype(BF16)) + b_ref[...]


def fourier_mix(pb, seq, w, bias):
    n_batch, lc, gw = pb.shape
    ctx = lc - seq
    ch = FOURIER_CH
    groups = FOURIER_GROUPS
    n2 = LANES
    n1 = seq // n2
    cc, cs = (jnp.asarray(t, BF16) for t in _dft_tables(ch))
    bias3 = bias.reshape(groups, 1, ch)

    k1 = np.arange(n1)[None, :, None]
    m1 = np.arange(n1)[None, None, :]
    m2 = np.arange(n2)[:, None, None]
    ang = 2.0 * np.pi * ((k1 * (n2 * m1 + m2)) % seq) / seq
    g_re = jnp.asarray(np.cos(ang), BF16)
    g_im = jnp.asarray(-np.sin(ang), BF16)
    nb = min(16, n2)
    slab = pl.BlockSpec((None, seq, ch), lambda b, g, j: (b, 0, g))
    gspec = pl.BlockSpec((nb, n1, n1), lambda b, g, j: (j, 0, 0))
    zspec = pl.BlockSpec((None, None, nb * n1, ch), lambda b, g, j: (b, g, j, 0))
    zr, zi = pl.pallas_call(
        functools.partial(_fft_a_kernel, nb), name="fft_stage_a",
        out_shape=[jax.ShapeDtypeStruct((n_batch, groups, seq, ch), F32)] * 2,
        grid=(n_batch, groups, n2 // nb),
        in_specs=[slab, gspec, gspec],
        out_specs=[zspec, zspec],
        compiler_params=_cparams(("parallel", "parallel", "arbitrary")),
    )(pb, g_re, g_im)

    fc, fs = (jnp.asarray(t, BF16) for t in _dft_tables(n2))
    kb = min(16, n1)
    scale = 1.0 / np.sqrt(float(seq) * ch)
    zslab = pl.BlockSpec((None, None, seq, ch), lambda b, g, j: (b, g, 0, 0), pipeline_mode=pl.Buffered(1))
    const = lambda a: pl.BlockSpec(a.shape, lambda b, g, j: (0,) * a.ndim)
    wspec = pl.BlockSpec((None, ch, ch), lambda b, g, j: (g, 0, 0))
    bspec = pl.BlockSpec((None, 1, ch), lambda b, g, j: (g, 0, 0))
    y_x = pl.pallas_call(
        functools.partial(_fft_b_kernel, kb, scale), name="fft_stage_b",
        out_shape=jax.ShapeDtypeStruct((n_batch, seq, gw), F32),
        grid=(n_batch, groups, n1 // kb),
        in_specs=[zslab, zslab, const(fc), const(fs), const(cc), const(cs), wspec, bspec],
        out_specs=pl.BlockSpec((None, seq, ch), lambda b, g, j: (b, 0, g)),
        compiler_params=_cparams(("parallel", "parallel", "arbitrary")),
    )(zr, zi, fc, fs, cc, cs, w, bias3)

    fcc, fsc = (jnp.asarray(t, BF16) for t in _dft_tables(ctx))
    const2 = lambda a: pl.BlockSpec(a.shape, lambda b, g: (0,) * a.ndim)
    ctx_tile = seq // ctx
    y_c = pl.pallas_call(
        functools.partial(_dft_dense_kernel, 1.0 / np.sqrt(float(ctx) * ch)), name="dft_ctx",
        out_shape=jax.ShapeDtypeStruct((n_batch, ctx, gw), F32),
        grid=(n_batch, groups),
        in_specs=[pl.BlockSpec((None, ctx, ch), lambda b, g: (b, ctx_tile, g)),
                  const2(fcc), const2(fsc), const2(cc), const2(cs),
                  pl.BlockSpec((None, ch, ch), lambda b, g: (g, 0, 0)),
                  pl.BlockSpec((None, 1, ch), lambda b, g: (g, 0, 0))],
        out_specs=pl.BlockSpec((None, ctx, ch), lambda b, g: (b, 0, g)),
        compiler_params=_cparams(("parallel", "arbitrary")),
    )(pb, fcc, fsc, cc, cs, w, bias3)
    return jnp.concatenate([y_x, y_c], axis=1)


def _gmlp_kernel(pc_ref, norm_ref, ws_ref, bs_ref, o_ref):
    gw = GROUP_W
    x = pc_ref[...]
    z = 0.5 * x * (1.0 + jnp.tanh(0.7978845608028654 * (x + 0.044715 * (x * x * x))))
    tm = x.shape[0]
    for g in range(GMLP_GROUPS):
        u = z[:, g * GMLP_CH:(g + 1) * GMLP_CH]
        v = z[:, gw + g * GMLP_CH:gw + (g + 1) * GMLP_CH]
        v = v * lax.rsqrt(jnp.mean(v * v, axis=-1, keepdims=True) + EPS) * norm_ref[g:g + 1, :]
        v = v.astype(BF16)
        ws = ws_ref[g].astype(BF16)
        for c in range(tm // GMLP_CHUNK):
            rows = slice(c * GMLP_CHUNK, (c + 1) * GMLP_CHUNK)
            f = _dot(ws, v[rows]) + bs_ref[g]
            o_ref[rows, g * GMLP_CH:(g + 1) * GMLP_CH] = (u[rows] * f).astype(o_ref.dtype)


def gmlp_mix(pc, norm_g, ws, bs):
    n_batch, lc, c = pc.shape
    bs_b = jnp.broadcast_to(bs[:, :, None], bs.shape + (GMLP_CH,))
    const = lambda a: pl.BlockSpec(a.shape, lambda b, i: (0,) * a.ndim)
    return pl.pallas_call(
        _gmlp_kernel, name="gmlp",
        out_shape=jax.ShapeDtypeStruct((n_batch, lc, GROUP_W), BF16),
        grid=(n_batch, lc // TM),
        in_specs=[pl.BlockSpec((None, TM, c), lambda b, i: (b, i, 0)), const(norm_g), const(ws), const(bs_b)],
        out_specs=pl.BlockSpec((None, TM, GROUP_W), lambda b, i: (b, i, 0)),
        compiler_params=_cparams(("parallel", "arbitrary")),
    )(pc, norm_g, ws, bs_b)


def _rope_tables(seq, ctx):
    half = ATT_HEAD // 4
    inv_freq = ROPE_BASE ** (-np.arange(half, dtype=np.float64) / half)
    pos = np.arange(seq)
    ang_r = (pos // GRID_W)[:, None] * inv_freq[None, :]
    ang_c = (pos % GRID_W)[:, None] * inv_freq[None, :]
    cos = np.concatenate([np.cos(ang_r)] * 2 + [np.cos(ang_c)] * 2, axis=1)
    sin = np.concatenate([-np.sin(ang_r), np.sin(ang_r), -np.sin(ang_c), np.sin(ang_c)], axis=1)
    cos = np.concatenate([cos, np.ones((ctx, ATT_HEAD))], axis=0)
    sin = np.concatenate([sin, np.zeros((ctx, ATT_HEAD))], axis=0)
    return jnp.asarray(np.tile(cos, (1, 2)), F32), jnp.asarray(np.tile(sin, (1, 2)), F32)


def _rope(t, cos, sin):
    n = t.shape[1]
    q = ATT_HEAD // 4
    lane = lax.broadcasted_iota(jnp.int32, t.shape, 1)
    swapped = jnp.where((lane % (2 * q)) < q, pltpu.roll(t, n - q, 1), pltpu.roll(t, q, 1))
    return t * cos + swapped * sin


def _attn_prep_kernel(pd_ref, cos_ref, sin_ref, qg_ref, kg_ref, seg_ref, q_o, k_o, v_o):
    gw = GROUP_W
    kvw = ATT_KV_HEADS * ATT_HEAD
    seg = seg_ref[...]
    inv = 1.0 / ATT_HEAD
    cos, sin = cos_ref[...], sin_ref[...]
    q = pd_ref[:, 0:gw]
    q = q * lax.rsqrt(_segsum(q * q, seg) * inv + EPS) * qg_ref[...]
    q = _rope(q, jnp.concatenate([cos] * (gw // kvw), axis=1), jnp.concatenate([sin] * (gw // kvw), axis=1))
    q_o[...] = (q * (ATT_HEAD ** -0.5)).astype(q_o.dtype)
    k = pd_ref[:, gw:gw + kvw]
    k = k * lax.rsqrt(_segsum(k * k, seg[:kvw, :kvw]) * inv + EPS) * kg_ref[...]
    k = _rope(k, cos, sin)
    v = pd_ref[:, gw + kvw:gw + 2 * kvw]
    lane = lax.broadcasted_iota(jnp.int32, k.shape, 1)

    def variants(t):
        h0 = jnp.where(lane < ATT_HEAD, t, 0.0)
        h1 = jnp.where(lane >= ATT_HEAD, t, 0.0)
        return jnp.concatenate([h0, pltpu.roll(h0, ATT_HEAD, 1), pltpu.roll(h1, ATT_HEAD, 1), h1], axis=1)

    k_o[...] = variants(k).astype(k_o.dtype)
    v_o[...] = variants(v).astype(v_o.dtype)


def attention_prepare(pd, seq, q_gain, k_gain):
    n_batch, lc, c = pd.shape
    gw = GROUP_W
    kvw = ATT_KV_HEADS * ATT_HEAD
    cos, sin = _rope_tables(seq, lc - seq)
    seg = _seg_matrix(gw, ATT_HEAD)
    qg = jnp.tile(q_gain, ATT_Q_HEADS).reshape(1, gw)
    kg = jnp.tile(k_gain, ATT_KV_HEADS).reshape(1, kvw)
    const = lambda a: pl.BlockSpec(a.shape, lambda b, i: (0,) * a.ndim)
    tab = pl.BlockSpec((TM, kvw), lambda b, i: (i, 0))
    row = lambda n: pl.BlockSpec((None, TM, n), lambda b, i: (b, i, 0))
    return pl.pallas_call(
        _attn_prep_kernel, name="attn_prep",
        out_shape=[jax.ShapeDtypeStruct((n_batch, lc, gw), BF16)] * 3,
        grid=(n_batch, lc // TM),
        in_specs=[row(c), tab, tab, const(qg), const(kg), const(seg)],
        out_specs=[row(gw)] * 3,
        compiler_params=_cparams(("parallel", "arbitrary")),
    )(pd, cos, sin, qg, kg, seg)


def _attn_kernel(nb, sink_ref, q_ref, kp_ref, kc_ref, kn_ref, kx_ref, vp_ref, vc_ref, vn_ref, vx_ref, o_ref):
    i = pl.program_id(1)
    blk = q_ref.shape[0]
    is_lat = i < nb
    rowi = lax.broadcasted_iota(jnp.int32, (blk, blk), 0)
    coli = lax.broadcasted_iota(jnp.int32, (blk, blk), 1)
    m_prev = (coli >= rowi) & is_lat & (i >= 1)
    m_cur = jnp.broadcast_to(is_lat, (blk, blk))
    m_next = (coli <= rowi) & (i < nb - 1)
    for p in range(ATT_Q_HEADS // 2):
        q = q_ref[:, p * LANES:(p + 1) * LANES]
        acc = None
        for par in range(2):
            h = 2 * p + par
            g = h // (ATT_Q_HEADS // ATT_KV_HEADS)
            col = slice((2 * g + par) * LANES, (2 * g + par + 1) * LANES)
            sink = sink_ref[h]
            s1 = jnp.where(m_prev, _dot_nt(q, kp_ref[:, col]), NEG_INF)
            s2 = jnp.where(m_cur, _dot_nt(q, kc_ref[:, col]), NEG_INF)
            s3 = jnp.where(m_next, _dot_nt(q, kn_ref[:, col]), NEG_INF)
            sx = _dot_nt(q, kx_ref[:, col])
            m = jnp.maximum(jnp.maximum(jnp.max(jnp.maximum(jnp.maximum(s1, s2), s3), axis=-1, keepdims=True),
                                        jnp.max(sx, axis=-1, keepdims=True)), sink)
            p1, p2, p3, px = jnp.exp(s1 - m), jnp.exp(s2 - m), jnp.exp(s3 - m), jnp.exp(sx - m)
            den = (jnp.sum(p1 + p2 + p3, axis=-1, keepdims=True) + jnp.sum(px, axis=-1, keepdims=True)
                   + jnp.exp(sink - m))
            o = (_dot(p1.astype(BF16), vp_ref[:, col]) + _dot(p2.astype(BF16), vc_ref[:, col])
                 + _dot(p3.astype(BF16), vn_ref[:, col]) + _dot(px.astype(BF16), vx_ref[:, col])) / den
            acc = o if acc is None else acc + o
        o_ref[:, p * LANES:(p + 1) * LANES] = acc.astype(o_ref.dtype)


def window_attention(q, kv, vv, sink, seq):
    n_batch, lc, gw = q.shape
    blk = ATT_BLOCK
    nb = seq // blk
    ctx = lc - seq
    wide = kv.shape[-1]
    qspec = pl.BlockSpec((None, blk, gw), lambda b, i, s: (b, i, 0))
    prev = pl.BlockSpec((None, blk, wide), lambda b, i, s: (b, jnp.clip(i - 1, 0, nb - 1), 0))
    cur = pl.BlockSpec((None, blk, wide), lambda b, i, s: (b, jnp.minimum(i, nb - 1), 0))
    nxt = pl.BlockSpec((None, blk, wide), lambda b, i, s: (b, jnp.clip(i + 1, 0, nb - 1), 0))
    cx = pl.BlockSpec((None, ctx, wide), lambda b, i, s: (b, seq // ctx, 0))
    return pl.pallas_call(
        functools.partial(_attn_kernel, nb), name="window_attn",
        out_shape=jax.ShapeDtypeStruct((n_batch, lc, gw), BF16),
        grid_spec=pltpu.PrefetchScalarGridSpec(
            num_scalar_prefetch=1, grid=(n_batch, lc // blk),
            in_specs=[qspec, prev, cur, nxt, cx, prev, cur, nxt, cx],
            out_specs=qspec),
        compiler_params=_cparams(("parallel", "arbitrary")),
    )(sink.astype(F32), q, kv, kv, kv, kv, vv, vv, vv, vv)


def _top2_rows(val, iota):
    m1 = jnp.max(val, axis=0, keepdims=True)
    i1 = jnp.min(jnp.where(val == m1, iota, SUBLANES), axis=0, keepdims=True)
    rest = jnp.where(iota == i1, -jnp.inf, val)
    m2 = jnp.max(rest, axis=0, keepdims=True)
    i2 = jnp.min(jnp.where(rest == m2, iota, SUBLANES), axis=0, keepdims=True)
    return m1, m2, i1, i2


def _outproj_kernel(ya_ref, yb_ref, yc_ref, yd_ref, x_ref, mod_ref, g_ref, wo_ref, wr_ref, rb_ref,
                    xo_ref, h_ref, e_ref, gw_ref):
    tm = x_ref.shape[0] // OUTPROJ_SPLIT
    for part in range(OUTPROJ_SPLIT):
        _outproj_rows(slice(part * tm, (part + 1) * tm), ya_ref, yb_ref, yc_ref, yd_ref, x_ref, mod_ref, g_ref,
                      wo_ref, wr_ref, rb_ref, xo_ref, h_ref, e_ref, gw_ref)


def _outproj_rows(rows, ya_ref, yb_ref, yc_ref, yd_ref, x_ref, mod_ref, g_ref, wo_ref, wr_ref, rb_ref,
                  xo_ref, h_ref, e_ref, gw_ref):
    d = x_ref.shape[-1]
    gw = GROUP_W
    acc = None
    for n, y_ref in enumerate((ya_ref, yb_ref, yc_ref, yd_ref)):
        part = _dot(y_ref[rows, :].astype(BF16), wo_ref[n * gw:(n + 1) * gw, :])
        acc = part if acc is None else acc + part
    x = x_ref[rows, :] + mod_ref[:, 2 * d:3 * d] * acc
    xo_ref[rows, :] = x
    y = x * lax.rsqrt(jnp.mean(x * x, axis=-1, keepdims=True) + EPS) * g_ref[...]
    h = y * (1.0 + mod_ref[:, 4 * d:5 * d]) + mod_ref[:, 3 * d:4 * d]
    h_ref[rows, :] = h.astype(h_ref.dtype)
    hh, hl = _split(h)
    wh, wl = _split(wr_ref[...])
    logits = _dot_nt(wh, hh) + _dot_nt(wh, hl) + _dot_nt(wl, hh)
    scores = _sigmoid(logits)
    sel = scores + rb_ref[:, rows]
    tm = x.shape[0]
    iota = lax.broadcasted_iota(jnp.int32, (EXPERTS_PER_GROUP, tm), 0)
    best_val, best = None, None
    for grp in range(N_EXPERT_GROUPS):
        m1, m2, _, _ = _top2_rows(sel[grp * EXPERTS_PER_GROUP:(grp + 1) * EXPERTS_PER_GROUP], iota)
        gs = m1 + m2
        if grp == 0:
            best_val, best = gs, jnp.zeros_like(gs, dtype=jnp.int32)
        else:
            upd = gs > best_val
            best_val = jnp.where(upd, gs, best_val)
            best = jnp.where(upd, grp, best)
    sel_in = jnp.zeros((EXPERTS_PER_GROUP, tm), F32)
    sc_in = jnp.zeros((EXPERTS_PER_GROUP, tm), F32)
    for grp in range(N_EXPERT_GROUPS):
        members = slice(grp * EXPERTS_PER_GROUP, (grp + 1) * EXPERTS_PER_GROUP)
        hit = best == grp
        sel_in = jnp.where(hit, sel[members], sel_in)
        sc_in = jnp.where(hit, scores[members], sc_in)
    _, _, i1, i2 = _top2_rows(sel_in, iota)
    w1 = jnp.sum(jnp.where(iota == i1, sc_in, 0.0), axis=0, keepdims=True)
    w2 = jnp.sum(jnp.where(iota == i2, sc_in, 0.0), axis=0, keepdims=True)
    tot = w1 + w2
    e_ref[0:1, rows] = best * EXPERTS_PER_GROUP + i1
    e_ref[1:2, rows] = best * EXPERTS_PER_GROUP + i2
    gw_ref[0:1, rows] = w1 / tot
    gw_ref[1:2, rows] = w2 / tot


def out_projection_router(ys, stream, mod, gain2, w_out, w_router, router_bias, seq):
    n_batch, lc, d = stream.shape
    gw = GROUP_W
    ctx_tile = seq // TM
    wr_t = w_router.T
    rb = jnp.broadcast_to(router_bias.astype(F32)[:, None], (N_EXPERTS, TM))
    row = lambda n: pl.BlockSpec((None, TM, n), lambda b, i: (b, i, 0))
    const = lambda a: pl.BlockSpec(a.shape, lambda b, i: (0,) * a.ndim)
    lane_row = pl.BlockSpec((None, TOP_K, TM), lambda b, i: (b, 0, i))
    return pl.pallas_call(
        _outproj_kernel, name="out_proj_router",
        out_shape=[jax.ShapeDtypeStruct((n_batch, lc, d), F32), jax.ShapeDtypeStruct((n_batch, lc, d), F32),
                   jax.ShapeDtypeStruct((n_batch, TOP_K, lc), jnp.int32),
                   jax.ShapeDtypeStruct((n_batch, TOP_K, lc), F32)],
        grid=(n_batch, lc // TM),
        in_specs=[row(gw)] * 4 + [row(d), pl.BlockSpec((None, 1, mod.shape[-1]), _mod_index(n_batch, ctx_tile)),
                                  pl.BlockSpec((1, d), lambda b, i: (0, 0)),
                                  pl.BlockSpec(w_out.shape, lambda b, i: (0, 0), pipeline_mode=pl.Buffered(1)),
                                  const(wr_t), const(rb)],
        out_specs=[row(d), row(d), lane_row, lane_row],
        compiler_params=_cparams(("parallel", "arbitrary")),
    )(*ys, stream, mod, gain2.reshape(1, d), w_out, wr_t, rb)


def _expert_kernel(be_ref, nu_ref, x_ref, wg_ref, wu_ref, wd_ref, o_ref, wg_s, wu_s, wd_s):
    i = pl.program_id(0)
    prev = be_ref[jnp.maximum(i - 1, 0)]
    used = i < nu_ref[0]

    @pl.when(used & ((i == 0) | (be_ref[i] != prev)))
    def _():
        wg_s[...] = wg_ref[...].astype(BF16)
        wu_s[...] = wu_ref[...].astype(BF16)
        wd_s[...] = wd_ref[...].astype(BF16)

    @pl.when(used)
    def _():
        x = x_ref[...].astype(BF16)
        a = _dot(x, wg_s[...])
        hb = (a * _sigmoid(a)) * _dot(x, wu_s[...])
        o_ref[...] = _dot(hb.astype(BF16), wd_s[...])

    @pl.when(jnp.logical_not(used))
    def _():
        o_ref[...] = jnp.zeros_like(o_ref)


def expert_ffn(xs, block_e, n_used, wg, wu, wd, layer):
    rows, d = xs.shape
    n_blocks = rows // MOE_BLOCK
    de = wg.shape[-1]
    return pl.pallas_call(
        _expert_kernel, name="moe_experts",
        out_shape=jax.ShapeDtypeStruct((rows, d), F32),
        grid_spec=pltpu.PrefetchScalarGridSpec(
            num_scalar_prefetch=2, grid=(n_blocks,),
            in_specs=[pl.BlockSpec((MOE_BLOCK, d), lambda i, be, nu: (i, 0)),
                      pl.BlockSpec((None, None, d, de), lambda i, be, nu: (layer, be[i], 0, 0)),
                      pl.BlockSpec((None, None, d, de), lambda i, be, nu: (layer, be[i], 0, 0)),
                      pl.BlockSpec((None, None, de, d), lambda i, be, nu: (layer, be[i], 0, 0))],
            out_specs=pl.BlockSpec((MOE_BLOCK, d), lambda i, be, nu: (i, 0)),
            scratch_shapes=[pltpu.VMEM((d, de), BF16), pltpu.VMEM((d, de), BF16), pltpu.VMEM((de, d), BF16)]),
        compiler_params=_cparams(("arbitrary",)),
    )(block_e, n_used, xs, wg, wu, wd)


RANK_WIDTHS = (1280, 1024, 768, 512, 256, 128)


def _rank_kernel(e_ref, tri_ref, rank_ref, count_ref, carry_ref):
    first = (pl.program_id(0) == 0) & (pl.program_id(1) == 0)

    @pl.when(first)
    def _():
        carry_ref[...] = jnp.zeros_like(carry_ref)

    width = e_ref.shape[-1]
    expert = lax.broadcasted_iota(jnp.int32, (N_EXPERTS, width), 0)
    carry = carry_ref[...]
    for k in range(TOP_K):
        onehot = jnp.where(expert == e_ref[k:k + 1, :], 1.0, 0.0)
        before = _dot(onehot.astype(BF16), tri_ref[...])
        rank_ref[k:k + 1, :] = jnp.sum(onehot * (before + carry), axis=0, keepdims=True).astype(jnp.int32)
        carry = carry + jnp.sum(onehot, axis=1, keepdims=True)
    carry_ref[...] = carry
    count_ref[...] = jnp.broadcast_to(carry, count_ref.shape)


def expert_ranks(eidx):
    n_batch, top_k, lc = eidx.shape
    width = next(w for w in RANK_WIDTHS if lc % w == 0)
    tri = jnp.asarray(np.triu(np.ones((width, width), np.float32), 1), BF16)
    blk = pl.BlockSpec((None, top_k, width), lambda b, i: (b, 0, i))
    rank, counts = pl.pallas_call(
        _rank_kernel, name="moe_rank",
        out_shape=[jax.ShapeDtypeStruct((n_batch, top_k, lc), jnp.int32),
                   jax.ShapeDtypeStruct((N_EXPERTS, LANES), F32)],
        grid=(n_batch, lc // width),
        in_specs=[blk, pl.BlockSpec(tri.shape, lambda b, i: (0, 0))],
        out_specs=[blk, pl.BlockSpec((N_EXPERTS, LANES), lambda b, i: (0, 0))],
        scratch_shapes=[pltpu.VMEM((N_EXPERTS, 1), F32)],
        compiler_params=_cparams(("arbitrary", "arbitrary")),
    )(eidx, tri)
    return rank, counts[:, 0].astype(jnp.int32)


def _combine_kernel(x_ref, y0_ref, y1_ref, g0_ref, g1_ref, mod_ref, o_ref):
    d = x_ref.shape[-1]
    y = y0_ref[...] * g0_ref[...] + y1_ref[...] * g1_ref[...]
    o_ref[...] = x_ref[...] + mod_ref[:, 5 * d:6 * d] * y


def moe_combine(stream, y0, y1, g0, g1, mod, seq, n_rows):
    n_batch, lc, d = stream.shape
    row = pl.BlockSpec((None, TM, d), lambda b, i: (b, i, 0))
    col = pl.BlockSpec((None, TM, 1), lambda b, i: (b, i, 0))
    return pl.pallas_call(
        _combine_kernel, name="moe_combine",
        out_shape=jax.ShapeDtypeStruct((n_batch, n_rows, d), F32),
        grid=(n_batch, n_rows // TM),
        in_specs=[row, row, row, col, col, pl.BlockSpec((None, 1, mod.shape[-1]), _mod_index(n_batch, seq // TM))],
        out_specs=row,
        compiler_params=_cparams(("parallel", "arbitrary")),
    )(stream, y0, y1, g0, g1, mod)


def moe_ffn(stream, h2, eidx, gates, mod, wg, wu, wd, layer, seq, n_rows):
    n_batch, lc, d = h2.shape
    n_tok = n_batch * lc
    n_assign = n_tok * TOP_K
    rank, counts = expert_ranks(eidx)
    padded = (counts + MOE_BLOCK - 1) // MOE_BLOCK * MOE_BLOCK
    pad_end = jnp.cumsum(padded)
    experts = jnp.arange(N_EXPERTS, dtype=jnp.int32)
    start = jnp.sum(jnp.where(eidx[..., None] == experts, pad_end - padded, 0), axis=-1)
    dest = start + rank
    n_blocks = n_assign // MOE_BLOCK + N_EXPERTS
    tok = jnp.broadcast_to((jnp.arange(n_batch, dtype=jnp.int32) * lc)[:, None, None]
                           + jnp.arange(lc, dtype=jnp.int32)[None, None, :], dest.shape)
    row_tok = jnp.zeros((n_blocks * MOE_BLOCK,), jnp.int32).at[dest.reshape(-1)].set(tok.reshape(-1))
    block_row = jnp.arange(n_blocks, dtype=jnp.int32) * MOE_BLOCK
    block_e = jnp.minimum(jnp.sum((pad_end[None, :] <= block_row[:, None]).astype(jnp.int32), axis=1), N_EXPERTS - 1)
    n_used = (pad_end[-1:] // MOE_BLOCK).astype(jnp.int32)
    xs = h2.reshape(n_tok, d)[row_tok]
    ys = expert_ffn(xs, block_e, n_used, wg, wu, wd, layer)
    y0 = ys[dest[:, 0].reshape(-1)].reshape(n_batch, lc, d)
    y1 = ys[dest[:, 1].reshape(-1)].reshape(n_batch, lc, d)
    g0 = gates[:, 0].reshape(n_batch, lc, 1)
    g1 = gates[:, 1].reshape(n_batch, lc, 1)
    return moe_combine(stream, y0, y1, g0, g1, mod, seq, n_rows)


def kernel(x, c, ctx, c_ctx, norm1, norm2, w_ada, b_ada, w_in, w_out, rwkv_conv, rwkv_w0, rwkv_w_up, rwkv_a0,
           rwkv_a_up, rwkv_g_up, rwkv_k_k, rwkv_k_a, rwkv_r_k, rwkv_ln_w, rwkv_ln_b, rwkv_v0, rwkv_v_down,
           rwkv_v_up, fourier_w, fourier_b, gmlp_norm, gmlp_ws, gmlp_bs, attn_q_norm, attn_k_norm, attn_sink,
           w_router, router_bias, w_e_gate, w_e_up, w_e_down):
    n_batch, seq, d = x.shape
    depth = w_in.shape[0]
    stream = jnp.concatenate([x, ctx], axis=1)
    c_all = jnp.zeros((SUBLANES, d), F32).at[:n_batch].set(c).at[n_batch].set(c_ctx)
    col_splits = np.cumsum([A_COLS, B_COLS, C_COLS])
    v_first = None
    for l in range(depth):
        mod = ada_modulation(c_all, w_ada, b_ada, l).reshape(SUBLANES, 1, 6 * d)
        w_parts = [w.astype(BF16) for w in jnp.split(w_in[l], col_splits, axis=1)]
        pa, pb, pc, pd = in_projection(stream, mod, norm1[l], w_parts, seq)

        vres = None if l == 0 else (rwkv_v0[l - 1], rwkv_v_down[l - 1], rwkv_v_up[l - 1])
        r, kk, v, k0, wd0, kka0, k1, wd1, kka1, g = rwkv_prepare(
            pa, seq, rwkv_conv[l], rwkv_w0[l], rwkv_w_up[l], rwkv_a0[l], rwkv_a_up[l], rwkv_g_up[l],
            rwkv_k_k[l], rwkv_k_a[l], v_first, vres)
        if l == 0:
            v_first = v
        yf, yb = rwkv_scan(r, kk, v, ((k0, wd0, kka0), (k1, wd1, kka1)), seq)
        y_a = rwkv_readout(yf, yb, r, k0, k1, v, g, rwkv_r_k[l].reshape(-1), rwkv_ln_w[l], rwkv_ln_b[l])

        y_b = fourier_mix(pb, seq, fourier_w[l], fourier_b[l])
        y_c = gmlp_mix(pc, gmlp_norm[l], gmlp_ws[l], gmlp_bs[l])
        q, kv, vv = attention_prepare(pd, seq, attn_q_norm[l], attn_k_norm[l])
        y_d = window_attention(q, kv, vv, attn_sink[l], seq)

        stream, h2, eidx, gates = out_projection_router(
            (y_a, y_b, y_c, y_d), stream, mod, norm2[l], w_out[l].astype(BF16), w_router, router_bias, seq)
        n_rows = seq if l == depth - 1 else stream.shape[1]
        stream = moe_ffn(stream, h2, eidx, gates, mod, w_e_gate, w_e_up, w_e_down, l, seq, n_rows)
    return stream
```

```python
import functools

import numpy as np
import jax
import jax.numpy as jnp
from jax import lax
from jax.experimental import pallas as pl
from jax.experimental.pallas import tpu as pltpu

F32 = jnp.float32
BF16 = jnp.bfloat16

D_MODEL = 2048
DEPTH = 2
GRID_W = 64
EPS = 1e-6
NEG_INF = -1e30
GROUP_W = 512
RWKV_HEAD = 64
RWKV_HEADS = 8
DECAY_LORA = 64
ICLR_LORA = 64
GATE_LORA = 128
VRES_LORA = 32
RWKV_GN_EPS = 64e-5
FOURIER_GROUPS = 4
FOURIER_CH = 128
GMLP_GROUPS = 4
GMLP_CH = 128
GMLP_CHUNK = 128
ATT_HEAD = 64
ATT_Q_HEADS = 8
ATT_KV_HEADS = 2
WINDOW = 128
ATT_BLOCK = 128
ROPE_BASE = 10000.0
N_EXPERTS = 64
N_EXPERT_GROUPS = 8
EXPERTS_PER_GROUP = 8
TOP_K = 2
D_EXPERT = 512
MOE_BLOCK = 128
A_COLS = 1920
B_COLS = 512
C_COLS = 1024
D_COLS = 768

LANES = 128
SUBLANES = 8
TM = 256
SCAN_TB = 128
SCAN_GROUP = SUBLANES
OUTPROJ_SPLIT = 2
VMEM_LIMIT = 56 * 1024 * 1024


def _cparams(sem, vmem=VMEM_LIMIT):
    return pltpu.CompilerParams(dimension_semantics=sem, vmem_limit_bytes=vmem)


def _dot(a, b):
    return jnp.dot(a, b, preferred_element_type=F32)


def _dot_nt(a, b):
    return lax.dot_general(a, b, (((1,), (1,)), ((), ())), preferred_element_type=F32)


def _dot_tn(a, b):
    return lax.dot_general(a, b, (((0,), (0,)), ((), ())), preferred_element_type=F32)


def _split(x):
    hi = x.astype(BF16)
    lo = (x - hi.astype(F32)).astype(BF16)
    return hi, lo


def _pack_bf16_halves(x):
    n = x.shape[1] // 2
    lo = lax.bitcast_convert_type(x[:, :n].astype(BF16).astype(F32), jnp.uint32) >> 16
    hi = lax.bitcast_convert_type(x[:, n:].astype(BF16).astype(F32), jnp.uint32) & jnp.uint32(0xFFFF0000)
    return hi | lo


def _unpack_bf16_halves(w):
    lo = lax.bitcast_convert_type(w << 16, F32).astype(BF16)
    hi = lax.bitcast_convert_type(w & jnp.uint32(0xFFFF0000), F32).astype(BF16)
    return lo, hi


def _dot1(a, b):
    return _dot(a.astype(BF16), b.astype(BF16))


def _segsum(x, seg):
    xh, xl = _split(x)
    return _dot(xh, seg) + _dot(xl, seg)


def _sigmoid(x):
    return 1.0 / (1.0 + jnp.exp(-x))


def _seg_matrix(width, seg):
    i = np.arange(width) // seg
    return jnp.asarray((i[:, None] == i[None, :]).astype(np.float32), dtype=BF16)


def _ada_kernel(c_ref, w_ref, b_ref, o_ref):
    c = c_ref[...]
    a = (c * _sigmoid(c)).astype(BF16)
    o_ref[...] = _dot(a, w_ref[...].astype(BF16)) + b_ref[...]


def ada_modulation(c_all, w_ada, b_ada, layer, tn=1024):
    rows, d = c_all.shape
    n_layers, _, n = w_ada.shape
    return pl.pallas_call(
        _ada_kernel, name="ada_mod",
        out_shape=jax.ShapeDtypeStruct((rows, n), F32),
        grid=(n // tn,),
        in_specs=[pl.BlockSpec((rows, d), lambda j: (0, 0)),
                  pl.BlockSpec((None, d, tn), lambda j: (layer, 0, j)),
                  pl.BlockSpec((None, 1, tn), lambda j: (layer, 0, j))],
        out_specs=pl.BlockSpec((rows, tn), lambda j: (0, j)),
        compiler_params=_cparams(("arbitrary",)),
    )(c_all, w_ada, b_ada.reshape(n_layers, 1, n))


def _inproj_kernel(x_ref, mod_ref, g_ref, wa_ref, wb_ref, wc_ref, wd_ref, pa_ref, pb_ref, pc_ref, pd_ref):
    d = x_ref.shape[-1]
    x = x_ref[...]
    y = x * lax.rsqrt(jnp.mean(x * x, axis=-1, keepdims=True) + EPS) * g_ref[...]
    sh = mod_ref[:, 0:d]
    sc = mod_ref[:, d:2 * d]
    h = (y * (1.0 + sc) + sh).astype(BF16)
    pa_ref[...] = _dot(h, wa_ref[...])
    pb_ref[...] = _dot(h, wb_ref[...])
    pc_ref[...] = _dot(h, wc_ref[...])
    pd_ref[...] = _dot(h, wd_ref[...])


def _mod_index(n_batch, ctx_tile):
    return lambda b, i: (jnp.where(i >= ctx_tile, n_batch, b), 0, 0)


def in_projection(stream, mod, gain, w_parts, seq):
    n_batch, lc, d = stream.shape
    ctx_tile = seq // TM
    resident = lambda w: pl.BlockSpec(w.shape, lambda b, i: (0, 0), pipeline_mode=pl.Buffered(1))
    row = lambda n: pl.BlockSpec((None, TM, n), lambda b, i: (b, i, 0))
    return pl.pallas_call(
        _inproj_kernel, name="in_proj",
        out_shape=[jax.ShapeDtypeStruct((n_batch, lc, w.shape[1]), F32) for w in w_parts],
        grid=(n_batch, lc // TM),
        in_specs=[row(d),
                  pl.BlockSpec((None, 1, mod.shape[-1]), _mod_index(n_batch, ctx_tile)),
                  pl.BlockSpec((1, d), lambda b, i: (0, 0))] + [resident(w) for w in w_parts],
        out_specs=[row(w.shape[1]) for w in w_parts],
        compiler_params=_cparams(("parallel", "arbitrary")),
    )(stream, mod, gain.reshape(1, d), *w_parts)


def _rwkv_prep_kernel(has_vres, ctx_tile, *refs):
    (pa_ref, hp_ref, hn_ref, conv_ref, w0_ref, wup_ref, a0_ref, aup_ref, gup_ref, kk_par_ref, ka_par_ref,
     seg_ref) = refs[:12]
    rest = refs[12:]
    if has_vres:
        vf_ref, v0_ref, vdn_ref, vup_ref = rest[:4]
        rest = rest[4:]
    r_o, kk_o, v_o, k0_o, w0_o, kka0_o, k1_o, w1_o, kka1_o, g_o = rest
    gw = GROUP_W
    x = pa_ref[...]
    tm = x.shape[0]
    i = pl.program_id(1)
    first = (i == 0) | (i == ctx_tile)
    last = (i == ctx_tile - 1) | (i == pl.num_programs(1) - 1)
    halo_prev = jnp.where(first, 0.0, hp_ref[SUBLANES - 1:SUBLANES, :])
    halo_next = jnp.where(last, 0.0, hn_ref[0:1, :])
    row = lax.broadcasted_iota(jnp.int32, x.shape, 0)
    x_prev = jnp.where(row == 0, halo_prev, pltpu.roll(x, 1, 0))
    x_next = jnp.where(row == tm - 1, halo_next, pltpu.roll(x, tm - 1, 0))
    y = x_prev * conv_ref[0:1, :] + x * conv_ref[1:2, :] + x_next * conv_ref[2:3, :]
    r = y[:, 0:gw]
    k = y[:, gw:2 * gw]
    v = y[:, 2 * gw:3 * gw]
    wl = y[:, 3 * gw:3 * gw + 2 * DECAY_LORA]
    al = y[:, 3 * gw + 2 * DECAY_LORA:3 * gw + 2 * DECAY_LORA + 2 * ICLR_LORA]
    gl = y[:, 3 * gw + 2 * DECAY_LORA + 2 * ICLR_LORA:]
    if has_vres:
        mix = _sigmoid(v0_ref[...] + _dot1(_dot1(v, vdn_ref[...]), vup_ref[...]))
        v = v + (vf_ref[...] - v) * mix
    kk = k * kk_par_ref[...]
    nrm = jnp.maximum(jnp.sqrt(_segsum(kk * kk, seg_ref[...])), 1e-12)
    kk = kk / nrm
    r_o[...] = r
    kk_o[...] = kk
    v_o[...] = v
    tanh_wl = jnp.tanh(wl)
    for d, (k_o, w_o, kka_o) in enumerate(((k0_o, w0_o, kka0_o), (k1_o, w1_o, kka1_o))):
        w_raw = w0_ref[d:d + 1, :] + _dot1(tanh_wl, wup_ref[d])
        z = -w_raw
        softplus = jnp.maximum(z, 0.0) + jnp.log(1.0 + jnp.exp(-jnp.abs(z)))
        w_o[...] = jnp.exp(-jnp.exp(-softplus - 0.5))
        a = _sigmoid(a0_ref[d:d + 1, :] + _dot1(al, aup_ref[d]))
        k_o[...] = k * (1.0 + (a - 1.0) * ka_par_ref[...])
        kka_o[...] = kk * a
    g_o[...] = _dot1(_sigmoid(gl), gup_ref[...])


def rwkv_prepare(pa, seq, conv_w, w0, w_up, a0, a_up, g_up, k_k, k_a, v_first=None, vres=None):
    n_batch, lc, c = pa.shape
    gw = GROUP_W
    sub_per_tile = TM // SUBLANES
    n_sub = lc // SUBLANES
    pad_rows = lambda w, d, n: jnp.zeros((2 * n, gw), F32).at[d * n:(d + 1) * n].set(w)
    wup = jnp.stack([pad_rows(w_up[d], d, DECAY_LORA) for d in range(2)])
    aup = jnp.stack([pad_rows(a_up[d], d, ICLR_LORA) for d in range(2)])
    seg = _seg_matrix(gw, RWKV_HEAD)
    const = lambda a: pl.BlockSpec(a.shape, lambda b, i: (0,) * a.ndim)
    row = lambda n: pl.BlockSpec((None, TM, n), lambda b, i: (b, i, 0))
    halo_prev = pl.BlockSpec((None, SUBLANES, c), lambda b, i: (b, jnp.maximum(i * sub_per_tile - 1, 0), 0))
    halo_next = pl.BlockSpec((None, SUBLANES, c), lambda b, i: (b, jnp.minimum((i + 1) * sub_per_tile, n_sub - 1), 0))
    args = [pa, pa, pa, conv_w, w0, wup, a0, aup, g_up, k_k.reshape(1, gw), k_a.reshape(1, gw), seg]
    specs = [row(c), halo_prev, halo_next] + [const(a) for a in args[3:]]
    has_vres = vres is not None
    if has_vres:
        v0, v_down, v_up = vres
        vdn = jnp.zeros((gw, LANES), F32).at[:, :VRES_LORA].set(v_down)
        vup = jnp.zeros((LANES, gw), F32).at[:VRES_LORA].set(v_up)
        extra = [v_first, v0.reshape(1, gw), vdn, vup]
        args += extra
        specs += [row(gw)] + [const(a) for a in extra[1:]]
    return pl.pallas_call(
        functools.partial(_rwkv_prep_kernel, has_vres, seq // TM), name="rwkv_prep",
        out_shape=[jax.ShapeDtypeStruct((n_batch, lc, gw), F32)] * 10,
        grid=(n_batch, lc // TM),
        in_specs=specs,
        out_specs=[row(gw)] * 10,
        compiler_params=_cparams(("parallel", "arbitrary")),
    )(*args)


def _scan_kernel(n_batch, rf, kkf, vf, kf, wf, kaf, rb, kkb, vb, kb, wb, kab, bsel_ref, segbd_ref,
                 yf_ref, yb_ref, s_ref, vk_ref, yt_ref):
    tb = rf.shape[1]
    n_pairs = GROUP_W // LANES
    half = RWKV_HEAD

    @pl.when(pl.program_id(0) == 0)
    def _():
        s_ref[...] = jnp.zeros_like(s_ref)
        yt_ref[...] = jnp.zeros_like(yt_ref)

    segbd = segbd_ref[...]
    dirs = ((rf, kkf, vf, kf, wf, kaf, yf_ref), (rb, kkb, vb, kb, wb, kab, yb_ref))
    chains = [(d, b, p) for d in range(2) for b in range(n_batch) for p in range(n_pairs)]
    n_ch = len(chains)
    lane = lax.broadcasted_iota(jnp.int32, (RWKV_HEAD, LANES), 1) % half

    n_sets = 2
    per_set = n_ch // n_sets
    sets = [list(range(si * per_set, (si + 1) * per_set)) for si in range(n_sets)]

    def side_by_side(parts):
        return jnp.concatenate([jnp.concatenate(parts[i:i + 2], axis=1) for i in range(0, per_set, 2)], axis=0)

    def piece(full, n):
        return full[(n // 2) * RWKV_HEAD:(n // 2 + 1) * RWKV_HEAD, (n % 2) * LANES:(n % 2 + 1) * LANES]

    def group(g, carry):
        bases = (pl.multiple_of(g * SCAN_GROUP, SCAN_GROUP),
                 pl.multiple_of(tb - SCAN_GROUP - g * SCAN_GROUP, SCAN_GROUP))

        def row(which, ci, j):
            d, b, p = chains[ci]
            t = j if d == 0 else SCAN_GROUP - 1 - j
            return dirs[d][which][b, pl.ds(bases[d], SCAN_GROUP), p * LANES:(p + 1) * LANES][t:t + 1]

        def sk_dot(states, cs, j):
            return _dot(side_by_side([(states[n] * row(1, ci, j)).astype(BF16) for n, ci in enumerate(cs)]), segbd)

        sk = []
        for cs in sets:
            for ci in cs:
                d, b, p = chains[ci]
                v8 = dirs[d][2][b, pl.ds(bases[d], SCAN_GROUP), p * LANES:(p + 1) * LANES]
                k8 = dirs[d][3][b, pl.ds(bases[d], SCAN_GROUP), p * LANES:(p + 1) * LANES]
                vt = jnp.concatenate([v8[:, :RWKV_HEAD], v8[:, RWKV_HEAD:]], axis=0).T.astype(BF16)
                kmat = jnp.concatenate([bsel_ref[t] * k8[t:t + 1] for t in range(SCAN_GROUP)],
                                       axis=1).astype(BF16)
                vk_ref[ci] = _dot(vt, kmat)
            sk.append(sk_dot([s_ref[ci] for ci in cs], cs, 0))
        for j in range(SCAN_GROUP):
            for si, cs in enumerate(sets):
                states = []
                for n, ci in enumerate(cs):
                    d = chains[ci][0]
                    t = j if d == 0 else SCAN_GROUP - 1 - j
                    s = (s_ref[ci] * row(4, ci, j) - piece(sk[si], n) * row(5, ci, j)
                         + vk_ref[ci, :, t * LANES:(t + 1) * LANES])
                    s_ref[ci] = s
                    states.append(s)
                y_all = _dot(side_by_side([(s * row(0, ci, j)).astype(BF16) for s, ci in zip(states, cs)]), segbd)
                if j + 1 < SCAN_GROUP:
                    sk[si] = sk_dot(states, cs, j + 1)
                for n, ci in enumerate(cs):
                    d = chains[ci][0]
                    t = j if d == 0 else SCAN_GROUP - 1 - j
                    yt_ref[ci] = jnp.where(lane == (bases[d] + t) % half, piece(y_all, n), yt_ref[ci])
        return carry

    def flush(first_row):
        for ci, (d, b, p) in enumerate(chains):
            t = yt_ref[ci].T
            rows = slice(first_row[d], first_row[d] + half)
            dirs[d][6][b, rows, p * LANES:(p + 1) * LANES] = jnp.concatenate([t[:half], t[half:]], axis=1)

    groups_per_flush = half // SCAN_GROUP
    for part in range(tb // half):
        lax.fori_loop(part * groups_per_flush, (part + 1) * groups_per_flush, group, 0)
        flush((part * half, tb - (part + 1) * half))


def _scan_selectors():
    bsel = np.zeros((SCAN_GROUP, 2 * SCAN_GROUP, LANES), np.float32)
    for h in range(2):
        for j in range(SCAN_GROUP):
            bsel[j, h * SCAN_GROUP + j, h * RWKV_HEAD:(h + 1) * RWKV_HEAD] = 1.0
    return jnp.asarray(bsel, F32), _seg_matrix(2 * LANES, RWKV_HEAD)


def rwkv_scan(r, kk, v, dirs, seq):
    n_batch, lc, gw = r.shape
    tb = SCAN_TB
    nx = seq // tb
    nc = (lc - seq) // tb
    n_chains = 2 * n_batch * (gw // LANES)
    fwd = lambda i: (0, jnp.where(i < nc, nx + i, i - nc), 0)
    bwd = lambda i: (0, nx + nc - 1 - i, 0)
    blk = lambda m: pl.BlockSpec((n_batch, tb, gw), m)
    const = lambda a: pl.BlockSpec(a.shape, lambda i: (0,) * a.ndim)
    bsel, segbd = _scan_selectors()
    return pl.pallas_call(
        functools.partial(_scan_kernel, n_batch), name="rwkv_scan",
        out_shape=[jax.ShapeDtypeStruct((n_batch, lc, gw), F32)] * 2,
        grid=(nx + nc,),
        in_specs=[blk(fwd)] * 6 + [blk(bwd)] * 6 + [const(bsel), const(segbd)],
        out_specs=[blk(fwd), blk(bwd)],
        scratch_shapes=[pltpu.VMEM((n_chains, RWKV_HEAD, LANES), F32),
                        pltpu.VMEM((n_chains, RWKV_HEAD, SCAN_GROUP * LANES), F32),
                        pltpu.VMEM((n_chains, RWKV_HEAD, LANES), F32)],
        compiler_params=_cparams(("arbitrary",)),
    )(r, kk, v, *dirs[0], r, kk, v, *dirs[1], bsel, segbd)


def _rwkv_readout_kernel(yf_ref, yb_ref, r_ref, k0_ref, k1_ref, v_ref, g_ref, rk_ref, lnw_ref, lnb_ref, seg_ref,
                         o_ref):
    seg = seg_ref[...]
    inv = 1.0 / RWKV_HEAD
    y = yf_ref[...] + yb_ref[...]
    mu = _segsum(y, seg) * inv
    dlt = y - mu
    var = _segsum(dlt * dlt, seg) * inv
    yn = dlt * lax.rsqrt(var + RWKV_GN_EPS) * lnw_ref[...] + lnb_ref[...]
    kbar = 0.5 * (k0_ref[...] + k1_ref[...])
    bonus = _segsum(r_ref[...] * kbar * rk_ref[...], seg) * v_ref[...]
    o_ref[...] = ((yn + bonus) * g_ref[...]).astype(o_ref.dtype)


def rwkv_readout(yf, yb, r, k0, k1, v, g, r_k, ln_w, ln_b):
    n_batch, lc, gw = r.shape
    seg = _seg_matrix(gw, RWKV_HEAD)
    row = pl.BlockSpec((None, TM, gw), lambda b, i: (b, i, 0))
    const = lambda a: pl.BlockSpec(a.shape, lambda b, i: (0,) * a.ndim)
    params = [r_k.reshape(1, gw), ln_w.reshape(1, gw), ln_b.reshape(1, gw), seg]
    return pl.pallas_call(
        _rwkv_readout_kernel, name="rwkv_readout",
        out_shape=jax.ShapeDtypeStruct((n_batch, lc, gw), BF16),
        grid=(n_batch, lc // TM),
        in_specs=[row] * 7 + [const(a) for a in params],
        out_specs=row,
        compiler_params=_cparams(("parallel", "arbitrary")),
    )(yf, yb, r, k0, k1, v, g, *params)


def _dft_tables(n):
    k = np.arange(n)
    ang = 2.0 * np.pi * ((k[:, None] * k[None, :]) % n) / n
    return np.cos(ang), np.sin(ang)


def _fft_a_kernel(nb, f_ref, gr_ref, gi_ref, zr_ref, zi_ref):
    n1 = gr_ref.shape[1]
    n2_total = f_ref.shape[0] // n1
    j = pl.program_id(2)
    for i in range(nb):
        rows = f_ref[pl.ds(j * nb + i, n1, stride=n2_total), :].astype(BF16)
        zr_ref[i * n1:(i + 1) * n1, :] = _dot(gr_ref[i], rows)
        zi_ref[i * n1:(i + 1) * n1, :] = _dot(gi_ref[i], rows)


def _fft_b_kernel(kb, scale, zr_ref, zi_ref, fc_ref, fs_ref, cc_ref, cs_ref, w_ref, b_ref, o_ref):
    n2 = fc_ref.shape[0]
    n1 = zr_ref.shape[0] // n2
    j = pl.program_id(2)
    fc, fs = fc_ref[...], fs_ref[...]
    for i in range(kb):
        k1 = j * kb + i
        zr = zr_ref[pl.ds(k1, n2, stride=n1), :].astype(BF16)
        zi = zi_ref[pl.ds(k1, n2, stride=n1), :].astype(BF16)
        xr = _dot(fc, zr) + _dot(fs, zi)
        xi = _dot(fc, zi) - _dot(fs, zr)
        re = (_dot(xr.astype(BF16), cc_ref[...]) + _dot(xi.astype(BF16), cs_ref[...])) * scale
        o_ref[pl.ds(k1, n2, stride=n1), :] = _dot(re.astype(BF16), w_ref[...].astype(BF16)) + b_ref[...]


def _dft_dense_kernel(scale, f_ref, fc_ref, fs_ref, cc_ref, cs_ref, w_ref, b_ref, o_ref):
    f = f_ref[...].astype(BF16)
    xr = _dot(fc_ref[...], f)
    xi = -_dot(fs_ref[...], f)
    re = (_dot(xr.astype(BF16), cc_ref[...]) + _dot(xi.astype(BF16), cs_ref[...])) * scale
    o_ref[...] = _dot(re.astype(BF16), w_ref[...].astype(BF16)) + b_ref[...]


def fourier_mix(pb, seq, w, bias):
    n_batch, lc, gw = pb.shape
    ctx = lc - seq
    ch = FOURIER_CH
    groups = FOURIER_GROUPS
    n2 = LANES
    n1 = seq // n2
    cc, cs = (jnp.asarray(t, BF16) for t in _dft_tables(ch))
    bias3 = bias.reshape(groups, 1, ch)

    k1 = np.arange(n1)[None, :, None]
    m1 = np.arange(n1)[None, None, :]
    m2 = np.arange(n2)[:, None, None]
    ang = 2.0 * np.pi * ((k1 * (n2 * m1 + m2)) % seq) / seq
    g_re = jnp.asarray(np.cos(ang), BF16)
    g_im = jnp.asarray(-np.sin(ang), BF16)
    nb = min(16, n2)
    slab = pl.BlockSpec((None, seq, ch), lambda b, g, j: (b, 0, g))
    gspec = pl.BlockSpec((nb, n1, n1), lambda b, g, j: (j, 0, 0))
    zspec = pl.BlockSpec((None, None, nb * n1, ch), lambda b, g, j: (b, g, j, 0))
    zr, zi = pl.pallas_call(
        functools.partial(_fft_a_kernel, nb), name="fft_stage_a",
        out_shape=[jax.ShapeDtypeStruct((n_batch, groups, seq, ch), F32)] * 2,
        grid=(n_batch, groups, n2 // nb),
        in_specs=[slab, gspec, gspec],
        out_specs=[zspec, zspec],
        compiler_params=_cparams(("parallel", "parallel", "arbitrary")),
    )(pb, g_re, g_im)

    fc, fs = (jnp.asarray(t, BF16) for t in _dft_tables(n2))
    kb = min(16, n1)
    scale = 1.0 / np.sqrt(float(seq) * ch)
    zslab = pl.BlockSpec((None, None, seq, ch), lambda b, g, j: (b, g, 0, 0), pipeline_mode=pl.Buffered(1))
    const = lambda a: pl.BlockSpec(a.shape, lambda b, g, j: (0,) * a.ndim)
    wspec = pl.BlockSpec((None, ch, ch), lambda b, g, j: (g, 0, 0))
    bspec = pl.BlockSpec((None, 1, ch), lambda b, g, j: (g, 0, 0))
    y_x = pl.pallas_call(
        functools.partial(_fft_b_kernel, kb, scale), name="fft_stage_b",
        out_shape=jax.ShapeDtypeStruct((n_batch, seq, gw), F32),
        grid=(n_batch, groups, n1 // kb),
        in_specs=[zslab, zslab, const(fc), const(fs), const(cc), const(cs), wspec, bspec],
        out_specs=pl.BlockSpec((None, seq, ch), lambda b, g, j: (b, 0, g)),
        compiler_params=_cparams(("parallel", "parallel", "arbitrary")),
    )(zr, zi, fc, fs, cc, cs, w, bias3)

    fcc, fsc = (jnp.asarray(t, BF16) for t in _dft_tables(ctx))
    const2 = lambda a: pl.BlockSpec(a.shape, lambda b, g: (0,) * a.ndim)
    ctx_tile = seq // ctx
    y_c = pl.pallas_call(
        functools.partial(_dft_dense_kernel, 1.0 / np.sqrt(float(ctx) * ch)), name="dft_ctx",
        out_shape=jax.ShapeDtypeStruct((n_batch, ctx, gw), F32),
        grid=(n_batch, groups),
        in_specs=[pl.BlockSpec((None, ctx, ch), lambda b, g: (b, ctx_tile, g)),
                  const2(fcc), const2(fsc), const2(cc), const2(cs),
                  pl.BlockSpec((None, ch, ch), lambda b, g: (g, 0, 0)),
                  pl.BlockSpec((None, 1, ch), lambda b, g: (g, 0, 0))],
        out_specs=pl.BlockSpec((None, ctx, ch), lambda b, g: (b, 0, g)),
        compiler_params=_cparams(("parallel", "arbitrary")),
    )(pb, fcc, fsc, cc, cs, w, bias3)
    return jnp.concatenate([y_x, y_c], axis=1)


def _gmlp_kernel(pc_ref, norm_ref, ws_ref, bs_ref, o_ref):
    gw = GROUP_W
    x = pc_ref[...]
    z = 0.5 * x * (1.0 + jnp.tanh(0.7978845608028654 * (x + 0.044715 * (x * x * x))))
    tm = x.shape[0]
    for g in range(GMLP_GROUPS):
        u = z[:, g * GMLP_CH:(g + 1) * GMLP_CH]
        v = z[:, gw + g * GMLP_CH:gw + (g + 1) * GMLP_CH]
        v = v * lax.rsqrt(jnp.mean(v * v, axis=-1, keepdims=True) + EPS) * norm_ref[g:g + 1, :]
        v = v.astype(BF16)
        ws = ws_ref[g].astype(BF16)
        for c in range(tm // GMLP_CHUNK):
            rows = slice(c * GMLP_CHUNK, (c + 1) * GMLP_CHUNK)
            f = _dot(ws, v[rows]) + bs_ref[g]
            o_ref[rows, g * GMLP_CH:(g + 1) * GMLP_CH] = (u[rows] * f).astype(o_ref.dtype)


def gmlp_mix(pc, norm_g, ws, bs):
    n_batch, lc, c = pc.shape
    bs_b = jnp.broadcast_to(bs[:, :, None], bs.shape + (GMLP_CH,))
    const = lambda a: pl.BlockSpec(a.shape, lambda b, i: (0,) * a.ndim)
    return pl.pallas_call(
        _gmlp_kernel, name="gmlp",
        out_shape=jax.ShapeDtypeStruct((n_batch, lc, GROUP_W), BF16),
        grid=(n_batch, lc // TM),
        in_specs=[pl.BlockSpec((None, TM, c), lambda b, i: (b, i, 0)), const(norm_g), const(ws), const(bs_b)],
        out_specs=pl.BlockSpec((None, TM, GROUP_W), lambda b, i: (b, i, 0)),
        compiler_params=_cparams(("parallel", "arbitrary")),
    )(pc, norm_g, ws, bs_b)


def _rope_tables(seq, ctx):
    half = ATT_HEAD // 4
    inv_freq = ROPE_BASE ** (-np.arange(half, dtype=np.float64) / half)
    pos = np.arange(seq)
    ang_r = (pos // GRID_W)[:, None] * inv_freq[None, :]
    ang_c = (pos % GRID_W)[:, None] * inv_freq[None, :]
    cos = np.concatenate([np.cos(ang_r)] * 2 + [np.cos(ang_c)] * 2, axis=1)
    sin = np.concatenate([-np.sin(ang_r), np.sin(ang_r), -np.sin(ang_c), np.sin(ang_c)], axis=1)
    cos = np.concatenate([cos, np.ones((ctx, ATT_HEAD))], axis=0)
    sin = np.concatenate([sin, np.zeros((ctx, ATT_HEAD))], axis=0)
    return jnp.asarray(np.tile(cos, (1, 2)), F32), jnp.asarray(np.tile(sin, (1, 2)), F32)


def _rope(t, cos, sin):
    n = t.shape[1]
    q = ATT_HEAD // 4
    lane = lax.broadcasted_iota(jnp.int32, t.shape, 1)
    swapped = jnp.where((lane % (2 * q)) < q, pltpu.roll(t, n - q, 1), pltpu.roll(t, q, 1))
    return t * cos + swapped * sin


def _attn_prep_kernel(pd_ref, cos_ref, sin_ref, qg_ref, kg_ref, seg_ref, q_o, k_o, v_o):
    gw = GROUP_W
    kvw = ATT_KV_HEADS * ATT_HEAD
    seg = seg_ref[...]
    inv = 1.0 / ATT_HEAD
    cos, sin = cos_ref[...], sin_ref[...]
    q = pd_ref[:, 0:gw]
    q = q * lax.rsqrt(_segsum(q * q, seg) * inv + EPS) * qg_ref[...]
    q = _rope(q, jnp.concatenate([cos] * (gw // kvw), axis=1), jnp.concatenate([sin] * (gw // kvw), axis=1))
    q_o[...] = (q * (ATT_HEAD ** -0.5)).astype(q_o.dtype)
    k = pd_ref[:, gw:gw + kvw]
    k = k * lax.rsqrt(_segsum(k * k, seg[:kvw, :kvw]) * inv + EPS) * kg_ref[...]
    k = _rope(k, cos, sin)
    v = pd_ref[:, gw + kvw:gw + 2 * kvw]
    lane = lax.broadcasted_iota(jnp.int32, k.shape, 1)

    def variants(t):
        h0 = jnp.where(lane < ATT_HEAD, t, 0.0)
        h1 = jnp.where(lane >= ATT_HEAD, t, 0.0)
        return jnp.concatenate([h0, pltpu.roll(h0, ATT_HEAD, 1), pltpu.roll(h1, ATT_HEAD, 1), h1], axis=1)

    k_o[...] = variants(k).astype(k_o.dtype)
    v_o[...] = variants(v).astype(v_o.dtype)


def attention_prepare(pd, seq, q_gain, k_gain):
    n_batch, lc, c = pd.shape
    gw = GROUP_W
    kvw = ATT_KV_HEADS * ATT_HEAD
    cos, sin = _rope_tables(seq, lc - seq)
    seg = _seg_matrix(gw, ATT_HEAD)
    qg = jnp.tile(q_gain, ATT_Q_HEADS).reshape(1, gw)
    kg = jnp.tile(k_gain, ATT_KV_HEADS).reshape(1, kvw)
    const = lambda a: pl.BlockSpec(a.shape, lambda b, i: (0,) * a.ndim)
    tab = pl.BlockSpec((TM, kvw), lambda b, i: (i, 0))
    row = lambda n: pl.BlockSpec((None, TM, n), lambda b, i: (b, i, 0))
    return pl.pallas_call(
        _attn_prep_kernel, name="attn_prep",
        out_shape=[jax.ShapeDtypeStruct((n_batch, lc, gw), BF16)] * 3,
        grid=(n_batch, lc // TM),
        in_specs=[row(c), tab, tab, const(qg), const(kg), const(seg)],
        out_specs=[row(gw)] * 3,
        compiler_params=_cparams(("parallel", "arbitrary")),
    )(pd, cos, sin, qg, kg, seg)


def _attn_kernel(nb, sink_ref, q_ref, kp_ref, kc_ref, kn_ref, kx_ref, vp_ref, vc_ref, vn_ref, vx_ref, o_ref):
    i = pl.program_id(1)
    blk = q_ref.shape[0]
    is_lat = i < nb
    rowi = lax.broadcasted_iota(jnp.int32, (blk, blk), 0)
    coli = lax.broadcasted_iota(jnp.int32, (blk, blk), 1)
    m_prev = (coli >= rowi) & is_lat & (i >= 1)
    m_cur = jnp.broadcast_to(is_lat, (blk, blk))
    m_next = (coli <= rowi) & (i < nb - 1)
    for p in range(ATT_Q_HEADS // 2):
        q = q_ref[:, p * LANES:(p + 1) * LANES]
        acc = None
        for par in range(2):
            h = 2 * p + par
            g = h // (ATT_Q_HEADS // ATT_KV_HEADS)
            col = slice((2 * g + par) * LANES, (2 * g + par + 1) * LANES)
            sink = sink_ref[h]
            s1 = jnp.where(m_prev, _dot_nt(q, kp_ref[:, col]), NEG_INF)
            s2 = jnp.where(m_cur, _dot_nt(q, kc_ref[:, col]), NEG_INF)
            s3 = jnp.where(m_next, _dot_nt(q, kn_ref[:, col]), NEG_INF)
            sx = _dot_nt(q, kx_ref[:, col])
            m = jnp.maximum(jnp.maximum(jnp.max(jnp.maximum(jnp.maximum(s1, s2), s3), axis=-1, keepdims=True),
                                        jnp.max(sx, axis=-1, keepdims=True)), sink)
            p1, p2, p3, px = jnp.exp(s1 - m), jnp.exp(s2 - m), jnp.exp(s3 - m), jnp.exp(sx - m)
            den = (jnp.sum(p1 + p2 + p3, axis=-1, keepdims=True) + jnp.sum(px, axis=-1, keepdims=True)
                   + jnp.exp(sink - m))
            o = (_dot(p1.astype(BF16), vp_ref[:, col]) + _dot(p2.astype(BF16), vc_ref[:, col])
                 + _dot(p3.astype(BF16), vn_ref[:, col]) + _dot(px.astype(BF16), vx_ref[:, col])) / den
            acc = o if acc is None else acc + o
        o_ref[:, p * LANES:(p + 1) * LANES] = acc.astype(o_ref.dtype)


def window_attention(q, kv, vv, sink, seq):
    n_batch, lc, gw = q.shape
    blk = ATT_BLOCK
    nb = seq // blk
    ctx = lc - seq
    wide = kv.shape[-1]
    qspec = pl.BlockSpec((None, blk, gw), lambda b, i, s: (b, i, 0))
    prev = pl.BlockSpec((None, blk, wide), lambda b, i, s: (b, jnp.clip(i - 1, 0, nb - 1), 0))
    cur = pl.BlockSpec((None, blk, wide), lambda b, i, s: (b, jnp.minimum(i, nb - 1), 0))
    nxt = pl.BlockSpec((None, blk, wide), lambda b, i, s: (b, jnp.clip(i + 1, 0, nb - 1), 0))
    cx = pl.BlockSpec((None, ctx, wide), lambda b, i, s: (b, seq // ctx, 0))
    return pl.pallas_call(
        functools.partial(_attn_kernel, nb), name="window_attn",
        out_shape=jax.ShapeDtypeStruct((n_batch, lc, gw), BF16),
        grid_spec=pltpu.PrefetchScalarGridSpec(
            num_scalar_prefetch=1, grid=(n_batch, lc // blk),
            in_specs=[qspec, prev, cur, nxt, cx, prev, cur, nxt, cx],
            out_specs=qspec),
        compiler_params=_cparams(("parallel", "arbitrary")),
    )(sink.astype(F32), q, kv, kv, kv, kv, vv, vv, vv, vv)


def _top2_rows(val, iota):
    m1 = jnp.max(val, axis=0, keepdims=True)
    i1 = jnp.min(jnp.where(val == m1, iota, SUBLANES), axis=0, keepdims=True)
    rest = jnp.where(iota == i1, -jnp.inf, val)
    m2 = jnp.max(rest, axis=0, keepdims=True)
    i2 = jnp.min(jnp.where(rest == m2, iota, SUBLANES), axis=0, keepdims=True)
    return m1, m2, i1, i2


def _outproj_kernel(ya_ref, yb_ref, yc_ref, yd_ref, x_ref, mod_ref, g_ref, wo_ref, wr_ref, rb_ref,
                    xo_ref, h_ref, e_ref, gw_ref):
    tm = x_ref.shape[0] // OUTPROJ_SPLIT
    for part in range(OUTPROJ_SPLIT):
        _outproj_rows(slice(part * tm, (part + 1) * tm), ya_ref, yb_ref, yc_ref, yd_ref, x_ref, mod_ref, g_ref,
                      wo_ref, wr_ref, rb_ref, xo_ref, h_ref, e_ref, gw_ref)


def _outproj_rows(rows, ya_ref, yb_ref, yc_ref, yd_ref, x_ref, mod_ref, g_ref, wo_ref, wr_ref, rb_ref,
                  xo_ref, h_ref, e_ref, gw_ref):
    d = x_ref.shape[-1]
    gw = GROUP_W
    acc = None
    for n, y_ref in enumerate((ya_ref, yb_ref, yc_ref, yd_ref)):
        part = _dot(y_ref[rows, :].astype(BF16), wo_ref[n * gw:(n + 1) * gw, :])
        acc = part if acc is None else acc + part
    x = x_ref[rows, :] + mod_ref[:, 2 * d:3 * d] * acc
    xo_ref[rows, :] = x
    y = x * lax.rsqrt(jnp.mean(x * x, axis=-1, keepdims=True) + EPS) * g_ref[...]
    h = y * (1.0 + mod_ref[:, 4 * d:5 * d]) + mod_ref[:, 3 * d:4 * d]
    h_ref[rows, :] = _pack_bf16_halves(h)
    hh, hl = _split(h)
    wh, wl = _split(wr_ref[...])
    logits = _dot_nt(wh, hh) + _dot_nt(wh, hl) + _dot_nt(wl, hh)
    scores = _sigmoid(logits)
    sel = scores + rb_ref[:, rows]
    tm = x.shape[0]
    iota = lax.broadcasted_iota(jnp.int32, (EXPERTS_PER_GROUP, tm), 0)
    best_val, best = None, None
    for grp in range(N_EXPERT_GROUPS):
        m1, m2, _, _ = _top2_rows(sel[grp * EXPERTS_PER_GROUP:(grp + 1) * EXPERTS_PER_GROUP], iota)
        gs = m1 + m2
        if grp == 0:
            best_val, best = gs, jnp.zeros_like(gs, dtype=jnp.int32)
        else:
            upd = gs > best_val
            best_val = jnp.where(upd, gs, best_val)
            best = jnp.where(upd, grp, best)
    sel_in = jnp.zeros((EXPERTS_PER_GROUP, tm), F32)
    sc_in = jnp.zeros((EXPERTS_PER_GROUP, tm), F32)
    for grp in range(N_EXPERT_GROUPS):
        members = slice(grp * EXPERTS_PER_GROUP, (grp + 1) * EXPERTS_PER_GROUP)
        hit = best == grp
        sel_in = jnp.where(hit, sel[members], sel_in)
        sc_in = jnp.where(hit, scores[members], sc_in)
    _, _, i1, i2 = _top2_rows(sel_in, iota)
    w1 = jnp.sum(jnp.where(iota == i1, sc_in, 0.0), axis=0, keepdims=True)
    w2 = jnp.sum(jnp.where(iota == i2, sc_in, 0.0), axis=0, keepdims=True)
    tot = w1 + w2
    e_ref[0:1, rows] = best * EXPERTS_PER_GROUP + i1
    e_ref[1:2, rows] = best * EXPERTS_PER_GROUP + i2
    gw_ref[0:1, rows] = w1 / tot
    gw_ref[1:2, rows] = w2 / tot


def out_projection_router(ys, stream, mod, gain2, w_out, w_router, router_bias, seq):
    n_batch, lc, d = stream.shape
    gw = GROUP_W
    ctx_tile = seq // TM
    wr_t = w_router.T
    rb = jnp.broadcast_to(router_bias.astype(F32)[:, None], (N_EXPERTS, TM))
    row = lambda n: pl.BlockSpec((None, TM, n), lambda b, i: (b, i, 0))
    const = lambda a: pl.BlockSpec(a.shape, lambda b, i: (0,) * a.ndim)
    lane_row = pl.BlockSpec((None, TOP_K, TM), lambda b, i: (b, 0, i))
    return pl.pallas_call(
        _outproj_kernel, name="out_proj_router",
        out_shape=[jax.ShapeDtypeStruct((n_batch, lc, d), F32), jax.ShapeDtypeStruct((n_batch, lc, d // 2), jnp.uint32),
                   jax.ShapeDtypeStruct((n_batch, TOP_K, lc), jnp.int32),
                   jax.ShapeDtypeStruct((n_batch, TOP_K, lc), F32)],
        grid=(n_batch, lc // TM),
        in_specs=[row(gw)] * 4 + [row(d), pl.BlockSpec((None, 1, mod.shape[-1]), _mod_index(n_batch, ctx_tile)),
                                  pl.BlockSpec((1, d), lambda b, i: (0, 0)),
                                  pl.BlockSpec(w_out.shape, lambda b, i: (0, 0), pipeline_mode=pl.Buffered(1)),
                                  const(wr_t), const(rb)],
        out_specs=[row(d), row(d // 2), lane_row, lane_row],
        compiler_params=_cparams(("parallel", "arbitrary")),
    )(*ys, stream, mod, gain2.reshape(1, d), w_out, wr_t, rb)


def _expert_kernel(be_ref, nu_ref, x_ref, wg_ref, wu_ref, wd_ref, o_ref, wg_s, wu_s, wd_s):
    i = pl.program_id(0)
    prev = be_ref[jnp.maximum(i - 1, 0)]
    used = i < nu_ref[0]

    @pl.when(used & ((i == 0) | (be_ref[i] != prev)))
    def _():
        wg_s[...] = wg_ref[...].astype(BF16)
        wu_s[...] = wu_ref[...].astype(BF16)
        wd_s[...] = wd_ref[...].astype(BF16)

    @pl.when(used)
    def _():
        x_lo, x_hi = _unpack_bf16_halves(x_ref[...])
        n = x_lo.shape[1]
        a = _dot(x_lo, wg_s[:n, :]) + _dot(x_hi, wg_s[n:, :])
        u = _dot(x_lo, wu_s[:n, :]) + _dot(x_hi, wu_s[n:, :])
        hb = (a * _sigmoid(a)) * u
        o_ref[...] = _dot(hb.astype(BF16), wd_s[...])

    @pl.when(jnp.logical_not(used))
    def _():
        o_ref[...] = jnp.zeros_like(o_ref)


def expert_ffn(xs, block_e, n_used, wg, wu, wd, layer):
    rows, packed = xs.shape
    n_blocks = rows // MOE_BLOCK
    d, de = wg.shape[-2:]
    return pl.pallas_call(
        _expert_kernel, name="moe_experts",
        out_shape=jax.ShapeDtypeStruct((rows, d), F32),
        grid_spec=pltpu.PrefetchScalarGridSpec(
            num_scalar_prefetch=2, grid=(n_blocks,),
            in_specs=[pl.BlockSpec((MOE_BLOCK, packed), lambda i, be, nu: (i, 0)),
                      pl.BlockSpec((None, None, d, de), lambda i, be, nu: (layer, be[i], 0, 0)),
                      pl.BlockSpec((None, None, d, de), lambda i, be, nu: (layer, be[i], 0, 0)),
                      pl.BlockSpec((None, None, de, d), lambda i, be, nu: (layer, be[i], 0, 0))],
            out_specs=pl.BlockSpec((MOE_BLOCK, d), lambda i, be, nu: (i, 0)),
            scratch_shapes=[pltpu.VMEM((d, de), BF16), pltpu.VMEM((d, de), BF16), pltpu.VMEM((de, d), BF16)]),
        compiler_params=_cparams(("arbitrary",)),
    )(block_e, n_used, xs, wg, wu, wd)


RANK_WIDTHS = (1280, 1024, 768, 512, 256, 128)


def _rank_kernel(e_ref, tri_ref, rank_ref, count_ref, carry_ref):
    first = (pl.program_id(0) == 0) & (pl.program_id(1) == 0)

    @pl.when(first)
    def _():
        carry_ref[...] = jnp.zeros_like(carry_ref)

    width = e_ref.shape[-1]
    expert = lax.broadcasted_iota(jnp.int32, (N_EXPERTS, width), 0)
    carry = carry_ref[...]
    for k in range(TOP_K):
        onehot = jnp.where(expert == e_ref[k:k + 1, :], 1.0, 0.0)
        before = _dot(onehot.astype(BF16), tri_ref[...])
        rank_ref[k:k + 1, :] = jnp.sum(onehot * (before + carry), axis=0, keepdims=True).astype(jnp.int32)
        carry = carry + jnp.sum(onehot, axis=1, keepdims=True)
    carry_ref[...] = carry
    count_ref[...] = jnp.broadcast_to(carry, count_ref.shape)


def expert_ranks(eidx):
    n_batch, top_k, lc = eidx.shape
    width = next(w for w in RANK_WIDTHS if lc % w == 0)
    tri = jnp.asarray(np.triu(np.ones((width, width), np.float32), 1), BF16)
    blk = pl.BlockSpec((None, top_k, width), lambda b, i: (b, 0, i))
    rank, counts = pl.pallas_call(
        _rank_kernel, name="moe_rank",
        out_shape=[jax.ShapeDtypeStruct((n_batch, top_k, lc), jnp.int32),
                   jax.ShapeDtypeStruct((N_EXPERTS, LANES), F32)],
        grid=(n_batch, lc // width),
        in_specs=[blk, pl.BlockSpec(tri.shape, lambda b, i: (0, 0))],
        out_specs=[blk, pl.BlockSpec((N_EXPERTS, LANES), lambda b, i: (0, 0))],
        scratch_shapes=[pltpu.VMEM((N_EXPERTS, 1), F32)],
        compiler_params=_cparams(("arbitrary", "arbitrary")),
    )(eidx, tri)
    return rank, counts[:, 0].astype(jnp.int32)


def _combine_kernel(x_ref, y0_ref, y1_ref, g0_ref, g1_ref, mod_ref, o_ref):
    d = x_ref.shape[-1]
    y = y0_ref[...] * g0_ref[...] + y1_ref[...] * g1_ref[...]
    o_ref[...] = x_ref[...] + mod_ref[:, 5 * d:6 * d] * y


def moe_combine(stream, y0, y1, g0, g1, mod, seq, n_rows):
    n_batch, lc, d = stream.shape
    row = pl.BlockSpec((None, TM, d), lambda b, i: (b, i, 0))
    col = pl.BlockSpec((None, TM, 1), lambda b, i: (b, i, 0))
    return pl.pallas_call(
        _combine_kernel, name="moe_combine",
        out_shape=jax.ShapeDtypeStruct((n_batch, n_rows, d), F32),
        grid=(n_batch, n_rows // TM),
        in_specs=[row, row, row, col, col, pl.BlockSpec((None, 1, mod.shape[-1]), _mod_index(n_batch, seq // TM))],
        out_specs=row,
        compiler_params=_cparams(("parallel", "arbitrary")),
    )(stream, y0, y1, g0, g1, mod)


def moe_ffn(stream, h2, eidx, gates, mod, wg, wu, wd, layer, seq, n_rows):
    n_batch, lc, d = stream.shape
    n_tok = n_batch * lc
    n_assign = n_tok * TOP_K
    rank, counts = expert_ranks(eidx)
    padded = (counts + MOE_BLOCK - 1) // MOE_BLOCK * MOE_BLOCK
    pad_end = jnp.cumsum(padded)
    experts = jnp.arange(N_EXPERTS, dtype=jnp.int32)
    start = jnp.sum(jnp.where(eidx[..., None] == experts, pad_end - padded, 0), axis=-1)
    dest = start + rank
    n_blocks = n_assign // MOE_BLOCK + N_EXPERTS
    tok = jnp.broadcast_to((jnp.arange(n_batch, dtype=jnp.int32) * lc)[:, None, None]
                           + jnp.arange(lc, dtype=jnp.int32)[None, None, :], dest.shape)
    row_tok = jnp.zeros((n_blocks * MOE_BLOCK,), jnp.int32).at[dest.reshape(-1)].set(tok.reshape(-1))
    block_row = jnp.arange(n_blocks, dtype=jnp.int32) * MOE_BLOCK
    block_e = jnp.minimum(jnp.sum((pad_end[None, :] <= block_row[:, None]).astype(jnp.int32), axis=1), N_EXPERTS - 1)
    n_used = (pad_end[-1:] // MOE_BLOCK).astype(jnp.int32)
    xs = h2.reshape(n_tok, h2.shape[-1])[row_tok]
    ys = expert_ffn(xs, block_e, n_used, wg, wu, wd, layer)
    y0 = ys[dest[:, 0].reshape(-1)].reshape(n_batch, lc, d)
    y1 = ys[dest[:, 1].reshape(-1)].reshape(n_batch, lc, d)
    g0 = gates[:, 0].reshape(n_batch, lc, 1)
    g1 = gates[:, 1].reshape(n_batch, lc, 1)
    return moe_combine(stream, y0, y1, g0, g1, mod, seq, n_rows)


def kernel(x, c, ctx, c_ctx, norm1, norm2, w_ada, b_ada, w_in, w_out, rwkv_conv, rwkv_w0, rwkv_w_up, rwkv_a0,
           rwkv_a_up, rwkv_g_up, rwkv_k_k, rwkv_k_a, rwkv_r_k, rwkv_ln_w, rwkv_ln_b, rwkv_v0, rwkv_v_down,
           rwkv_v_up, fourier_w, fourier_b, gmlp_norm, gmlp_ws, gmlp_bs, attn_q_norm, attn_k_norm, attn_sink,
           w_router, router_bias, w_e_gate, w_e_up, w_e_down):
    n_batch, seq, d = x.shape
    depth = w_in.shape[0]
    stream = jnp.concatenate([x, ctx], axis=1)
    c_all = jnp.zeros((SUBLANES, d), F32).at[:n_batch].set(c).at[n_batch].set(c_ctx)
    col_splits = np.cumsum([A_COLS, B_COLS, C_COLS])
    v_first = None
    for l in range(depth):
        mod = ada_modulation(c_all, w_ada, b_ada, l).reshape(SUBLANES, 1, 6 * d)
        w_parts = [w.astype(BF16) for w in jnp.split(w_in[l], col_splits, axis=1)]
        pa, pb, pc, pd = in_projection(stream, mod, norm1[l], w_parts, seq)

        vres = None if l == 0 else (rwkv_v0[l - 1], rwkv_v_down[l - 1], rwkv_v_up[l - 1])
        r, kk, v, k0, wd0, kka0, k1, wd1, kka1, g = rwkv_prepare(
            pa, seq, rwkv_conv[l], rwkv_w0[l], rwkv_w_up[l], rwkv_a0[l], rwkv_a_up[l], rwkv_g_up[l],
            rwkv_k_k[l], rwkv_k_a[l], v_first, vres)
        if l == 0:
            v_first = v
        yf, yb = rwkv_scan(r, kk, v, ((k0, wd0, kka0), (k1, wd1, kka1)), seq)
        y_a = rwkv_readout(yf, yb, r, k0, k1, v, g, rwkv_r_k[l].reshape(-1), rwkv_ln_w[l], rwkv_ln_b[l])

        y_b = fourier_mix(pb, seq, fourier_w[l], fourier_b[l])
        y_c = gmlp_mix(pc, gmlp_norm[l], gmlp_ws[l], gmlp_bs[l])
        q, kv, vv = attention_prepare(pd, seq, attn_q_norm[l], attn_k_norm[l])
        y_d = window_attention(q, kv, vv, attn_sink[l], seq)

        stream, h2, eidx, gates = out_projection_router(
            (y_a, y_b, y_c, y_d), stream, mod, norm2[l], w_out[l].astype(BF16), w_router, router_bias, seq)
        n_rows = seq if l == depth - 1 else stream.shape[1]
        stream = moe_ffn(stream, h2, eidx, gates, mod, w_e_gate, w_e_up, w_e_down, l, seq, n_rows)
    return stream
```

```python
import functools

import numpy as np
import jax
import jax.numpy as jnp
from jax import lax
from jax.experimental import pallas as pl
from jax.experimental.pallas import tpu as pltpu

F32 = jnp.float32
BF16 = jnp.bfloat16

D_MODEL = 2048
DEPTH = 2
GRID_W = 64
EPS = 1e-6
NEG_INF = -1e30
GROUP_W = 512
RWKV_HEAD = 64
RWKV_HEADS = 8
DECAY_LORA = 64
ICLR_LORA = 64
GATE_LORA = 128
VRES_LORA = 32
RWKV_GN_EPS = 64e-5
FOURIER_GROUPS = 4
FOURIER_CH = 128
GMLP_GROUPS = 4
GMLP_CH = 128
GMLP_CHUNK = 128
ATT_HEAD = 64
ATT_Q_HEADS = 8
ATT_KV_HEADS = 2
WINDOW = 128
ATT_BLOCK = 128
ROPE_BASE = 10000.0
N_EXPERTS = 64
N_EXPERT_GROUPS = 8
EXPERTS_PER_GROUP = 8
TOP_K = 2
D_EXPERT = 512
MOE_BLOCK = 128
A_COLS = 1920
B_COLS = 512
C_COLS = 1024
D_COLS = 768

LANES = 128
SUBLANES = 8
TM = 256
SCAN_TB = 128
SCAN_GROUP = SUBLANES
SCAN_TILES = 4
OUTPROJ_SPLIT = 2
VMEM_LIMIT = 56 * 1024 * 1024


def _cparams(sem, vmem=VMEM_LIMIT):
    return pltpu.CompilerParams(dimension_semantics=sem, vmem_limit_bytes=vmem)


def _dot(a, b):
    return jnp.dot(a, b, preferred_element_type=F32)


def _dot_nt(a, b):
    return lax.dot_general(a, b, (((1,), (1,)), ((), ())), preferred_element_type=F32)


def _dot_tn(a, b):
    return lax.dot_general(a, b, (((0,), (0,)), ((), ())), preferred_element_type=F32)


def _split(x):
    hi = x.astype(BF16)
    lo = (x - hi.astype(F32)).astype(BF16)
    return hi, lo


def _pack_bf16_halves(x):
    n = x.shape[1] // 2
    lo = lax.bitcast_convert_type(x[:, :n].astype(BF16).astype(F32), jnp.uint32) >> 16
    hi = lax.bitcast_convert_type(x[:, n:].astype(BF16).astype(F32), jnp.uint32) & jnp.uint32(0xFFFF0000)
    return hi | lo


def _unpack_bf16_halves(w):
    lo = lax.bitcast_convert_type(w << 16, F32).astype(BF16)
    hi = lax.bitcast_convert_type(w & jnp.uint32(0xFFFF0000), F32).astype(BF16)
    return lo, hi


def _dot1(a, b):
    return _dot(a.astype(BF16), b.astype(BF16))


def _segsum(x, seg):
    xh, xl = _split(x)
    return _dot(xh, seg) + _dot(xl, seg)


def _sigmoid(x):
    return 1.0 / (1.0 + jnp.exp(-x))


def _seg_matrix(width, seg):
    i = np.arange(width) // seg
    return jnp.asarray((i[:, None] == i[None, :]).astype(np.float32), dtype=BF16)


def _ada_kernel(c_ref, w_ref, b_ref, o_ref):
    c = c_ref[...]
    a = (c * _sigmoid(c)).astype(BF16)
    o_ref[...] = _dot(a, w_ref[...].astype(BF16)) + b_ref[...]


def ada_modulation(c_all, w_ada, b_ada, layer, tn=1024):
    rows, d = c_all.shape
    n_layers, _, n = w_ada.shape
    return pl.pallas_call(
        _ada_kernel, name="ada_mod",
        out_shape=jax.ShapeDtypeStruct((rows, n), F32),
        grid=(n // tn,),
        in_specs=[pl.BlockSpec((rows, d), lambda j: (0, 0)),
                  pl.BlockSpec((None, d, tn), lambda j: (layer, 0, j)),
                  pl.BlockSpec((None, 1, tn), lambda j: (layer, 0, j))],
        out_specs=pl.BlockSpec((rows, tn), lambda j: (0, j)),
        compiler_params=_cparams(("arbitrary",)),
    )(c_all, w_ada, b_ada.reshape(n_layers, 1, n))


def _inproj_kernel(x_ref, mod_ref, g_ref, wa_ref, wb_ref, wc_ref, wd_ref, pa_ref, pb_ref, pc_ref, pd_ref):
    d = x_ref.shape[-1]
    x = x_ref[...]
    y = x * lax.rsqrt(jnp.mean(x * x, axis=-1, keepdims=True) + EPS) * g_ref[...]
    sh = mod_ref[:, 0:d]
    sc = mod_ref[:, d:2 * d]
    h = (y * (1.0 + sc) + sh).astype(BF16)
    pa_ref[...] = _dot(h, wa_ref[...])
    pb_ref[...] = _dot(h, wb_ref[...])
    pc_ref[...] = _dot(h, wc_ref[...])
    pd_ref[...] = _dot(h, wd_ref[...])


def _mod_index(n_batch, ctx_tile):
    return lambda b, i: (jnp.where(i >= ctx_tile, n_batch, b), 0, 0)


def in_projection(stream, mod, gain, w_parts, seq):
    n_batch, lc, d = stream.shape
    ctx_tile = seq // TM
    resident = lambda w: pl.BlockSpec(w.shape, lambda b, i: (0, 0), pipeline_mode=pl.Buffered(1))
    row = lambda n: pl.BlockSpec((None, TM, n), lambda b, i: (b, i, 0))
    return pl.pallas_call(
        _inproj_kernel, name="in_proj",
        out_shape=[jax.ShapeDtypeStruct((n_batch, lc, w.shape[1]), F32) for w in w_parts],
        grid=(n_batch, lc // TM),
        in_specs=[row(d),
                  pl.BlockSpec((None, 1, mod.shape[-1]), _mod_index(n_batch, ctx_tile)),
                  pl.BlockSpec((1, d), lambda b, i: (0, 0))] + [resident(w) for w in w_parts],
        out_specs=[row(w.shape[1]) for w in w_parts],
        compiler_params=_cparams(("parallel", "arbitrary")),
    )(stream, mod, gain.reshape(1, d), *w_parts)


def _rwkv_prep_kernel(has_vres, ctx_tile, *refs):
    (pa_ref, hp_ref, hn_ref, conv_ref, w0_ref, wup_ref, a0_ref, aup_ref, gup_ref, kk_par_ref, ka_par_ref,
     seg_ref) = refs[:12]
    rest = refs[12:]
    if has_vres:
        vf_ref, v0_ref, vdn_ref, vup_ref = rest[:4]
        rest = rest[4:]
    r_o, kk_o, v_o, k0_o, w0_o, kka0_o, k1_o, w1_o, kka1_o, g_o = rest
    gw = GROUP_W
    x = pa_ref[...]
    tm = x.shape[0]
    i = pl.program_id(1)
    first = (i == 0) | (i == ctx_tile)
    last = (i == ctx_tile - 1) | (i == pl.num_programs(1) - 1)
    halo_prev = jnp.where(first, 0.0, hp_ref[SUBLANES - 1:SUBLANES, :])
    halo_next = jnp.where(last, 0.0, hn_ref[0:1, :])
    row = lax.broadcasted_iota(jnp.int32, x.shape, 0)
    x_prev = jnp.where(row == 0, halo_prev, pltpu.roll(x, 1, 0))
    x_next = jnp.where(row == tm - 1, halo_next, pltpu.roll(x, tm - 1, 0))
    y = x_prev * conv_ref[0:1, :] + x * conv_ref[1:2, :] + x_next * conv_ref[2:3, :]
    r = y[:, 0:gw]
    k = y[:, gw:2 * gw]
    v = y[:, 2 * gw:3 * gw]
    wl = y[:, 3 * gw:3 * gw + 2 * DECAY_LORA]
    al = y[:, 3 * gw + 2 * DECAY_LORA:3 * gw + 2 * DECAY_LORA + 2 * ICLR_LORA]
    gl = y[:, 3 * gw + 2 * DECAY_LORA + 2 * ICLR_LORA:]
    if has_vres:
        mix = _sigmoid(v0_ref[...] + _dot1(_dot1(v, vdn_ref[...]), vup_ref[...]))
        v = v + (vf_ref[...] - v) * mix
    kk = k * kk_par_ref[...]
    nrm = jnp.maximum(jnp.sqrt(_segsum(kk * kk, seg_ref[...])), 1e-12)
    kk = kk / nrm
    r_o[...] = r
    kk_o[...] = kk
    v_o[...] = v
    tanh_wl = jnp.tanh(wl)
    for d, (k_o, w_o, kka_o) in enumerate(((k0_o, w0_o, kka0_o), (k1_o, w1_o, kka1_o))):
        w_raw = w0_ref[d:d + 1, :] + _dot1(tanh_wl, wup_ref[d])
        z = -w_raw
        softplus = jnp.maximum(z, 0.0) + jnp.log(1.0 + jnp.exp(-jnp.abs(z)))
        w_o[...] = jnp.exp(-jnp.exp(-softplus - 0.5))
        a = _sigmoid(a0_ref[d:d + 1, :] + _dot1(al, aup_ref[d]))
        k_o[...] = k * (1.0 + (a - 1.0) * ka_par_ref[...])
        kka_o[...] = kk * a
    g_o[...] = _dot1(_sigmoid(gl), gup_ref[...])


def rwkv_prepare(pa, seq, conv_w, w0, w_up, a0, a_up, g_up, k_k, k_a, v_first=None, vres=None):
    n_batch, lc, c = pa.shape
    gw = GROUP_W
    sub_per_tile = TM // SUBLANES
    n_sub = lc // SUBLANES
    pad_rows = lambda w, d, n: jnp.zeros((2 * n, gw), F32).at[d * n:(d + 1) * n].set(w)
    wup = jnp.stack([pad_rows(w_up[d], d, DECAY_LORA) for d in range(2)])
    aup = jnp.stack([pad_rows(a_up[d], d, ICLR_LORA) for d in range(2)])
    seg = _seg_matrix(gw, RWKV_HEAD)
    const = lambda a: pl.BlockSpec(a.shape, lambda b, i: (0,) * a.ndim)
    row = lambda n: pl.BlockSpec((None, TM, n), lambda b, i: (b, i, 0))
    halo_prev = pl.BlockSpec((None, SUBLANES, c), lambda b, i: (b, jnp.maximum(i * sub_per_tile - 1, 0), 0))
    halo_next = pl.BlockSpec((None, SUBLANES, c), lambda b, i: (b, jnp.minimum((i + 1) * sub_per_tile, n_sub - 1), 0))
    args = [pa, pa, pa, conv_w, w0, wup, a0, aup, g_up, k_k.reshape(1, gw), k_a.reshape(1, gw), seg]
    specs = [row(c), halo_prev, halo_next] + [const(a) for a in args[3:]]
    has_vres = vres is not None
    if has_vres:
        v0, v_down, v_up = vres
        vdn = jnp.zeros((gw, LANES), F32).at[:, :VRES_LORA].set(v_down)
        vup = jnp.zeros((LANES, gw), F32).at[:VRES_LORA].set(v_up)
        extra = [v_first, v0.reshape(1, gw), vdn, vup]
        args += extra
        specs += [row(gw)] + [const(a) for a in extra[1:]]
    return pl.pallas_call(
        functools.partial(_rwkv_prep_kernel, has_vres, seq // TM), name="rwkv_prep",
        out_shape=[jax.ShapeDtypeStruct((n_batch, lc, gw), F32)] * 10,
        grid=(n_batch, lc // TM),
        in_specs=specs,
        out_specs=[row(gw)] * 10,
        compiler_params=_cparams(("parallel", "arbitrary")),
    )(*args)


def _scan_kernel(n_batch, rf, kkf, vf, kf, wf, kaf, rb, kkb, vb, kb, wb, kab, bsel_ref, segbd_ref, ysel_ref,
                 yf_ref, yb_ref, s_ref, vk_ref, yt_ref):
    tb = rf.shape[1]
    n_pairs = GROUP_W // LANES
    half = RWKV_HEAD

    @pl.when(pl.program_id(0) == 0)
    def _():
        s_ref[...] = jnp.zeros_like(s_ref)
        yt_ref[...] = jnp.zeros_like(yt_ref)

    segbd = segbd_ref[...]
    dirs = ((rf, kkf, vf, kf, wf, kaf, yf_ref), (rb, kkb, vb, kb, wb, kab, yb_ref))
    chains = [(d, b, p) for d in range(2) for b in range(n_batch) for p in range(n_pairs)]
    n_ch = len(chains)
    lane = lax.broadcasted_iota(jnp.int32, (RWKV_HEAD, LANES), 1) % half

    n_sets = 2
    per_set = n_ch // n_sets
    sets = [list(range(si * per_set, (si + 1) * per_set)) for si in range(n_sets)]

    def side_by_side(parts):
        return jnp.concatenate([jnp.concatenate(parts[i:i + 2], axis=1) for i in range(0, per_set, 2)], axis=0)

    def piece(full, n):
        return full[(n // 2) * RWKV_HEAD:(n // 2 + 1) * RWKV_HEAD, (n % 2) * LANES:(n % 2 + 1) * LANES]

    tile_w = SCAN_GROUP * LANES

    def group(g, carry):
        bases = [(pl.multiple_of((g * SCAN_TILES + u) * SCAN_GROUP, SCAN_GROUP),
                  pl.multiple_of(tb - SCAN_GROUP - (g * SCAN_TILES + u) * SCAN_GROUP, SCAN_GROUP))
                 for u in range(SCAN_TILES)]

        def row(which, ci, u, j):
            d, b, p = chains[ci]
            t = j if d == 0 else SCAN_GROUP - 1 - j
            return dirs[d][which][b, pl.ds(bases[u][d], SCAN_GROUP), p * LANES:(p + 1) * LANES][t:t + 1]

        def sk_dot(states_bf, cs, u, j):
            return _dot(side_by_side([states_bf[n] * row(1, ci, u, j).astype(BF16) for n, ci in enumerate(cs)]),
                        segbd)

        def outer_products(cs, u):
            for ci in cs:
                d, b, p = chains[ci]
                v8 = dirs[d][2][b, pl.ds(bases[u][d], SCAN_GROUP), p * LANES:(p + 1) * LANES]
                k8 = dirs[d][3][b, pl.ds(bases[u][d], SCAN_GROUP), p * LANES:(p + 1) * LANES]
                vt = jnp.concatenate([v8[:, :RWKV_HEAD], v8[:, RWKV_HEAD:]], axis=0).T.astype(BF16)
                kmat = jnp.concatenate([bsel_ref[t] * k8[t:t + 1] for t in range(SCAN_GROUP)],
                                       axis=1).astype(BF16)
                vk_ref[ci, :, u * tile_w:(u + 1) * tile_w] = _dot(vt, kmat)

        sk = []
        for cs in sets:
            outer_products(cs, 0)
            sk.append(sk_dot([s_ref[ci].astype(BF16) for ci in cs], cs, 0, 0))
        for u in range(SCAN_TILES):
            y_tile = [None] * n_sets
            for j in range(SCAN_GROUP):
                for si, cs in enumerate(sets):
                    d = chains[cs[0]][0]
                    t = j if d == 0 else SCAN_GROUP - 1 - j
                    states_bf = []
                    for n, ci in enumerate(cs):
                        s = (s_ref[ci] * row(4, ci, u, j) - piece(sk[si], n) * row(5, ci, u, j)
                             + vk_ref[ci, :, u * tile_w + t * LANES:u * tile_w + (t + 1) * LANES])
                        s_ref[ci] = s
                        states_bf.append(s.astype(BF16))
                    y_lhs = side_by_side([s * row(0, ci, u, j).astype(BF16) for s, ci in zip(states_bf, cs)])
                    y_step = _dot(y_lhs, ysel_ref[t])
                    y_tile[si] = y_step if j == 0 else y_tile[si] + y_step
                    if j + 1 < SCAN_GROUP:
                        sk[si] = sk_dot(states_bf, cs, u, j + 1)
                    elif u + 1 < SCAN_TILES:
                        sk[si] = sk_dot(states_bf, cs, u + 1, 0)
                    if j == SCAN_GROUP // 2 and u + 1 < SCAN_TILES:
                        outer_products(cs, u + 1)
            for si, cs in enumerate(sets):
                d = chains[cs[0]][0]
                first = bases[u][d] % half
                for n, ci in enumerate(cs):
                    moved = pltpu.roll(piece(y_tile[si], n), first, 1)
                    yt_ref[ci] = jnp.where(lane // SCAN_GROUP == first // SCAN_GROUP, moved, yt_ref[ci])
        return carry

    def flush(first_row):
        for ci, (d, b, p) in enumerate(chains):
            t = yt_ref[ci].T
            rows = slice(first_row[d], first_row[d] + half)
            dirs[d][6][b, rows, p * LANES:(p + 1) * LANES] = jnp.concatenate([t[:half], t[half:]], axis=1)

    groups_per_flush = half // (SCAN_GROUP * SCAN_TILES)
    for part in range(tb // half):
        lax.fori_loop(part * groups_per_flush, (part + 1) * groups_per_flush, group, 0)
        flush((part * half, tb - (part + 1) * half))


def _scan_selectors():
    bsel = np.zeros((SCAN_GROUP, 2 * SCAN_GROUP, LANES), np.float32)
    for h in range(2):
        for j in range(SCAN_GROUP):
            bsel[j, h * SCAN_GROUP + j, h * RWKV_HEAD:(h + 1) * RWKV_HEAD] = 1.0
    block = np.arange(2 * LANES) // RWKV_HEAD
    ysel = np.zeros((SCAN_GROUP, 2 * LANES, 2 * LANES), np.float32)
    for t in range(SCAN_GROUP):
        ysel[t] = (block[:, None] == block[None, :]) & ((np.arange(2 * LANES) % RWKV_HEAD) == t)[None, :]
    return jnp.asarray(bsel, F32), _seg_matrix(2 * LANES, RWKV_HEAD), jnp.asarray(ysel, BF16)


def rwkv_scan(r, kk, v, dirs, seq):
    n_batch, lc, gw = r.shape
    tb = SCAN_TB
    nx = seq // tb
    nc = (lc - seq) // tb
    n_chains = 2 * n_batch * (gw // LANES)
    fwd = lambda i: (0, jnp.where(i < nc, nx + i, i - nc), 0)
    bwd = lambda i: (0, nx + nc - 1 - i, 0)
    blk = lambda m: pl.BlockSpec((n_batch, tb, gw), m)
    const = lambda a: pl.BlockSpec(a.shape, lambda i: (0,) * a.ndim)
    bsel, segbd, ysel = _scan_selectors()
    return pl.pallas_call(
        functools.partial(_scan_kernel, n_batch), name="rwkv_scan",
        out_shape=[jax.ShapeDtypeStruct((n_batch, lc, gw), F32)] * 2,
        grid=(nx + nc,),
        in_specs=[blk(fwd)] * 6 + [blk(bwd)] * 6 + [const(bsel), const(segbd), const(ysel)],
        out_specs=[blk(fwd), blk(bwd)],
        scratch_shapes=[pltpu.VMEM((n_chains, RWKV_HEAD, LANES), F32),
                        pltpu.VMEM((n_chains, RWKV_HEAD, SCAN_TILES * SCAN_GROUP * LANES), F32),
                        pltpu.VMEM((n_chains, RWKV_HEAD, LANES), F32)],
        compiler_params=_cparams(("arbitrary",)),
    )(r, kk, v, *dirs[0], r, kk, v, *dirs[1], bsel, segbd, ysel)


def _rwkv_readout_kernel(yf_ref, yb_ref, r_ref, k0_ref, k1_ref, v_ref, g_ref, rk_ref, lnw_ref, lnb_ref, seg_ref,
                         o_ref):
    seg = seg_ref[...]
    inv = 1.0 / RWKV_HEAD
    y = yf_ref[...] + yb_ref[...]
    mu = _segsum(y, seg) * inv
    dlt = y - mu
    var = _segsum(dlt * dlt, seg) * inv
    yn = dlt * lax.rsqrt(var + RWKV_GN_EPS) * lnw_ref[...] + lnb_ref[...]
    kbar = 0.5 * (k0_ref[...] + k1_ref[...])
    bonus = _segsum(r_ref[...] * kbar * rk_ref[...], seg) * v_ref[...]
    o_ref[...] = ((yn + bonus) * g_ref[...]).astype(o_ref.dtype)


def rwkv_readout(yf, yb, r, k0, k1, v, g, r_k, ln_w, ln_b):
    n_batch, lc, gw = r.shape
    seg = _seg_matrix(gw, RWKV_HEAD)
    row = pl.BlockSpec((None, TM, gw), lambda b, i: (b, i, 0))
    const = lambda a: pl.BlockSpec(a.shape, lambda b, i: (0,) * a.ndim)
    params = [r_k.reshape(1, gw), ln_w.reshape(1, gw), ln_b.reshape(1, gw), seg]
    return pl.pallas_call(
        _rwkv_readout_kernel, name="rwkv_readout",
        out_shape=jax.ShapeDtypeStruct((n_batch, lc, gw), BF16),
        grid=(n_batch, lc // TM),
        in_specs=[row] * 7 + [const(a) for a in params],
        out_specs=row,
        compiler_params=_cparams(("parallel", "arbitrary")),
    )(yf, yb, r, k0, k1, v, g, *params)


def _dft_tables(n):
    k = np.arange(n)
    ang = 2.0 * np.pi * ((k[:, None] * k[None, :]) % n) / n
    return np.cos(ang), np.sin(ang)


def _fft_a_kernel(nb, f_ref, gr_ref, gi_ref, zr_ref, zi_ref):
    n1 = gr_ref.shape[1]
    n2_total = f_ref.shape[0] // n1
    j = pl.program_id(2)
    for i in range(nb):
        rows = f_ref[pl.ds(j * nb + i, n1, stride=n2_total), :].astype(BF16)
        zr_ref[i * n1:(i + 1) * n1, :] = _dot(gr_ref[i], rows)
        zi_ref[i * n1:(i + 1) * n1, :] = _dot(gi_ref[i], rows)


def _fft_b_kernel(kb, scale, zr_ref, zi_ref, fc_ref, fs_ref, cc_ref, cs_ref, w_ref, b_ref, o_ref):
    n2 = fc_ref.shape[0]
    n1 = zr_ref.shape[0] // n2
    j = pl.program_id(2)
    fc, fs = fc_ref[...], fs_ref[...]
    for i in range(kb):
        k1 = j * kb + i
        zr = zr_ref[pl.ds(k1, n2, stride=n1), :].astype(BF16)
        zi = zi_ref[pl.ds(k1, n2, stride=n1), :].astype(BF16)
        xr = _dot(fc, zr) + _dot(fs, zi)
        xi = _dot(fc, zi) - _dot(fs, zr)
        re = (_dot(xr.astype(BF16), cc_ref[...]) + _dot(xi.astype(BF16), cs_ref[...])) * scale
        o_ref[pl.ds(k1, n2, stride=n1), :] = _dot(re.astype(BF16), w_ref[...].astype(BF16)) + b_ref[...]


def _dft_dense_kernel(scale, f_ref, fc_ref, fs_ref, cc_ref, cs_ref, w_ref, b_ref, o_ref):
    f = f_ref[...].astype(BF16)
    xr = _dot(fc_ref[...], f)
    xi = -_dot(fs_ref[...], f)
    re = (_dot(xr.astype(BF16), cc_ref[...]) + _dot(xi.astype(BF16), cs_ref[...])) * scale
    o_ref[...] = _dot(re.astype(BF16), w_ref[...].astype(BF16)) + b_ref[...]


def fourier_mix(pb, seq, w, bias):
    n_batch, lc, gw = pb.shape
    ctx = lc - seq
    ch = FOURIER_CH
    groups = FOURIER_GROUPS
    n2 = LANES
    n1 = seq // n2
    cc, cs = (jnp.asarray(t, BF16) for t in _dft_tables(ch))
    bias3 = bias.reshape(groups, 1, ch)

    k1 = np.arange(n1)[None, :, None]
    m1 = np.arange(n1)[None, None, :]
    m2 = np.arange(n2)[:, None, None]
    ang = 2.0 * np.pi * ((k1 * (n2 * m1 + m2)) % seq) / seq
    g_re = jnp.asarray(np.cos(ang), BF16)
    g_im = jnp.asarray(-np.sin(ang), BF16)
    nb = min(16, n2)
    slab = pl.BlockSpec((None, seq, ch), lambda b, g, j: (b, 0, g))
    gspec = pl.BlockSpec((nb, n1, n1), lambda b, g, j: (j, 0, 0))
    zspec = pl.BlockSpec((None, None, nb * n1, ch), lambda b, g, j: (b, g, j, 0))
    zr, zi = pl.pallas_call(
        functools.partial(_fft_a_kernel, nb), name="fft_stage_a",
        out_shape=[jax.ShapeDtypeStruct((n_batch, groups, seq, ch), F32)] * 2,
        grid=(n_batch, groups, n2 // nb),
        in_specs=[slab, gspec, gspec],
        out_specs=[zspec, zspec],
        compiler_params=_cparams(("parallel", "parallel", "arbitrary")),
    )(pb, g_re, g_im)

    fc, fs = (jnp.asarray(t, BF16) for t in _dft_tables(n2))
    kb = min(16, n1)
    scale = 1.0 / np.sqrt(float(seq) * ch)
    zslab = pl.BlockSpec((None, None, seq, ch), lambda b, g, j: (b, g, 0, 0), pipeline_mode=pl.Buffered(1))
    const = lambda a: pl.BlockSpec(a.shape, lambda b, g, j: (0,) * a.ndim)
    wspec = pl.BlockSpec((None, ch, ch), lambda b, g, j: (g, 0, 0))
    bspec = pl.BlockSpec((None, 1, ch), lambda b, g, j: (g, 0, 0))
    y_x = pl.pallas_call(
        functools.partial(_fft_b_kernel, kb, scale), name="fft_stage_b",
        out_shape=jax.ShapeDtypeStruct((n_batch, seq, gw), F32),
        grid=(n_batch, groups, n1 // kb),
        in_specs=[zslab, zslab, const(fc), const(fs), const(cc), const(cs), wspec, bspec],
        out_specs=pl.BlockSpec((None, seq, ch), lambda b, g, j: (b, 0, g)),
        compiler_params=_cparams(("parallel", "parallel", "arbitrary")),
    )(zr, zi, fc, fs, cc, cs, w, bias3)

    fcc, fsc = (jnp.asarray(t, BF16) for t in _dft_tables(ctx))
    const2 = lambda a: pl.BlockSpec(a.shape, lambda b, g: (0,) * a.ndim)
    ctx_tile = seq // ctx
    y_c = pl.pallas_call(
        functools.partial(_dft_dense_kernel, 1.0 / np.sqrt(float(ctx) * ch)), name="dft_ctx",
        out_shape=jax.ShapeDtypeStruct((n_batch, ctx, gw), F32),
        grid=(n_batch, groups),
        in_specs=[pl.BlockSpec((None, ctx, ch), lambda b, g: (b, ctx_tile, g)),
                  const2(fcc), const2(fsc), const2(cc), const2(cs),
                  pl.BlockSpec((None, ch, ch), lambda b, g: (g, 0, 0)),
                  pl.BlockSpec((None, 1, ch), lambda b, g: (g, 0, 0))],
        out_specs=pl.BlockSpec((None, ctx, ch), lambda b, g: (b, 0, g)),
        compiler_params=_cparams(("parallel", "arbitrary")),
    )(pb, fcc, fsc, cc, cs, w, bias3)
    return jnp.concatenate([y_x, y_c], axis=1)


def _gmlp_kernel(pc_ref, norm_ref, ws_ref, bs_ref, o_ref):
    gw = GROUP_W
    x = pc_ref[...]
    z = 0.5 * x * (1.0 + jnp.tanh(0.7978845608028654 * (x + 0.044715 * (x * x * x))))
    tm = x.shape[0]
    for g in range(GMLP_GROUPS):
        u = z[:, g * GMLP_CH:(g + 1) * GMLP_CH]
        v = z[:, gw + g * GMLP_CH:gw + (g + 1) * GMLP_CH]
        v = v * lax.rsqrt(jnp.mean(v * v, axis=-1, keepdims=True) + EPS) * norm_ref[g:g + 1, :]
        v = v.astype(BF16)
        ws = ws_ref[g].astype(BF16)
        for c in range(tm // GMLP_CHUNK):
            rows = slice(c * GMLP_CHUNK, (c + 1) * GMLP_CHUNK)
            f = _dot(ws, v[rows]) + bs_ref[g]
            o_ref[rows, g * GMLP_CH:(g + 1) * GMLP_CH] = (u[rows] * f).astype(o_ref.dtype)


def gmlp_mix(pc, norm_g, ws, bs):
    n_batch, lc, c = pc.shape
    bs_b = jnp.broadcast_to(bs[:, :, None], bs.shape + (GMLP_CH,))
    const = lambda a: pl.BlockSpec(a.shape, lambda b, i: (0,) * a.ndim)
    return pl.pallas_call(
        _gmlp_kernel, name="gmlp",
        out_shape=jax.ShapeDtypeStruct((n_batch, lc, GROUP_W), BF16),
        grid=(n_batch, lc // TM),
        in_specs=[pl.BlockSpec((None, TM, c), lambda b, i: (b, i, 0)), const(norm_g), const(ws), const(bs_b)],
        out_specs=pl.BlockSpec((None, TM, GROUP_W), lambda b, i: (b, i, 0)),
        compiler_params=_cparams(("parallel", "arbitrary")),
    )(pc, norm_g, ws, bs_b)


def _rope_tables(seq, ctx):
    half = ATT_HEAD // 4
    inv_freq = ROPE_BASE ** (-np.arange(half, dtype=np.float64) / half)
    pos = np.arange(seq)
    ang_r = (pos // GRID_W)[:, None] * inv_freq[None, :]
    ang_c = (pos % GRID_W)[:, None] * inv_freq[None, :]
    cos = np.concatenate([np.cos(ang_r)] * 2 + [np.cos(ang_c)] * 2, axis=1)
    sin = np.concatenate([-np.sin(ang_r), np.sin(ang_r), -np.sin(ang_c), np.sin(ang_c)], axis=1)
    cos = np.concatenate([cos, np.ones((ctx, ATT_HEAD))], axis=0)
    sin = np.concatenate([sin, np.zeros((ctx, ATT_HEAD))], axis=0)
    return jnp.asarray(np.tile(cos, (1, 2)), F32), jnp.asarray(np.tile(sin, (1, 2)), F32)


def _rope(t, cos, sin):
    n = t.shape[1]
    q = ATT_HEAD // 4
    lane = lax.broadcasted_iota(jnp.int32, t.shape, 1)
    swapped = jnp.where((lane % (2 * q)) < q, pltpu.roll(t, n - q, 1), pltpu.roll(t, q, 1))
    return t * cos + swapped * sin


def _attn_prep_kernel(pd_ref, cos_ref, sin_ref, qg_ref, kg_ref, seg_ref, q_o, k_o, v_o):
    gw = GROUP_W
    kvw = ATT_KV_HEADS * ATT_HEAD
    seg = seg_ref[...]
    inv = 1.0 / ATT_HEAD
    cos, sin = cos_ref[...], sin_ref[...]
    q = pd_ref[:, 0:gw]
    q = q * lax.rsqrt(_segsum(q * q, seg) * inv + EPS) * qg_ref[...]
    q = _rope(q, jnp.concatenate([cos] * (gw // kvw), axis=1), jnp.concatenate([sin] * (gw // kvw), axis=1))
    q_o[...] = (q * (ATT_HEAD ** -0.5)).astype(q_o.dtype)
    k = pd_ref[:, gw:gw + kvw]
    k = k * lax.rsqrt(_segsum(k * k, seg[:kvw, :kvw]) * inv + EPS) * kg_ref[...]
    k = _rope(k, cos, sin)
    v = pd_ref[:, gw + kvw:gw + 2 * kvw]
    lane = lax.broadcasted_iota(jnp.int32, k.shape, 1)

    def variants(t):
        h0 = jnp.where(lane < ATT_HEAD, t, 0.0)
        h1 = jnp.where(lane >= ATT_HEAD, t, 0.0)
        return jnp.concatenate([h0, pltpu.roll(h0, ATT_HEAD, 1), pltpu.roll(h1, ATT_HEAD, 1), h1], axis=1)

    k_o[...] = variants(k).astype(k_o.dtype)
    v_o[...] = variants(v).astype(v_o.dtype)


def attention_prepare(pd, seq, q_gain, k_gain):
    n_batch, lc, c = pd.shape
    gw = GROUP_W
    kvw = ATT_KV_HEADS * ATT_HEAD
    cos, sin = _rope_tables(seq, lc - seq)
    seg = _seg_matrix(gw, ATT_HEAD)
    qg = jnp.tile(q_gain, ATT_Q_HEADS).reshape(1, gw)
    kg = jnp.tile(k_gain, ATT_KV_HEADS).reshape(1, kvw)
    const = lambda a: pl.BlockSpec(a.shape, lambda b, i: (0,) * a.ndim)
    tab = pl.BlockSpec((TM, kvw), lambda b, i: (i, 0))
    row = lambda n: pl.BlockSpec((None, TM, n), lambda b, i: (b, i, 0))
    return pl.pallas_call(
        _attn_prep_kernel, name="attn_prep",
        out_shape=[jax.ShapeDtypeStruct((n_batch, lc, gw), BF16)] * 3,
        grid=(n_batch, lc // TM),
        in_specs=[row(c), tab, tab, const(qg), const(kg), const(seg)],
        out_specs=[row(gw)] * 3,
        compiler_params=_cparams(("parallel", "arbitrary")),
    )(pd, cos, sin, qg, kg, seg)


def _attn_kernel(nb, sink_ref, q_ref, kp_ref, kc_ref, kn_ref, kx_ref, vp_ref, vc_ref, vn_ref, vx_ref, o_ref):
    i = pl.program_id(1)
    blk = q_ref.shape[0]
    is_lat = i < nb
    rowi = lax.broadcasted_iota(jnp.int32, (blk, blk), 0)
    coli = lax.broadcasted_iota(jnp.int32, (blk, blk), 1)
    m_prev = (coli >= rowi) & is_lat & (i >= 1)
    m_cur = jnp.broadcast_to(is_lat, (blk, blk))
    m_next = (coli <= rowi) & (i < nb - 1)
    for p in range(ATT_Q_HEADS // 2):
        q = q_ref[:, p * LANES:(p + 1) * LANES]
        acc = None
        for par in range(2):
            h = 2 * p + par
            g = h // (ATT_Q_HEADS // ATT_KV_HEADS)
            col = slice((2 * g + par) * LANES, (2 * g + par + 1) * LANES)
            sink = sink_ref[h]
            s1 = jnp.where(m_prev, _dot_nt(q, kp_ref[:, col]), NEG_INF)
            s2 = jnp.where(m_cur, _dot_nt(q, kc_ref[:, col]), NEG_INF)
            s3 = jnp.where(m_next, _dot_nt(q, kn_ref[:, col]), NEG_INF)
            sx = _dot_nt(q, kx_ref[:, col])
            m = jnp.maximum(jnp.maximum(jnp.max(jnp.maximum(jnp.maximum(s1, s2), s3), axis=-1, keepdims=True),
                                        jnp.max(sx, axis=-1, keepdims=True)), sink)
            p1, p2, p3, px = jnp.exp(s1 - m), jnp.exp(s2 - m), jnp.exp(s3 - m), jnp.exp(sx - m)
            den = (jnp.sum(p1 + p2 + p3, axis=-1, keepdims=True) + jnp.sum(px, axis=-1, keepdims=True)
                   + jnp.exp(sink - m))
            o = (_dot(p1.astype(BF16), vp_ref[:, col]) + _dot(p2.astype(BF16), vc_ref[:, col])
                 + _dot(p3.astype(BF16), vn_ref[:, col]) + _dot(px.astype(BF16), vx_ref[:, col])) / den
            acc = o if acc is None else acc + o
        o_ref[:, p * LANES:(p + 1) * LANES] = acc.astype(o_ref.dtype)


def window_attention(q, kv, vv, sink, seq):
    n_batch, lc, gw = q.shape
    blk = ATT_BLOCK
    nb = seq // blk
    ctx = lc - seq
    wide = kv.shape[-1]
    qspec = pl.BlockSpec((None, blk, gw), lambda b, i, s: (b, i, 0))
    prev = pl.BlockSpec((None, blk, wide), lambda b, i, s: (b, jnp.clip(i - 1, 0, nb - 1), 0))
    cur = pl.BlockSpec((None, blk, wide), lambda b, i, s: (b, jnp.minimum(i, nb - 1), 0))
    nxt = pl.BlockSpec((None, blk, wide), lambda b, i, s: (b, jnp.clip(i + 1, 0, nb - 1), 0))
    cx = pl.BlockSpec((None, ctx, wide), lambda b, i, s: (b, seq // ctx, 0))
    return pl.pallas_call(
        functools.partial(_attn_kernel, nb), name="window_attn",
        out_shape=jax.ShapeDtypeStruct((n_batch, lc, gw), BF16),
        grid_spec=pltpu.PrefetchScalarGridSpec(
            num_scalar_prefetch=1, grid=(n_batch, lc // blk),
            in_specs=[qspec, prev, cur, nxt, cx, prev, cur, nxt, cx],
            out_specs=qspec),
        compiler_params=_cparams(("parallel", "arbitrary")),
    )(sink.astype(F32), q, kv, kv, kv, kv, vv, vv, vv, vv)


def _top2_rows(val, iota):
    m1 = jnp.max(val, axis=0, keepdims=True)
    i1 = jnp.min(jnp.where(val == m1, iota, SUBLANES), axis=0, keepdims=True)
    rest = jnp.where(iota == i1, -jnp.inf, val)
    m2 = jnp.max(rest, axis=0, keepdims=True)
    i2 = jnp.min(jnp.where(rest == m2, iota, SUBLANES), axis=0, keepdims=True)
    return m1, m2, i1, i2


def _outproj_kernel(ya_ref, yb_ref, yc_ref, yd_ref, x_ref, mod_ref, g_ref, wo_ref, wr_ref, rb_ref,
                    xo_ref, h_ref, e_ref, gw_ref):
    tm = x_ref.shape[0] // OUTPROJ_SPLIT
    for part in range(OUTPROJ_SPLIT):
        _outproj_rows(slice(part * tm, (part + 1) * tm), ya_ref, yb_ref, yc_ref, yd_ref, x_ref, mod_ref, g_ref,
                      wo_ref, wr_ref, rb_ref, xo_ref, h_ref, e_ref, gw_ref)


def _outproj_rows(rows, ya_ref, yb_ref, yc_ref, yd_ref, x_ref, mod_ref, g_ref, wo_ref, wr_ref, rb_ref,
                  xo_ref, h_ref, e_ref, gw_ref):
    d = x_ref.shape[-1]
    gw = GROUP_W
    acc = None
    for n, y_ref in enumerate((ya_ref, yb_ref, yc_ref, yd_ref)):
        part = _dot(y_ref[rows, :].astype(BF16), wo_ref[n * gw:(n + 1) * gw, :])
        acc = part if acc is None else acc + part
    x = x_ref[rows, :] + mod_ref[:, 2 * d:3 * d] * acc
    xo_ref[rows, :] = x
    y = x * lax.rsqrt(jnp.mean(x * x, axis=-1, keepdims=True) + EPS) * g_ref[...]
    h = y * (1.0 + mod_ref[:, 4 * d:5 * d]) + mod_ref[:, 3 * d:4 * d]
    h_ref[rows, :] = _pack_bf16_halves(h)
    hh, hl = _split(h)
    wh, wl = _split(wr_ref[...])
    logits = _dot_nt(wh, hh) + _dot_nt(wh, hl) + _dot_nt(wl, hh)
    scores = _sigmoid(logits)
    sel = scores + rb_ref[:, rows]
    tm = x.shape[0]
    iota = lax.broadcasted_iota(jnp.int32, (EXPERTS_PER_GROUP, tm), 0)
    best_val, best = None, None
    for grp in range(N_EXPERT_GROUPS):
        m1, m2, _, _ = _top2_rows(sel[grp * EXPERTS_PER_GROUP:(grp + 1) * EXPERTS_PER_GROUP], iota)
        gs = m1 + m2
        if grp == 0:
            best_val, best = gs, jnp.zeros_like(gs, dtype=jnp.int32)
        else:
            upd = gs > best_val
            best_val = jnp.where(upd, gs, best_val)
            best = jnp.where(upd, grp, best)
    sel_in = jnp.zeros((EXPERTS_PER_GROUP, tm), F32)
    sc_in = jnp.zeros((EXPERTS_PER_GROUP, tm), F32)
    for grp in range(N_EXPERT_GROUPS):
        members = slice(grp * EXPERTS_PER_GROUP, (grp + 1) * EXPERTS_PER_GROUP)
        hit = best == grp
        sel_in = jnp.where(hit, sel[members], sel_in)
        sc_in = jnp.where(hit, scores[members], sc_in)
    _, _, i1, i2 = _top2_rows(sel_in, iota)
    w1 = jnp.sum(jnp.where(iota == i1, sc_in, 0.0), axis=0, keepdims=True)
    w2 = jnp.sum(jnp.where(iota == i2, sc_in, 0.0), axis=0, keepdims=True)
    tot = w1 + w2
    e_ref[0:1, rows] = best * EXPERTS_PER_GROUP + i1
    e_ref[1:2, rows] = best * EXPERTS_PER_GROUP + i2
    gw_ref[0:1, rows] = w1 / tot
    gw_ref[1:2, rows] = w2 / tot


def out_projection_router(ys, stream, mod, gain2, w_out, w_router, router_bias, seq):
    n_batch, lc, d = stream.shape
    gw = GROUP_W
    ctx_tile = seq // TM
    wr_t = w_router.T
    rb = jnp.broadcast_to(router_bias.astype(F32)[:, None], (N_EXPERTS, TM))
    row = lambda n: pl.BlockSpec((None, TM, n), lambda b, i: (b, i, 0))
    const = lambda a: pl.BlockSpec(a.shape, lambda b, i: (0,) * a.ndim)
    lane_row = pl.BlockSpec((None, TOP_K, TM), lambda b, i: (b, 0, i))
    return pl.pallas_call(
        _outproj_kernel, name="out_proj_router",
        out_shape=[jax.ShapeDtypeStruct((n_batch, lc, d), F32), jax.ShapeDtypeStruct((n_batch, lc, d // 2), jnp.uint32),
                   jax.ShapeDtypeStruct((n_batch, TOP_K, lc), jnp.int32),
                   jax.ShapeDtypeStruct((n_batch, TOP_K, lc), F32)],
        grid=(n_batch, lc // TM),
        in_specs=[row(gw)] * 4 + [row(d), pl.BlockSpec((None, 1, mod.shape[-1]), _mod_index(n_batch, ctx_tile)),
                                  pl.BlockSpec((1, d), lambda b, i: (0, 0)),
                                  pl.BlockSpec(w_out.shape, lambda b, i: (0, 0), pipeline_mode=pl.Buffered(1)),
                                  const(wr_t), const(rb)],
        out_specs=[row(d), row(d // 2), lane_row, lane_row],
        compiler_params=_cparams(("parallel", "arbitrary")),
    )(*ys, stream, mod, gain2.reshape(1, d), w_out, wr_t, rb)


def _expert_kernel(be_ref, nu_ref, x_ref, wg_ref, wu_ref, wd_ref, o_ref, wg_s, wu_s, wd_s):
    i = pl.program_id(0)
    prev = be_ref[jnp.maximum(i - 1, 0)]
    used = i < nu_ref[0]

    @pl.when(used & ((i == 0) | (be_ref[i] != prev)))
    def _():
        wg_s[...] = wg_ref[...].astype(BF16)
        wu_s[...] = wu_ref[...].astype(BF16)
        wd_s[...] = wd_ref[...].astype(BF16)

    @pl.when(used)
    def _():
        x_lo, x_hi = _unpack_bf16_halves(x_ref[...])
        n = x_lo.shape[1]
        a = _dot(x_lo, wg_s[:n, :]) + _dot(x_hi, wg_s[n:, :])
        u = _dot(x_lo, wu_s[:n, :]) + _dot(x_hi, wu_s[n:, :])
        hb = (a * _sigmoid(a)) * u
        o_ref[...] = _dot(hb.astype(BF16), wd_s[...])

    @pl.when(jnp.logical_not(used))
    def _():
        o_ref[...] = jnp.zeros_like(o_ref)


def expert_ffn(xs, block_e, n_used, wg, wu, wd, layer):
    rows, packed = xs.shape
    n_blocks = rows // MOE_BLOCK
    d, de = wg.shape[-2:]
    return pl.pallas_call(
        _expert_kernel, name="moe_experts",
        out_shape=jax.ShapeDtypeStruct((rows, d), F32),
        grid_spec=pltpu.PrefetchScalarGridSpec(
            num_scalar_prefetch=2, grid=(n_blocks,),
            in_specs=[pl.BlockSpec((MOE_BLOCK, packed), lambda i, be, nu: (i, 0)),
                      pl.BlockSpec((None, None, d, de), lambda i, be, nu: (layer, be[i], 0, 0)),
                      pl.BlockSpec((None, None, d, de), lambda i, be, nu: (layer, be[i], 0, 0)),
                      pl.BlockSpec((None, None, de, d), lambda i, be, nu: (layer, be[i], 0, 0))],
            out_specs=pl.BlockSpec((MOE_BLOCK, d), lambda i, be, nu: (i, 0)),
            scratch_shapes=[pltpu.VMEM((d, de), BF16), pltpu.VMEM((d, de), BF16), pltpu.VMEM((de, d), BF16)]),
        compiler_params=_cparams(("arbitrary",)),
    )(block_e, n_used, xs, wg, wu, wd)


RANK_WIDTHS = (1280, 1024, 768, 512, 256, 128)


def _rank_kernel(e_ref, tri_ref, rank_ref, count_ref, carry_ref):
    first = (pl.program_id(0) == 0) & (pl.program_id(1) == 0)

    @pl.when(first)
    def _():
        carry_ref[...] = jnp.zeros_like(carry_ref)

    width = e_ref.shape[-1]
    expert = lax.broadcasted_iota(jnp.int32, (N_EXPERTS, width), 0)
    carry = carry_ref[...]
    for k in range(TOP_K):
        onehot = jnp.where(expert == e_ref[k:k + 1, :], 1.0, 0.0)
        before = _dot(onehot.astype(BF16), tri_ref[...])
        rank_ref[k:k + 1, :] = jnp.sum(onehot * (before + carry), axis=0, keepdims=True).astype(jnp.int32)
        carry = carry + jnp.sum(onehot, axis=1, keepdims=True)
    carry_ref[...] = carry
    count_ref[...] = jnp.broadcast_to(carry, count_ref.shape)


def expert_ranks(eidx):
    n_batch, top_k, lc = eidx.shape
    width = next(w for w in RANK_WIDTHS if lc % w == 0)
    tri = jnp.asarray(np.triu(np.ones((width, width), np.float32), 1), BF16)
    blk = pl.BlockSpec((None, top_k, width), lambda b, i: (b, 0, i))
    rank, counts = pl.pallas_call(
        _rank_kernel, name="moe_rank",
        out_shape=[jax.ShapeDtypeStruct((n_batch, top_k, lc), jnp.int32),
                   jax.ShapeDtypeStruct((N_EXPERTS, LANES), F32)],
        grid=(n_batch, lc // width),
        in_specs=[blk, pl.BlockSpec(tri.shape, lambda b, i: (0, 0))],
        out_specs=[blk, pl.BlockSpec((N_EXPERTS, LANES), lambda b, i: (0, 0))],
        scratch_shapes=[pltpu.VMEM((N_EXPERTS, 1), F32)],
        compiler_params=_cparams(("arbitrary", "arbitrary")),
    )(eidx, tri)
    return rank, counts[:, 0].astype(jnp.int32)


def _combine_kernel(x_ref, y0_ref, y1_ref, g0_ref, g1_ref, mod_ref, o_ref):
    d = x_ref.shape[-1]
    y = y0_ref[...] * g0_ref[...] + y1_ref[...] * g1_ref[...]
    o_ref[...] = x_ref[...] + mod_ref[:, 5 * d:6 * d] * y


def moe_combine(stream, y0, y1, g0, g1, mod, seq, n_rows):
    n_batch, lc, d = stream.shape
    row = pl.BlockSpec((None, TM, d), lambda b, i: (b, i, 0))
    col = pl.BlockSpec((None, TM, 1), lambda b, i: (b, i, 0))
    return pl.pallas_call(
        _combine_kernel, name="moe_combine",
        out_shape=jax.ShapeDtypeStruct((n_batch, n_rows, d), F32),
        grid=(n_batch, n_rows // TM),
        in_specs=[row, row, row, col, col, pl.BlockSpec((None, 1, mod.shape[-1]), _mod_index(n_batch, seq // TM))],
        out_specs=row,
        compiler_params=_cparams(("parallel", "arbitrary")),
    )(stream, y0, y1, g0, g1, mod)


def moe_ffn(stream, h2, eidx, gates, mod, wg, wu, wd, layer, seq, n_rows):
    n_batch, lc, d = stream.shape
    n_tok = n_batch * lc
    n_assign = n_tok * TOP_K
    rank, counts = expert_ranks(eidx)
    padded = (counts + MOE_BLOCK - 1) // MOE_BLOCK * MOE_BLOCK
    pad_end = jnp.cumsum(padded)
    experts = jnp.arange(N_EXPERTS, dtype=jnp.int32)
    start = jnp.sum(jnp.where(eidx[..., None] == experts, pad_end - padded, 0), axis=-1)
    dest = start + rank
    n_blocks = n_assign // MOE_BLOCK + N_EXPERTS
    tok = jnp.broadcast_to((jnp.arange(n_batch, dtype=jnp.int32) * lc)[:, None, None]
                           + jnp.arange(lc, dtype=jnp.int32)[None, None, :], dest.shape)
    row_tok = jnp.zeros((n_blocks * MOE_BLOCK,), jnp.int32).at[dest.reshape(-1)].set(tok.reshape(-1))
    block_row = jnp.arange(n_blocks, dtype=jnp.int32) * MOE_BLOCK
    block_e = jnp.minimum(jnp.sum((pad_end[None, :] <= block_row[:, None]).astype(jnp.int32), axis=1), N_EXPERTS - 1)
    n_used = (pad_end[-1:] // MOE_BLOCK).astype(jnp.int32)
    xs = h2.reshape(n_tok, h2.shape[-1])[row_tok]
    ys = expert_ffn(xs, block_e, n_used, wg, wu, wd, layer)
    y0 = ys[dest[:, 0].reshape(-1)].reshape(n_batch, lc, d)
    y1 = ys[dest[:, 1].reshape(-1)].reshape(n_batch, lc, d)
    g0 = gates[:, 0].reshape(n_batch, lc, 1)
    g1 = gates[:, 1].reshape(n_batch, lc, 1)
    return moe_combine(stream, y0, y1, g0, g1, mod, seq, n_rows)


def kernel(x, c, ctx, c_ctx, norm1, norm2, w_ada, b_ada, w_in, w_out, rwkv_conv, rwkv_w0, rwkv_w_up, rwkv_a0,
           rwkv_a_up, rwkv_g_up, rwkv_k_k, rwkv_k_a, rwkv_r_k, rwkv_ln_w, rwkv_ln_b, rwkv_v0, rwkv_v_down,
           rwkv_v_up, fourier_w, fourier_b, gmlp_norm, gmlp_ws, gmlp_bs, attn_q_norm, attn_k_norm, attn_sink,
           w_router, router_bias, w_e_gate, w_e_up, w_e_down):
    n_batch, seq, d = x.shape
    depth = w_in.shape[0]
    stream = jnp.concatenate([x, ctx], axis=1)
    c_all = jnp.zeros((SUBLANES, d), F32).at[:n_batch].set(c).at[n_batch].set(c_ctx)
    col_splits = np.cumsum([A_COLS, B_COLS, C_COLS])
    v_first = None
    for l in range(depth):
        mod = ada_modulation(c_all, w_ada, b_ada, l).reshape(SUBLANES, 1, 6 * d)
        w_parts = [w.astype(BF16) for w in jnp.split(w_in[l], col_splits, axis=1)]
        pa, pb, pc, pd = in_projection(stream, mod, norm1[l], w_parts, seq)

        vres = None if l == 0 else (rwkv_v0[l - 1], rwkv_v_down[l - 1], rwkv_v_up[l - 1])
        r, kk, v, k0, wd0, kka0, k1, wd1, kka1, g = rwkv_prepare(
            pa, seq, rwkv_conv[l], rwkv_w0[l], rwkv_w_up[l], rwkv_a0[l], rwkv_a_up[l], rwkv_g_up[l],
            rwkv_k_k[l], rwkv_k_a[l], v_first, vres)
        if l == 0:
            v_first = v
        yf, yb = rwkv_scan(r, kk, v, ((k0, wd0, kka0), (k1, wd1, kka1)), seq)
        y_a = rwkv_readout(yf, yb, r, k0, k1, v, g, rwkv_r_k[l].reshape(-1), rwkv_ln_w[l], rwkv_ln_b[l])

        y_b = fourier_mix(pb, seq, fourier_w[l], fourier_b[l])
        y_c = gmlp_mix(pc, gmlp_norm[l], gmlp_ws[l], gmlp_bs[l])
        q, kv, vv = attention_prepare(pd, seq, attn_q_norm[l], attn_k_norm[l])
        y_d = window_attention(q, kv, vv, attn_sink[l], seq)

        stream, h2, eidx, gates = out_projection_router(
            (y_a, y_b, y_c, y_d), stream, mod, norm2[l], w_out[l].astype(BF16), w_router, router_bias, seq)
        n_rows = seq if l == depth - 1 else stream.shape[1]
        stream = moe_ffn(stream, h2, eidx, gates, mod, w_e_gate, w_e_up, w_e_down, l, seq, n_rows)
    return stream
```

```python
import functools

import numpy as np
import jax
import jax.numpy as jnp
from jax import lax
from jax.experimental import pallas as pl
from jax.experimental.pallas import tpu as pltpu

F32 = jnp.float32
BF16 = jnp.bfloat16

D_MODEL = 2048
DEPTH = 2
GRID_W = 64
EPS = 1e-6
NEG_INF = -1e30
GROUP_W = 512
RWKV_HEAD = 64
RWKV_HEADS = 8
DECAY_LORA = 64
ICLR_LORA = 64
GATE_LORA = 128
VRES_LORA = 32
RWKV_GN_EPS = 64e-5
FOURIER_GROUPS = 4
FOURIER_CH = 128
GMLP_GROUPS = 4
GMLP_CH = 128
GMLP_CHUNK = 128
ATT_HEAD = 64
ATT_Q_HEADS = 8
ATT_KV_HEADS = 2
WINDOW = 128
ATT_BLOCK = 128
ROPE_BASE = 10000.0
N_EXPERTS = 64
N_EXPERT_GROUPS = 8
EXPERTS_PER_GROUP = 8
TOP_K = 2
D_EXPERT = 512
MOE_BLOCK = 128
A_COLS = 1920
B_COLS = 512
C_COLS = 1024
D_COLS = 768

LANES = 128
SUBLANES = 8
TM = 256
SCAN_TB = 128
SCAN_GROUP = SUBLANES
SCAN_TILES = 4
OUTPROJ_SPLIT = 2
VMEM_LIMIT = 56 * 1024 * 1024


def _cparams(sem, vmem=VMEM_LIMIT):
    return pltpu.CompilerParams(dimension_semantics=sem, vmem_limit_bytes=vmem)


def _dot(a, b):
    return jnp.dot(a, b, preferred_element_type=F32)


def _dot_nt(a, b):
    return lax.dot_general(a, b, (((1,), (1,)), ((), ())), preferred_element_type=F32)


def _dot_tn(a, b):
    return lax.dot_general(a, b, (((0,), (0,)), ((), ())), preferred_element_type=F32)


def _split(x):
    hi = x.astype(BF16)
    lo = (x - hi.astype(F32)).astype(BF16)
    return hi, lo


def _pack_bf16_halves(x):
    n = x.shape[1] // 2
    lo = lax.bitcast_convert_type(x[:, :n].astype(BF16).astype(F32), jnp.uint32) >> 16
    hi = lax.bitcast_convert_type(x[:, n:].astype(BF16).astype(F32), jnp.uint32) & jnp.uint32(0xFFFF0000)
    return hi | lo


def _unpack_bf16_halves(w):
    lo = lax.bitcast_convert_type(w << 16, F32).astype(BF16)
    hi = lax.bitcast_convert_type(w & jnp.uint32(0xFFFF0000), F32).astype(BF16)
    return lo, hi


def _dot1(a, b):
    return _dot(a.astype(BF16), b.astype(BF16))


def _segsum(x, seg):
    xh, xl = _split(x)
    return _dot(xh, seg) + _dot(xl, seg)


def _sigmoid(x):
    return 1.0 / (1.0 + jnp.exp(-x))


def _seg_matrix(width, seg):
    i = np.arange(width) // seg
    return jnp.asarray((i[:, None] == i[None, :]).astype(np.float32), dtype=BF16)


def _ada_kernel(c_ref, w_ref, b_ref, o_ref):
    c = c_ref[...]
    a = (c * _sigmoid(c)).astype(BF16)
    o_ref[...] = _dot(a, w_ref[...].astype(BF16)) + b_ref[...]


def ada_modulation(c_all, w_ada, b_ada, layer, tn=1024):
    rows, d = c_all.shape
    n_layers, _, n = w_ada.shape
    return pl.pallas_call(
        _ada_kernel, name="ada_mod",
        out_shape=jax.ShapeDtypeStruct((rows, n), F32),
        grid=(n // tn,),
        in_specs=[pl.BlockSpec((rows, d), lambda j: (0, 0)),
                  pl.BlockSpec((None, d, tn), lambda j: (layer, 0, j)),
                  pl.BlockSpec((None, 1, tn), lambda j: (layer, 0, j))],
        out_specs=pl.BlockSpec((rows, tn), lambda j: (0, j)),
        compiler_params=_cparams(("arbitrary",)),
    )(c_all, w_ada, b_ada.reshape(n_layers, 1, n))


def _inproj_kernel(x_ref, mod_ref, g_ref, wa_ref, wb_ref, wc_ref, wd_ref, pa_ref, pb_ref, pc_ref, pd_ref):
    d = x_ref.shape[-1]
    x = x_ref[...]
    y = x * lax.rsqrt(jnp.mean(x * x, axis=-1, keepdims=True) + EPS) * g_ref[...]
    sh = mod_ref[:, 0:d]
    sc = mod_ref[:, d:2 * d]
    h = (y * (1.0 + sc) + sh).astype(BF16)
    pa_ref[...] = _dot(h, wa_ref[...])
    pb_ref[...] = _dot(h, wb_ref[...])
    pc_ref[...] = _dot(h, wc_ref[...])
    pd_ref[...] = _dot(h, wd_ref[...])


def _mod_index(n_batch, ctx_tile):
    return lambda b, i: (jnp.where(i >= ctx_tile, n_batch, b), 0, 0)


def in_projection(stream, mod, gain, w_parts, seq):
    n_batch, lc, d = stream.shape
    ctx_tile = seq // TM
    resident = lambda w: pl.BlockSpec(w.shape, lambda b, i: (0, 0), pipeline_mode=pl.Buffered(1))
    row = lambda n: pl.BlockSpec((None, TM, n), lambda b, i: (b, i, 0))
    return pl.pallas_call(
        _inproj_kernel, name="in_proj",
        out_shape=[jax.ShapeDtypeStruct((n_batch, lc, w.shape[1]), F32) for w in w_parts],
        grid=(n_batch, lc // TM),
        in_specs=[row(d),
                  pl.BlockSpec((None, 1, mod.shape[-1]), _mod_index(n_batch, ctx_tile)),
                  pl.BlockSpec((1, d), lambda b, i: (0, 0))] + [resident(w) for w in w_parts],
        out_specs=[row(w.shape[1]) for w in w_parts],
        compiler_params=_cparams(("parallel", "arbitrary")),
    )(stream, mod, gain.reshape(1, d), *w_parts)


def _rwkv_prep_kernel(has_vres, ctx_tile, *refs):
    (pa_ref, hp_ref, hn_ref, conv_ref, w0_ref, wup_ref, a0_ref, aup_ref, gup_ref, kk_par_ref, ka_par_ref,
     seg_ref) = refs[:12]
    rest = refs[12:]
    if has_vres:
        vf_ref, v0_ref, vdn_ref, vup_ref = rest[:4]
        rest = rest[4:]
    r_o, kk_o, v_o, k0_o, w0_o, kka0_o, k1_o, w1_o, kka1_o, g_o = rest
    gw = GROUP_W
    x = pa_ref[...]
    tm = x.shape[0]
    i = pl.program_id(1)
    first = (i == 0) | (i == ctx_tile)
    last = (i == ctx_tile - 1) | (i == pl.num_programs(1) - 1)
    halo_prev = jnp.where(first, 0.0, hp_ref[SUBLANES - 1:SUBLANES, :])
    halo_next = jnp.where(last, 0.0, hn_ref[0:1, :])
    row = lax.broadcasted_iota(jnp.int32, x.shape, 0)
    x_prev = jnp.where(row == 0, halo_prev, pltpu.roll(x, 1, 0))
    x_next = jnp.where(row == tm - 1, halo_next, pltpu.roll(x, tm - 1, 0))
    y = x_prev * conv_ref[0:1, :] + x * conv_ref[1:2, :] + x_next * conv_ref[2:3, :]
    r = y[:, 0:gw]
    k = y[:, gw:2 * gw]
    v = y[:, 2 * gw:3 * gw]
    wl = y[:, 3 * gw:3 * gw + 2 * DECAY_LORA]
    al = y[:, 3 * gw + 2 * DECAY_LORA:3 * gw + 2 * DECAY_LORA + 2 * ICLR_LORA]
    gl = y[:, 3 * gw + 2 * DECAY_LORA + 2 * ICLR_LORA:]
    if has_vres:
        mix = _sigmoid(v0_ref[...] + _dot1(_dot1(v, vdn_ref[...]), vup_ref[...]))
        v = v + (vf_ref[...] - v) * mix
    kk = k * kk_par_ref[...]
    nrm = jnp.maximum(jnp.sqrt(_segsum(kk * kk, seg_ref[...])), 1e-12)
    kk = kk / nrm
    r_o[...] = r
    kk_o[...] = kk
    v_o[...] = v
    tanh_wl = jnp.tanh(wl)
    for d, (k_o, w_o, kka_o) in enumerate(((k0_o, w0_o, kka0_o), (k1_o, w1_o, kka1_o))):
        w_raw = w0_ref[d:d + 1, :] + _dot1(tanh_wl, wup_ref[d])
        z = -w_raw
        softplus = jnp.maximum(z, 0.0) + jnp.log(1.0 + jnp.exp(-jnp.abs(z)))
        w_o[...] = jnp.exp(-jnp.exp(-softplus - 0.5))
        a = _sigmoid(a0_ref[d:d + 1, :] + _dot1(al, aup_ref[d]))
        k_o[...] = k * (1.0 + (a - 1.0) * ka_par_ref[...])
        kka_o[...] = kk * a
    g_o[...] = _dot1(_sigmoid(gl), gup_ref[...])


def rwkv_prepare(pa, seq, conv_w, w0, w_up, a0, a_up, g_up, k_k, k_a, v_first=None, vres=None):
    n_batch, lc, c = pa.shape
    gw = GROUP_W
    sub_per_tile = TM // SUBLANES
    n_sub = lc // SUBLANES
    pad_rows = lambda w, d, n: jnp.zeros((2 * n, gw), F32).at[d * n:(d + 1) * n].set(w)
    wup = jnp.stack([pad_rows(w_up[d], d, DECAY_LORA) for d in range(2)])
    aup = jnp.stack([pad_rows(a_up[d], d, ICLR_LORA) for d in range(2)])
    seg = _seg_matrix(gw, RWKV_HEAD)
    const = lambda a: pl.BlockSpec(a.shape, lambda b, i: (0,) * a.ndim)
    row = lambda n: pl.BlockSpec((None, TM, n), lambda b, i: (b, i, 0))
    halo_prev = pl.BlockSpec((None, SUBLANES, c), lambda b, i: (b, jnp.maximum(i * sub_per_tile - 1, 0), 0))
    halo_next = pl.BlockSpec((None, SUBLANES, c), lambda b, i: (b, jnp.minimum((i + 1) * sub_per_tile, n_sub - 1), 0))
    args = [pa, pa, pa, conv_w, w0, wup, a0, aup, g_up, k_k.reshape(1, gw), k_a.reshape(1, gw), seg]
    specs = [row(c), halo_prev, halo_next] + [const(a) for a in args[3:]]
    has_vres = vres is not None
    if has_vres:
        v0, v_down, v_up = vres
        vdn = jnp.zeros((gw, LANES), F32).at[:, :VRES_LORA].set(v_down)
        vup = jnp.zeros((LANES, gw), F32).at[:VRES_LORA].set(v_up)
        extra = [v_first, v0.reshape(1, gw), vdn, vup]
        args += extra
        specs += [row(gw)] + [const(a) for a in extra[1:]]
    return pl.pallas_call(
        functools.partial(_rwkv_prep_kernel, has_vres, seq // TM), name="rwkv_prep",
        out_shape=[jax.ShapeDtypeStruct((n_batch, lc, gw), F32)] * 10,
        grid=(n_batch, lc // TM),
        in_specs=specs,
        out_specs=[row(gw)] * 10,
        compiler_params=_cparams(("parallel", "arbitrary")),
    )(*args)


def _scan_kernel(n_batch, rf, kkf, vf, kf, wf, kaf, rb, kkb, vb, kb, wb, kab, bsel_ref, segbd_ref, ysel_ref,
                 yf_ref, yb_ref, s_ref, vk_ref, yt_ref):
    tb = rf.shape[1]
    n_pairs = GROUP_W // LANES
    half = RWKV_HEAD

    @pl.when(pl.program_id(0) == 0)
    def _():
        s_ref[...] = jnp.zeros_like(s_ref)
        yt_ref[...] = jnp.zeros_like(yt_ref)

    segbd = segbd_ref[...]
    dirs = ((rf, kkf, vf, kf, wf, kaf, yf_ref), (rb, kkb, vb, kb, wb, kab, yb_ref))
    chains = [(d, b, p) for d in range(2) for b in range(n_batch) for p in range(n_pairs)]
    n_ch = len(chains)
    lane = lax.broadcasted_iota(jnp.int32, (RWKV_HEAD, LANES), 1) % half

    n_sets = 2
    per_set = n_ch // n_sets
    sets = [list(range(si * per_set, (si + 1) * per_set)) for si in range(n_sets)]

    def side_by_side(parts):
        return jnp.concatenate([jnp.concatenate(parts[i:i + 2], axis=1) for i in range(0, per_set, 2)], axis=0)

    def piece(full, n):
        return full[(n // 2) * RWKV_HEAD:(n // 2 + 1) * RWKV_HEAD, (n % 2) * LANES:(n % 2 + 1) * LANES]

    tile_w = SCAN_GROUP * LANES

    def group(g, carry):
        bases = [(pl.multiple_of((g * SCAN_TILES + u) * SCAN_GROUP, SCAN_GROUP),
                  pl.multiple_of(tb - SCAN_GROUP - (g * SCAN_TILES + u) * SCAN_GROUP, SCAN_GROUP))
                 for u in range(SCAN_TILES)]

        def row(which, ci, u, j):
            d, b, p = chains[ci]
            t = j if d == 0 else SCAN_GROUP - 1 - j
            return dirs[d][which][b, pl.ds(bases[u][d], SCAN_GROUP), p * LANES:(p + 1) * LANES][t:t + 1]

        def sk_dot(states_bf, cs, u, j):
            return _dot(side_by_side([states_bf[n] * row(1, ci, u, j).astype(BF16) for n, ci in enumerate(cs)]),
                        segbd)

        def outer_products(cs, u):
            for ci in cs:
                d, b, p = chains[ci]
                v8 = dirs[d][2][b, pl.ds(bases[u][d], SCAN_GROUP), p * LANES:(p + 1) * LANES]
                k8 = dirs[d][3][b, pl.ds(bases[u][d], SCAN_GROUP), p * LANES:(p + 1) * LANES]
                vt = jnp.concatenate([v8[:, :RWKV_HEAD], v8[:, RWKV_HEAD:]], axis=0).T.astype(BF16)
                kmat = jnp.concatenate([bsel_ref[t] * k8[t:t + 1] for t in range(SCAN_GROUP)],
                                       axis=1).astype(BF16)
                vk_ref[ci, :, u * tile_w:(u + 1) * tile_w] = _dot(vt, kmat)

        sk = []
        for cs in sets:
            outer_products(cs, 0)
            sk.append(sk_dot([s_ref[ci].astype(BF16) for ci in cs], cs, 0, 0))
        for u in range(SCAN_TILES):
            y_tile = [None] * n_sets
            for j in range(SCAN_GROUP):
                for si, cs in enumerate(sets):
                    d = chains[cs[0]][0]
                    t = j if d == 0 else SCAN_GROUP - 1 - j
                    states_bf = []
                    for n, ci in enumerate(cs):
                        s = (s_ref[ci] * row(4, ci, u, j) - piece(sk[si], n) * row(5, ci, u, j)
                             + vk_ref[ci, :, u * tile_w + t * LANES:u * tile_w + (t + 1) * LANES])
                        s_ref[ci] = s
                        states_bf.append(s.astype(BF16))
                    y_lhs = side_by_side([s * row(0, ci, u, j).astype(BF16) for s, ci in zip(states_bf, cs)])
                    y_step = _dot(y_lhs, ysel_ref[t])
                    y_tile[si] = y_step if j == 0 else y_tile[si] + y_step
                    if j + 1 < SCAN_GROUP:
                        sk[si] = sk_dot(states_bf, cs, u, j + 1)
                    elif u + 1 < SCAN_TILES:
                        sk[si] = sk_dot(states_bf, cs, u + 1, 0)
                    if j == SCAN_GROUP // 2 and u + 1 < SCAN_TILES:
                        outer_products(cs, u + 1)
            for si, cs in enumerate(sets):
                d = chains[cs[0]][0]
                first = bases[u][d] % half
                for n, ci in enumerate(cs):
                    moved = pltpu.roll(piece(y_tile[si], n), first, 1)
                    yt_ref[ci] = jnp.where(lane // SCAN_GROUP == first // SCAN_GROUP, moved, yt_ref[ci])
        return carry

    def flush(first_row):
        for ci, (d, b, p) in enumerate(chains):
            t = yt_ref[ci].T
            rows = slice(first_row[d], first_row[d] + half)
            dirs[d][6][b, rows, p * LANES:(p + 1) * LANES] = jnp.concatenate([t[:half], t[half:]], axis=1)

    groups_per_flush = half // (SCAN_GROUP * SCAN_TILES)
    for part in range(tb // half):
        lax.fori_loop(part * groups_per_flush, (part + 1) * groups_per_flush, group, 0)
        flush((part * half, tb - (part + 1) * half))


def _scan_selectors():
    bsel = np.zeros((SCAN_GROUP, 2 * SCAN_GROUP, LANES), np.float32)
    for h in range(2):
        for j in range(SCAN_GROUP):
            bsel[j, h * SCAN_GROUP + j, h * RWKV_HEAD:(h + 1) * RWKV_HEAD] = 1.0
    block = np.arange(2 * LANES) // RWKV_HEAD
    ysel = np.zeros((SCAN_GROUP, 2 * LANES, 2 * LANES), np.float32)
    for t in range(SCAN_GROUP):
        ysel[t] = (block[:, None] == block[None, :]) & ((np.arange(2 * LANES) % RWKV_HEAD) == t)[None, :]
    return jnp.asarray(bsel, F32), _seg_matrix(2 * LANES, RWKV_HEAD), jnp.asarray(ysel, BF16)


def rwkv_scan(r, kk, v, dirs, seq):
    n_batch, lc, gw = r.shape
    tb = SCAN_TB
    nx = seq // tb
    nc = (lc - seq) // tb
    n_chains = 2 * n_batch * (gw // LANES)
    fwd = lambda i: (0, jnp.where(i < nc, nx + i, i - nc), 0)
    bwd = lambda i: (0, nx + nc - 1 - i, 0)
    blk = lambda m: pl.BlockSpec((n_batch, tb, gw), m)
    const = lambda a: pl.BlockSpec(a.shape, lambda i: (0,) * a.ndim)
    bsel, segbd, ysel = _scan_selectors()
    return pl.pallas_call(
        functools.partial(_scan_kernel, n_batch), name="rwkv_scan",
        out_shape=[jax.ShapeDtypeStruct((n_batch, lc, gw), F32)] * 2,
        grid=(nx + nc,),
        in_specs=[blk(fwd)] * 6 + [blk(bwd)] * 6 + [const(bsel), const(segbd), const(ysel)],
        out_specs=[blk(fwd), blk(bwd)],
        scratch_shapes=[pltpu.VMEM((n_chains, RWKV_HEAD, LANES), F32),
                        pltpu.VMEM((n_chains, RWKV_HEAD, SCAN_TILES * SCAN_GROUP * LANES), F32),
                        pltpu.VMEM((n_chains, RWKV_HEAD, LANES), F32)],
        compiler_params=_cparams(("arbitrary",)),
    )(r, kk, v, *dirs[0], r, kk, v, *dirs[1], bsel, segbd, ysel)


def _rwkv_readout_kernel(yf_ref, yb_ref, r_ref, k0_ref, k1_ref, v_ref, g_ref, rk_ref, lnw_ref, lnb_ref, seg_ref,
                         o_ref):
    seg = seg_ref[...]
    inv = 1.0 / RWKV_HEAD
    y = yf_ref[...] + yb_ref[...]
    mu = _segsum(y, seg) * inv
    dlt = y - mu
    var = _segsum(dlt * dlt, seg) * inv
    yn = dlt * lax.rsqrt(var + RWKV_GN_EPS) * lnw_ref[...] + lnb_ref[...]
    kbar = 0.5 * (k0_ref[...] + k1_ref[...])
    bonus = _segsum(r_ref[...] * kbar * rk_ref[...], seg) * v_ref[...]
    o_ref[...] = ((yn + bonus) * g_ref[...]).astype(o_ref.dtype)


def rwkv_readout(yf, yb, r, k0, k1, v, g, r_k, ln_w, ln_b):
    n_batch, lc, gw = r.shape
    seg = _seg_matrix(gw, RWKV_HEAD)
    row = pl.BlockSpec((None, TM, gw), lambda b, i: (b, i, 0))
    const = lambda a: pl.BlockSpec(a.shape, lambda b, i: (0,) * a.ndim)
    params = [r_k.reshape(1, gw), ln_w.reshape(1, gw), ln_b.reshape(1, gw), seg]
    return pl.pallas_call(
        _rwkv_readout_kernel, name="rwkv_readout",
        out_shape=jax.ShapeDtypeStruct((n_batch, lc, gw), BF16),
        grid=(n_batch, lc // TM),
        in_specs=[row] * 7 + [const(a) for a in params],
        out_specs=row,
        compiler_params=_cparams(("parallel", "arbitrary")),
    )(yf, yb, r, k0, k1, v, g, *params)


def _dft_tables(n):
    k = np.arange(n)
    ang = 2.0 * np.pi * ((k[:, None] * k[None, :]) % n) / n
    return np.cos(ang), np.sin(ang)


def _fft_a_kernel(nb, f_ref, gr_ref, gi_ref, zr_ref, zi_ref):
    n1 = gr_ref.shape[1]
    n2_total = f_ref.shape[0] // n1
    j = pl.program_id(2)
    for i in range(nb):
        rows = f_ref[pl.ds(j * nb + i, n1, stride=n2_total), :].astype(BF16)
        zr_ref[i * n1:(i + 1) * n1, :] = _dot(gr_ref[i], rows)
        zi_ref[i * n1:(i + 1) * n1, :] = _dot(gi_ref[i], rows)


def _fft_b_kernel(kb, scale, zr_ref, zi_ref, fc_ref, fs_ref, cc_ref, cs_ref, w_ref, b_ref, o_ref):
    n2 = fc_ref.shape[0]
    n1 = zr_ref.shape[0] // n2
    j = pl.program_id(2)
    fc, fs = fc_ref[...], fs_ref[...]
    w = w_ref[...].astype(BF16)
    xs = []
    for i in range(kb):
        zr = zr_ref[pl.ds(j * kb + i, n2, stride=n1), :].astype(BF16)
        zi = zi_ref[pl.ds(j * kb + i, n2, stride=n1), :].astype(BF16)
        xs.append((_dot(fc, zr) + _dot(fs, zi), _dot(fc, zi) - _dot(fs, zr)))
    res = [(_dot(xr.astype(BF16), cc_ref[...]) + _dot(xi.astype(BF16), cs_ref[...])) * scale for xr, xi in xs]
    for i in range(kb):
        o_ref[pl.ds(j * kb + i, n2, stride=n1), :] = _dot(res[i].astype(BF16), w) + b_ref[...]


def _dft_dense_kernel(scale, f_ref, fc_ref, fs_ref, cc_ref, cs_ref, w_ref, b_ref, o_ref):
    f = f_ref[...].astype(BF16)
    xr = _dot(fc_ref[...], f)
    xi = -_dot(fs_ref[...], f)
    re = (_dot(xr.astype(BF16), cc_ref[...]) + _dot(xi.astype(BF16), cs_ref[...])) * scale
    o_ref[...] = _dot(re.astype(BF16), w_ref[...].astype(BF16)) + b_ref[...]


def fourier_mix(pb, seq, w, bias):
    n_batch, lc, gw = pb.shape
    ctx = lc - seq
    ch = FOURIER_CH
    groups = FOURIER_GROUPS
    n2 = LANES
    n1 = seq // n2
    cc, cs = (jnp.asarray(t, BF16) for t in _dft_tables(ch))
    bias3 = bias.reshape(groups, 1, ch)

    k1 = np.arange(n1)[None, :, None]
    m1 = np.arange(n1)[None, None, :]
    m2 = np.arange(n2)[:, None, None]
    ang = 2.0 * np.pi * ((k1 * (n2 * m1 + m2)) % seq) / seq
    g_re = jnp.asarray(np.cos(ang), BF16)
    g_im = jnp.asarray(-np.sin(ang), BF16)
    nb = min(16, n2)
    slab = pl.BlockSpec((None, seq, ch), lambda b, g, j: (b, 0, g))
    gspec = pl.BlockSpec((nb, n1, n1), lambda b, g, j: (j, 0, 0))
    zspec = pl.BlockSpec((None, None, nb * n1, ch), lambda b, g, j: (b, g, j, 0))
    zr, zi = pl.pallas_call(
        functools.partial(_fft_a_kernel, nb), name="fft_stage_a",
        out_shape=[jax.ShapeDtypeStruct((n_batch, groups, seq, ch), F32)] * 2,
        grid=(n_batch, groups, n2 // nb),
        in_specs=[slab, gspec, gspec],
        out_specs=[zspec, zspec],
        compiler_params=_cparams(("parallel", "parallel", "arbitrary")),
    )(pb, g_re, g_im)

    fc, fs = (jnp.asarray(t, BF16) for t in _dft_tables(n2))
    kb = min(16, n1)
    scale = 1.0 / np.sqrt(float(seq) * ch)
    zslab = pl.BlockSpec((None, None, seq, ch), lambda b, g, j: (b, g, 0, 0), pipeline_mode=pl.Buffered(1))
    const = lambda a: pl.BlockSpec(a.shape, lambda b, g, j: (0,) * a.ndim)
    wspec = pl.BlockSpec((None, ch, ch), lambda b, g, j: (g, 0, 0))
    bspec = pl.BlockSpec((None, 1, ch), lambda b, g, j: (g, 0, 0))
    y_x = pl.pallas_call(
        functools.partial(_fft_b_kernel, kb, scale), name="fft_stage_b",
        out_shape=jax.ShapeDtypeStruct((n_batch, seq, gw), F32),
        grid=(n_batch, groups, n1 // kb),
        in_specs=[zslab, zslab, const(fc), const(fs), const(cc), const(cs), wspec, bspec],
        out_specs=pl.BlockSpec((None, seq, ch), lambda b, g, j: (b, 0, g)),
        compiler_params=_cparams(("parallel", "parallel", "arbitrary")),
    )(zr, zi, fc, fs, cc, cs, w, bias3)

    fcc, fsc = (jnp.asarray(t, BF16) for t in _dft_tables(ctx))
    const2 = lambda a: pl.BlockSpec(a.shape, lambda b, g: (0,) * a.ndim)
    ctx_tile = seq // ctx
    y_c = pl.pallas_call(
        functools.partial(_dft_dense_kernel, 1.0 / np.sqrt(float(ctx) * ch)), name="dft_ctx",
        out_shape=jax.ShapeDtypeStruct((n_batch, ctx, gw), F32),
        grid=(n_batch, groups),
        in_specs=[pl.BlockSpec((None, ctx, ch), lambda b, g: (b, ctx_tile, g)),
                  const2(fcc), const2(fsc), const2(cc), const2(cs),
                  pl.BlockSpec((None, ch, ch), lambda b, g: (g, 0, 0)),
                  pl.BlockSpec((None, 1, ch), lambda b, g: (g, 0, 0))],
        out_specs=pl.BlockSpec((None, ctx, ch), lambda b, g: (b, 0, g)),
        compiler_params=_cparams(("parallel", "arbitrary")),
    )(pb, fcc, fsc, cc, cs, w, bias3)
    return jnp.concatenate([y_x, y_c], axis=1)


def _gmlp_kernel(pc_ref, norm_ref, ws_ref, bs_ref, o_ref):
    gw = GROUP_W
    x = pc_ref[...]
    z = 0.5 * x * (1.0 + jnp.tanh(0.7978845608028654 * (x + 0.044715 * (x * x * x))))
    tm = x.shape[0]
    for g in range(GMLP_GROUPS):
        u = z[:, g * GMLP_CH:(g + 1) * GMLP_CH]
        v = z[:, gw + g * GMLP_CH:gw + (g + 1) * GMLP_CH]
        v = v * lax.rsqrt(jnp.mean(v * v, axis=-1, keepdims=True) + EPS) * norm_ref[g:g + 1, :]
        v = v.astype(BF16)
        ws = ws_ref[g].astype(BF16)
        for c in range(tm // GMLP_CHUNK):
            rows = slice(c * GMLP_CHUNK, (c + 1) * GMLP_CHUNK)
            f = _dot(ws, v[rows]) + bs_ref[g]
            o_ref[rows, g * GMLP_CH:(g + 1) * GMLP_CH] = (u[rows] * f).astype(o_ref.dtype)


def gmlp_mix(pc, norm_g, ws, bs):
    n_batch, lc, c = pc.shape
    bs_b = jnp.broadcast_to(bs[:, :, None], bs.shape + (GMLP_CH,))
    const = lambda a: pl.BlockSpec(a.shape, lambda b, i: (0,) * a.ndim)
    return pl.pallas_call(
        _gmlp_kernel, name="gmlp",
        out_shape=jax.ShapeDtypeStruct((n_batch, lc, GROUP_W), BF16),
        grid=(n_batch, lc // TM),
        in_specs=[pl.BlockSpec((None, TM, c), lambda b, i: (b, i, 0)), const(norm_g), const(ws), const(bs_b)],
        out_specs=pl.BlockSpec((None, TM, GROUP_W), lambda b, i: (b, i, 0)),
        compiler_params=_cparams(("parallel", "arbitrary")),
    )(pc, norm_g, ws, bs_b)


def _rope_tables(seq, ctx):
    half = ATT_HEAD // 4
    inv_freq = ROPE_BASE ** (-np.arange(half, dtype=np.float64) / half)
    pos = np.arange(seq)
    ang_r = (pos // GRID_W)[:, None] * inv_freq[None, :]
    ang_c = (pos % GRID_W)[:, None] * inv_freq[None, :]
    cos = np.concatenate([np.cos(ang_r)] * 2 + [np.cos(ang_c)] * 2, axis=1)
    sin = np.concatenate([-np.sin(ang_r), np.sin(ang_r), -np.sin(ang_c), np.sin(ang_c)], axis=1)
    cos = np.concatenate([cos, np.ones((ctx, ATT_HEAD))], axis=0)
    sin = np.concatenate([sin, np.zeros((ctx, ATT_HEAD))], axis=0)
    return jnp.asarray(np.tile(cos, (1, 2)), F32), jnp.asarray(np.tile(sin, (1, 2)), F32)


def _rope(t, cos, sin):
    n = t.shape[1]
    q = ATT_HEAD // 4
    lane = lax.broadcasted_iota(jnp.int32, t.shape, 1)
    swapped = jnp.where((lane % (2 * q)) < q, pltpu.roll(t, n - q, 1), pltpu.roll(t, q, 1))
    return t * cos + swapped * sin


def _attn_prep_kernel(pd_ref, cos_ref, sin_ref, qg_ref, kg_ref, seg_ref, q_o, k_o, v_o):
    gw = GROUP_W
    kvw = ATT_KV_HEADS * ATT_HEAD
    seg = seg_ref[...]
    inv = 1.0 / ATT_HEAD
    cos, sin = cos_ref[...], sin_ref[...]
    q = pd_ref[:, 0:gw]
    q = q * lax.rsqrt(_segsum(q * q, seg) * inv + EPS) * qg_ref[...]
    q = _rope(q, jnp.concatenate([cos] * (gw // kvw), axis=1), jnp.concatenate([sin] * (gw // kvw), axis=1))
    q_o[...] = (q * (ATT_HEAD ** -0.5)).astype(q_o.dtype)
    k = pd_ref[:, gw:gw + kvw]
    k = k * lax.rsqrt(_segsum(k * k, seg[:kvw, :kvw]) * inv + EPS) * kg_ref[...]
    k = _rope(k, cos, sin)
    v = pd_ref[:, gw + kvw:gw + 2 * kvw]
    lane = lax.broadcasted_iota(jnp.int32, k.shape, 1)

    def variants(t):
        h0 = jnp.where(lane < ATT_HEAD, t, 0.0)
        h1 = jnp.where(lane >= ATT_HEAD, t, 0.0)
        return jnp.concatenate([h0, pltpu.roll(h0, ATT_HEAD, 1), pltpu.roll(h1, ATT_HEAD, 1), h1], axis=1)

    k_o[...] = variants(k).astype(k_o.dtype)
    v_o[...] = variants(v).astype(v_o.dtype)


def attention_prepare(pd, seq, q_gain, k_gain):
    n_batch, lc, c = pd.shape
    gw = GROUP_W
    kvw = ATT_KV_HEADS * ATT_HEAD
    cos, sin = _rope_tables(seq, lc - seq)
    seg = _seg_matrix(gw, ATT_HEAD)
    qg = jnp.tile(q_gain, ATT_Q_HEADS).reshape(1, gw)
    kg = jnp.tile(k_gain, ATT_KV_HEADS).reshape(1, kvw)
    const = lambda a: pl.BlockSpec(a.shape, lambda b, i: (0,) * a.ndim)
    tab = pl.BlockSpec((TM, kvw), lambda b, i: (i, 0))
    row = lambda n: pl.BlockSpec((None, TM, n), lambda b, i: (b, i, 0))
    return pl.pallas_call(
        _attn_prep_kernel, name="attn_prep",
        out_shape=[jax.ShapeDtypeStruct((n_batch, lc, gw), BF16)] * 3,
        grid=(n_batch, lc // TM),
        in_specs=[row(c), tab, tab, const(qg), const(kg), const(seg)],
        out_specs=[row(gw)] * 3,
        compiler_params=_cparams(("parallel", "arbitrary")),
    )(pd, cos, sin, qg, kg, seg)


def _attn_kernel(nb, sink_ref, q_ref, kp_ref, kc_ref, kn_ref, kx_ref, vp_ref, vc_ref, vn_ref, vx_ref, o_ref):
    i = pl.program_id(1)
    blk = q_ref.shape[0]
    is_lat = i < nb
    rowi = lax.broadcasted_iota(jnp.int32, (blk, blk), 0)
    coli = lax.broadcasted_iota(jnp.int32, (blk, blk), 1)
    m_prev = (coli >= rowi) & is_lat & (i >= 1)
    m_cur = jnp.broadcast_to(is_lat, (blk, blk))
    m_next = (coli <= rowi) & (i < nb - 1)
    def cols(h):
        g = h // (ATT_Q_HEADS // ATT_KV_HEADS)
        return slice((2 * g + h % 2) * LANES, (2 * g + h % 2 + 1) * LANES)

    scores = []
    for h in range(ATT_Q_HEADS):
        q = q_ref[:, (h // 2) * LANES:(h // 2 + 1) * LANES]
        col = cols(h)
        scores.append((_dot_nt(q, kp_ref[:, col]), _dot_nt(q, kc_ref[:, col]), _dot_nt(q, kn_ref[:, col]),
                       _dot_nt(q, kx_ref[:, col])))
    acc = None
    for h in range(ATT_Q_HEADS):
        col = cols(h)
        sink = sink_ref[h]
        s1 = jnp.where(m_prev, scores[h][0], NEG_INF)
        s2 = jnp.where(m_cur, scores[h][1], NEG_INF)
        s3 = jnp.where(m_next, scores[h][2], NEG_INF)
        sx = scores[h][3]
        m = jnp.maximum(jnp.maximum(jnp.max(jnp.maximum(jnp.maximum(s1, s2), s3), axis=-1, keepdims=True),
                                    jnp.max(sx, axis=-1, keepdims=True)), sink)
        p1, p2, p3, px = jnp.exp(s1 - m), jnp.exp(s2 - m), jnp.exp(s3 - m), jnp.exp(sx - m)
        den = (jnp.sum(p1 + p2 + p3, axis=-1, keepdims=True) + jnp.sum(px, axis=-1, keepdims=True)
               + jnp.exp(sink - m))
        o = (_dot(p1.astype(BF16), vp_ref[:, col]) + _dot(p2.astype(BF16), vc_ref[:, col])
             + _dot(p3.astype(BF16), vn_ref[:, col]) + _dot(px.astype(BF16), vx_ref[:, col])) / den
        acc = o if h % 2 == 0 else acc + o
        if h % 2 == 1:
            o_ref[:, (h // 2) * LANES:(h // 2 + 1) * LANES] = acc.astype(o_ref.dtype)


def window_attention(q, kv, vv, sink, seq):
    n_batch, lc, gw = q.shape
    blk = ATT_BLOCK
    nb = seq // blk
    ctx = lc - seq
    wide = kv.shape[-1]
    qspec = pl.BlockSpec((None, blk, gw), lambda b, i, s: (b, i, 0))
    prev = pl.BlockSpec((None, blk, wide), lambda b, i, s: (b, jnp.clip(i - 1, 0, nb - 1), 0))
    cur = pl.BlockSpec((None, blk, wide), lambda b, i, s: (b, jnp.minimum(i, nb - 1), 0))
    nxt = pl.BlockSpec((None, blk, wide), lambda b, i, s: (b, jnp.clip(i + 1, 0, nb - 1), 0))
    cx = pl.BlockSpec((None, ctx, wide), lambda b, i, s: (b, seq // ctx, 0))
    return pl.pallas_call(
        functools.partial(_attn_kernel, nb), name="window_attn",
        out_shape=jax.ShapeDtypeStruct((n_batch, lc, gw), BF16),
        grid_spec=pltpu.PrefetchScalarGridSpec(
            num_scalar_prefetch=1, grid=(n_batch, lc // blk),
            in_specs=[qspec, prev, cur, nxt, cx, prev, cur, nxt, cx],
            out_specs=qspec),
        compiler_params=_cparams(("parallel", "arbitrary")),
    )(sink.astype(F32), q, kv, kv, kv, kv, vv, vv, vv, vv)


def _top2_rows(val, iota):
    m1 = jnp.max(val, axis=0, keepdims=True)
    i1 = jnp.min(jnp.where(val == m1, iota, SUBLANES), axis=0, keepdims=True)
    rest = jnp.where(iota == i1, -jnp.inf, val)
    m2 = jnp.max(rest, axis=0, keepdims=True)
    i2 = jnp.min(jnp.where(rest == m2, iota, SUBLANES), axis=0, keepdims=True)
    return m1, m2, i1, i2


def _outproj_kernel(ya_ref, yb_ref, yc_ref, yd_ref, x_ref, mod_ref, g_ref, wo_ref, wr_ref, rb_ref,
                    xo_ref, h_ref, e_ref, gw_ref):
    gw = GROUP_W
    tm = x_ref.shape[0] // OUTPROJ_SPLIT
    parts = [slice(part * tm, (part + 1) * tm) for part in range(OUTPROJ_SPLIT)]
    accs = []
    for rows in parts:
        acc = None
        for n, y_ref in enumerate((ya_ref, yb_ref, yc_ref, yd_ref)):
            part = _dot(y_ref[rows, :].astype(BF16), wo_ref[n * gw:(n + 1) * gw, :])
            acc = part if acc is None else acc + part
        accs.append(acc)
    for rows, acc in zip(parts, accs):
        _outproj_rows(rows, acc, x_ref, mod_ref, g_ref, wr_ref, rb_ref, xo_ref, h_ref, e_ref, gw_ref)


def _outproj_rows(rows, acc, x_ref, mod_ref, g_ref, wr_ref, rb_ref, xo_ref, h_ref, e_ref, gw_ref):
    d = x_ref.shape[-1]
    x = x_ref[rows, :] + mod_ref[:, 2 * d:3 * d] * acc
    xo_ref[rows, :] = x
    y = x * lax.rsqrt(jnp.mean(x * x, axis=-1, keepdims=True) + EPS) * g_ref[...]
    h = y * (1.0 + mod_ref[:, 4 * d:5 * d]) + mod_ref[:, 3 * d:4 * d]
    h_ref[rows, :] = _pack_bf16_halves(h)
    hh, hl = _split(h)
    wh, wl = _split(wr_ref[...])
    logits = _dot_nt(wh, hh) + _dot_nt(wh, hl) + _dot_nt(wl, hh)
    scores = _sigmoid(logits)
    sel = scores + rb_ref[:, rows]
    tm = x.shape[0]
    iota = lax.broadcasted_iota(jnp.int32, (EXPERTS_PER_GROUP, tm), 0)
    best_val, best = None, None
    for grp in range(N_EXPERT_GROUPS):
        m1, m2, _, _ = _top2_rows(sel[grp * EXPERTS_PER_GROUP:(grp + 1) * EXPERTS_PER_GROUP], iota)
        gs = m1 + m2
        if grp == 0:
            best_val, best = gs, jnp.zeros_like(gs, dtype=jnp.int32)
        else:
            upd = gs > best_val
            best_val = jnp.where(upd, gs, best_val)
            best = jnp.where(upd, grp, best)
    sel_in = jnp.zeros((EXPERTS_PER_GROUP, tm), F32)
    sc_in = jnp.zeros((EXPERTS_PER_GROUP, tm), F32)
    for grp in range(N_EXPERT_GROUPS):
        members = slice(grp * EXPERTS_PER_GROUP, (grp + 1) * EXPERTS_PER_GROUP)
        hit = best == grp
        sel_in = jnp.where(hit, sel[members], sel_in)
        sc_in = jnp.where(hit, scores[members], sc_in)
    _, _, i1, i2 = _top2_rows(sel_in, iota)
    w1 = jnp.sum(jnp.where(iota == i1, sc_in, 0.0), axis=0, keepdims=True)
    w2 = jnp.sum(jnp.where(iota == i2, sc_in, 0.0), axis=0, keepdims=True)
    tot = w1 + w2
    e_ref[0:1, rows] = best * EXPERTS_PER_GROUP + i1
    e_ref[1:2, rows] = best * EXPERTS_PER_GROUP + i2
    gw_ref[0:1, rows] = w1 / tot
    gw_ref[1:2, rows] = w2 / tot


def out_projection_router(ys, stream, mod, gain2, w_out, w_router, router_bias, seq):
    n_batch, lc, d = stream.shape
    gw = GROUP_W
    ctx_tile = seq // TM
    wr_t = w_router.T
    rb = jnp.broadcast_to(router_bias.astype(F32)[:, None], (N_EXPERTS, TM))
    row = lambda n: pl.BlockSpec((None, TM, n), lambda b, i: (b, i, 0))
    const = lambda a: pl.BlockSpec(a.shape, lambda b, i: (0,) * a.ndim)
    lane_row = pl.BlockSpec((None, TOP_K, TM), lambda b, i: (b, 0, i))
    return pl.pallas_call(
        _outproj_kernel, name="out_proj_router",
        out_shape=[jax.ShapeDtypeStruct((n_batch, lc, d), F32), jax.ShapeDtypeStruct((n_batch, lc, d // 2), jnp.uint32),
                   jax.ShapeDtypeStruct((n_batch, TOP_K, lc), jnp.int32),
                   jax.ShapeDtypeStruct((n_batch, TOP_K, lc), F32)],
        grid=(n_batch, lc // TM),
        in_specs=[row(gw)] * 4 + [row(d), pl.BlockSpec((None, 1, mod.shape[-1]), _mod_index(n_batch, ctx_tile)),
                                  pl.BlockSpec((1, d), lambda b, i: (0, 0)),
                                  pl.BlockSpec(w_out.shape, lambda b, i: (0, 0), pipeline_mode=pl.Buffered(1)),
                                  const(wr_t), const(rb)],
        out_specs=[row(d), row(d // 2), lane_row, lane_row],
        compiler_params=_cparams(("parallel", "arbitrary")),
    )(*ys, stream, mod, gain2.reshape(1, d), w_out, wr_t, rb)


def _expert_kernel(be_ref, nu_ref, x_ref, wg_ref, wu_ref, wd_ref, o_ref, wg_s, wu_s, wd_s):
    i = pl.program_id(0)
    prev = be_ref[jnp.maximum(i - 1, 0)]
    used = i < nu_ref[0]

    @pl.when(used & ((i == 0) | (be_ref[i] != prev)))
    def _():
        wg_s[...] = wg_ref[...].astype(BF16)
        wu_s[...] = wu_ref[...].astype(BF16)
        wd_s[...] = wd_ref[...].astype(BF16)

    @pl.when(used)
    def _():
        x_lo, x_hi = _unpack_bf16_halves(x_ref[...])
        n = x_lo.shape[1]
        a = _dot(x_lo, wg_s[:n, :]) + _dot(x_hi, wg_s[n:, :])
        u = _dot(x_lo, wu_s[:n, :]) + _dot(x_hi, wu_s[n:, :])
        hb = (a * _sigmoid(a)) * u
        o_ref[...] = _dot(hb.astype(BF16), wd_s[...])

    @pl.when(jnp.logical_not(used))
    def _():
        o_ref[...] = jnp.zeros_like(o_ref)


def expert_ffn(xs, block_e, n_used, wg, wu, wd, layer):
    rows, packed = xs.shape
    n_blocks = rows // MOE_BLOCK
    d, de = wg.shape[-2:]
    return pl.pallas_call(
        _expert_kernel, name="moe_experts",
        out_shape=jax.ShapeDtypeStruct((rows, d), F32),
        grid_spec=pltpu.PrefetchScalarGridSpec(
            num_scalar_prefetch=2, grid=(n_blocks,),
            in_specs=[pl.BlockSpec((MOE_BLOCK, packed), lambda i, be, nu: (i, 0)),
                      pl.BlockSpec((None, None, d, de), lambda i, be, nu: (layer, be[i], 0, 0)),
                      pl.BlockSpec((None, None, d, de), lambda i, be, nu: (layer, be[i], 0, 0)),
                      pl.BlockSpec((None, None, de, d), lambda i, be, nu: (layer, be[i], 0, 0))],
            out_specs=pl.BlockSpec((MOE_BLOCK, d), lambda i, be, nu: (i, 0)),
            scratch_shapes=[pltpu.VMEM((d, de), BF16), pltpu.VMEM((d, de), BF16), pltpu.VMEM((de, d), BF16)]),
        compiler_params=_cparams(("arbitrary",)),
    )(block_e, n_used, xs, wg, wu, wd)


RANK_WIDTHS = (1280, 1024, 768, 512, 256, 128)


def _rank_kernel(e_ref, tri_ref, rank_ref, count_ref, carry_ref):
    first = (pl.program_id(0) == 0) & (pl.program_id(1) == 0)

    @pl.when(first)
    def _():
        carry_ref[...] = jnp.zeros_like(carry_ref)

    width = e_ref.shape[-1]
    expert = lax.broadcasted_iota(jnp.int32, (N_EXPERTS, width), 0)
    carry = carry_ref[...]
    for k in range(TOP_K):
        onehot = jnp.where(expert == e_ref[k:k + 1, :], 1.0, 0.0)
        before = _dot(onehot.astype(BF16), tri_ref[...])
        rank_ref[k:k + 1, :] = jnp.sum(onehot * (before + carry), axis=0, keepdims=True).astype(jnp.int32)
        carry = carry + jnp.sum(onehot, axis=1, keepdims=True)
    carry_ref[...] = carry
    count_ref[...] = jnp.broadcast_to(carry, count_ref.shape)


def expert_ranks(eidx):
    n_batch, top_k, lc = eidx.shape
    width = next(w for w in RANK_WIDTHS if lc % w == 0)
    tri = jnp.asarray(np.triu(np.ones((width, width), np.float32), 1), BF16)
    blk = pl.BlockSpec((None, top_k, width), lambda b, i: (b, 0, i))
    rank, counts = pl.pallas_call(
        _rank_kernel, name="moe_rank",
        out_shape=[jax.ShapeDtypeStruct((n_batch, top_k, lc), jnp.int32),
                   jax.ShapeDtypeStruct((N_EXPERTS, LANES), F32)],
        grid=(n_batch, lc // width),
        in_specs=[blk, pl.BlockSpec(tri.shape, lambda b, i: (0, 0))],
        out_specs=[blk, pl.BlockSpec((N_EXPERTS, LANES), lambda b, i: (0, 0))],
        scratch_shapes=[pltpu.VMEM((N_EXPERTS, 1), F32)],
        compiler_params=_cparams(("arbitrary", "arbitrary")),
    )(eidx, tri)
    return rank, counts[:, 0].astype(jnp.int32)


def _combine_kernel(x_ref, y0_ref, y1_ref, g0_ref, g1_ref, mod_ref, o_ref):
    d = x_ref.shape[-1]
    y = y0_ref[...] * g0_ref[...] + y1_ref[...] * g1_ref[...]
    o_ref[...] = x_ref[...] + mod_ref[:, 5 * d:6 * d] * y


def moe_combine(stream, y0, y1, g0, g1, mod, seq, n_rows):
    n_batch, lc, d = stream.shape
    row = pl.BlockSpec((None, TM, d), lambda b, i: (b, i, 0))
    col = pl.BlockSpec((None, TM, 1), lambda b, i: (b, i, 0))
    return pl.pallas_call(
        _combine_kernel, name="moe_combine",
        out_shape=jax.ShapeDtypeStruct((n_batch, n_rows, d), F32),
        grid=(n_batch, n_rows // TM),
        in_specs=[row, row, row, col, col, pl.BlockSpec((None, 1, mod.shape[-1]), _mod_index(n_batch, seq // TM))],
        out_specs=row,
        compiler_params=_cparams(("parallel", "arbitrary")),
    )(stream, y0, y1, g0, g1, mod)


def moe_ffn(stream, h2, eidx, gates, mod, wg, wu, wd, layer, seq, n_rows):
    n_batch, lc, d = stream.shape
    n_tok = n_batch * lc
    n_assign = n_tok * TOP_K
    rank, counts = expert_ranks(eidx)
    padded = (counts + MOE_BLOCK - 1) // MOE_BLOCK * MOE_BLOCK
    pad_end = jnp.cumsum(padded)
    experts = jnp.arange(N_EXPERTS, dtype=jnp.int32)
    start = jnp.sum(jnp.where(eidx[..., None] == experts, pad_end - padded, 0), axis=-1)
    dest = start + rank
    n_blocks = n_assign // MOE_BLOCK + N_EXPERTS
    tok = jnp.broadcast_to((jnp.arange(n_batch, dtype=jnp.int32) * lc)[:, None, None]
                           + jnp.arange(lc, dtype=jnp.int32)[None, None, :], dest.shape)
    filler = jnp.arange(n_blocks * MOE_BLOCK, dtype=jnp.int32) % n_tok
    row_tok = filler.at[dest.reshape(-1)].set(tok.reshape(-1))
    block_row = jnp.arange(n_blocks, dtype=jnp.int32) * MOE_BLOCK
    block_e = jnp.minimum(jnp.sum((pad_end[None, :] <= block_row[:, None]).astype(jnp.int32), axis=1), N_EXPERTS - 1)
    n_used = (pad_end[-1:] // MOE_BLOCK).astype(jnp.int32)
    xs = h2.reshape(n_tok, h2.shape[-1])[row_tok]
    ys = expert_ffn(xs, block_e, n_used, wg, wu, wd, layer)
    y0 = ys[dest[:, 0].reshape(-1)].reshape(n_batch, lc, d)
    y1 = ys[dest[:, 1].reshape(-1)].reshape(n_batch, lc, d)
    g0 = gates[:, 0].reshape(n_batch, lc, 1)
    g1 = gates[:, 1].reshape(n_batch, lc, 1)
    return moe_combine(stream, y0, y1, g0, g1, mod, seq, n_rows)


def kernel(x, c, ctx, c_ctx, norm1, norm2, w_ada, b_ada, w_in, w_out, rwkv_conv, rwkv_w0, rwkv_w_up, rwkv_a0,
           rwkv_a_up, rwkv_g_up, rwkv_k_k, rwkv_k_a, rwkv_r_k, rwkv_ln_w, rwkv_ln_b, rwkv_v0, rwkv_v_down,
           rwkv_v_up, fourier_w, fourier_b, gmlp_norm, gmlp_ws, gmlp_bs, attn_q_norm, attn_k_norm, attn_sink,
           w_router, router_bias, w_e_gate, w_e_up, w_e_down):
    n_batch, seq, d = x.shape
    depth = w_in.shape[0]
    stream = jnp.concatenate([x, ctx], axis=1)
    c_all = jnp.zeros((SUBLANES, d), F32).at[:n_batch].set(c).at[n_batch].set(c_ctx)
    col_splits = np.cumsum([A_COLS, B_COLS, C_COLS])
    v_first = None
    for l in range(depth):
        mod = ada_modulation(c_all, w_ada, b_ada, l).reshape(SUBLANES, 1, 6 * d)
        w_parts = [w.astype(BF16) for w in jnp.split(w_in[l], col_splits, axis=1)]
        pa, pb, pc, pd = in_projection(stream, mod, norm1[l], w_parts, seq)

        vres = None if l == 0 else (rwkv_v0[l - 1], rwkv_v_down[l - 1], rwkv_v_up[l - 1])
        r, kk, v, k0, wd0, kka0, k1, wd1, kka1, g = rwkv_prepare(
            pa, seq, rwkv_conv[l], rwkv_w0[l], rwkv_w_up[l], rwkv_a0[l], rwkv_a_up[l], rwkv_g_up[l],
            rwkv_k_k[l], rwkv_k_a[l], v_first, vres)
        if l == 0:
            v_first = v
        yf, yb = rwkv_scan(r, kk, v, ((k0, wd0, kka0), (k1, wd1, kka1)), seq)
        y_a = rwkv_readout(yf, yb, r, k0, k1, v, g, rwkv_r_k[l].reshape(-1), rwkv_ln_w[l], rwkv_ln_b[l])

        y_b = fourier_mix(pb, seq, fourier_w[l], fourier_b[l])
        y_c = gmlp_mix(pc, gmlp_norm[l], gmlp_ws[l], gmlp_bs[l])
        q, kv, vv = attention_prepare(pd, seq, attn_q_norm[l], attn_k_norm[l])
        y_d = window_attention(q, kv, vv, attn_sink[l], seq)

        stream, h2, eidx, gates = out_projection_router(
            (y_a, y_b, y_c, y_d), stream, mod, norm2[l], w_out[l].astype(BF16), w_router, router_bias, seq)
        n_rows = seq if l == depth - 1 else stream.shape[1]
        stream = moe_ffn(stream, h2, eidx, gates, mod, w_e_gate, w_e_up, w_e_down, l, seq, n_rows)
    return stream
```

```python
import functools

import numpy as np
import jax
import jax.numpy as jnp
from jax import lax
from jax.experimental import pallas as pl
from jax.experimental.pallas import tpu as pltpu

F32 = jnp.float32
BF16 = jnp.bfloat16

GRID_W = 64
EPS = 1e-6
NEG_INF = -1e30
GROUP_W = 512
RWKV_HEAD = 64
DECAY_LORA = 64
ICLR_LORA = 64
VRES_LORA = 32
RWKV_GN_EPS = 64e-5
FOURIER_GROUPS = 4
FOURIER_CH = 128
GMLP_GROUPS = 4
GMLP_CH = 128
GMLP_CHUNK = 128
ATT_HEAD = 64
ATT_Q_HEADS = 8
ATT_KV_HEADS = 2
ATT_BLOCK = 128
ROPE_BASE = 10000.0
N_EXPERTS = 64
N_EXPERT_GROUPS = 8
EXPERTS_PER_GROUP = 8
TOP_K = 2
MOE_BLOCK = 128
A_COLS = 1920
B_COLS = 512
C_COLS = 1024

LANES = 128
SUBLANES = 8
TM = 256
SCAN_TB = 128
SCAN_GROUP = SUBLANES
SCAN_TILES = 4
OUTPROJ_SPLIT = 2
VMEM_LIMIT = 56 * 1024 * 1024


def _cparams(sem, vmem=VMEM_LIMIT):
    return pltpu.CompilerParams(dimension_semantics=sem, vmem_limit_bytes=vmem)


def _dot(a, b):
    return jnp.dot(a, b, preferred_element_type=F32)


def _dot_nt(a, b):
    return lax.dot_general(a, b, (((1,), (1,)), ((), ())), preferred_element_type=F32)


def _split(x):
    hi = x.astype(BF16)
    lo = (x - hi.astype(F32)).astype(BF16)
    return hi, lo


def _pack_bf16_halves(x):
    n = x.shape[1] // 2
    lo = lax.bitcast_convert_type(x[:, :n].astype(BF16).astype(F32), jnp.uint32) >> 16
    hi = lax.bitcast_convert_type(x[:, n:].astype(BF16).astype(F32), jnp.uint32) & jnp.uint32(0xFFFF0000)
    return hi | lo


def _unpack_bf16_halves(w):
    lo = lax.bitcast_convert_type(w << 16, F32).astype(BF16)
    hi = lax.bitcast_convert_type(w & jnp.uint32(0xFFFF0000), F32).astype(BF16)
    return lo, hi


def _dot1(a, b):
    return _dot(a.astype(BF16), b.astype(BF16))


def _segsum(x, seg):
    xh, xl = _split(x)
    return _dot(xh, seg) + _dot(xl, seg)


def _sigmoid(x):
    return 1.0 / (1.0 + jnp.exp(-x))


def _seg_matrix(width, seg):
    i = np.arange(width) // seg
    return jnp.asarray((i[:, None] == i[None, :]).astype(np.float32), dtype=BF16)


def _ada_kernel(c_ref, w_ref, b_ref, o_ref):
    c = c_ref[...]
    a = (c * _sigmoid(c)).astype(BF16)
    o_ref[...] = _dot(a, w_ref[...].astype(BF16)) + b_ref[...]


def ada_modulation(c_all, w_ada, b_ada, layer, tn=1024):
    rows, d = c_all.shape
    n_layers, _, n = w_ada.shape
    return pl.pallas_call(
        _ada_kernel, name="ada_mod",
        out_shape=jax.ShapeDtypeStruct((rows, n), F32),
        grid=(n // tn,),
        in_specs=[pl.BlockSpec((rows, d), lambda j: (0, 0)),
                  pl.BlockSpec((None, d, tn), lambda j: (layer, 0, j)),
                  pl.BlockSpec((None, 1, tn), lambda j: (layer, 0, j))],
        out_specs=pl.BlockSpec((rows, tn), lambda j: (0, j)),
        compiler_params=_cparams(("arbitrary",)),
    )(c_all, w_ada, b_ada.reshape(n_layers, 1, n))


def _inproj_kernel(x_ref, mod_ref, g_ref, wa_ref, wb_ref, wc_ref, wd_ref, pa_ref, pb_ref, pc_ref, pd_ref):
    d = x_ref.shape[-1]
    x = x_ref[...]
    y = x * lax.rsqrt(jnp.mean(x * x, axis=-1, keepdims=True) + EPS) * g_ref[...]
    sh = mod_ref[:, 0:d]
    sc = mod_ref[:, d:2 * d]
    h = (y * (1.0 + sc) + sh).astype(BF16)
    pa_ref[...] = _dot(h, wa_ref[...])
    pb_ref[...] = _dot(h, wb_ref[...])
    pc_ref[...] = _dot(h, wc_ref[...])
    pd_ref[...] = _dot(h, wd_ref[...])


def _mod_index(n_batch, ctx_tile):
    return lambda b, i: (jnp.where(i >= ctx_tile, n_batch, b), 0, 0)


def in_projection(stream, mod, gain, w_parts, seq):
    n_batch, lc, d = stream.shape
    ctx_tile = seq // TM
    resident = lambda w: pl.BlockSpec(w.shape, lambda b, i: (0, 0), pipeline_mode=pl.Buffered(1))
    row = lambda n: pl.BlockSpec((None, TM, n), lambda b, i: (b, i, 0))
    return pl.pallas_call(
        _inproj_kernel, name="in_proj",
        out_shape=[jax.ShapeDtypeStruct((n_batch, lc, w.shape[1]), F32) for w in w_parts],
        grid=(n_batch, lc // TM),
        in_specs=[row(d),
                  pl.BlockSpec((None, 1, mod.shape[-1]), _mod_index(n_batch, ctx_tile)),
                  pl.BlockSpec((1, d), lambda b, i: (0, 0))] + [resident(w) for w in w_parts],
        out_specs=[row(w.shape[1]) for w in w_parts],
        compiler_params=_cparams(("parallel", "arbitrary")),
    )(stream, mod, gain.reshape(1, d), *w_parts)


def _rwkv_prep_kernel(has_vres, ctx_tile, *refs):
    (pa_ref, hp_ref, hn_ref, conv_ref, w0_ref, wup_ref, a0_ref, aup_ref, gup_ref, kk_par_ref, ka_par_ref,
     seg_ref) = refs[:12]
    rest = refs[12:]
    if has_vres:
        vf_ref, v0_ref, vdn_ref, vup_ref = rest[:4]
        rest = rest[4:]
    r_o, kk_o, v_o, k0_o, w0_o, kka0_o, k1_o, w1_o, kka1_o, g_o = rest
    gw = GROUP_W
    x = pa_ref[...]
    tm = x.shape[0]
    i = pl.program_id(1)
    first = (i == 0) | (i == ctx_tile)
    last = (i == ctx_tile - 1) | (i == pl.num_programs(1) - 1)
    halo_prev = jnp.where(first, 0.0, hp_ref[SUBLANES - 1:SUBLANES, :])
    halo_next = jnp.where(last, 0.0, hn_ref[0:1, :])
    row = lax.broadcasted_iota(jnp.int32, x.shape, 0)
    x_prev = jnp.where(row == 0, halo_prev, pltpu.roll(x, 1, 0))
    x_next = jnp.where(row == tm - 1, halo_next, pltpu.roll(x, tm - 1, 0))
    y = x_prev * conv_ref[0:1, :] + x * conv_ref[1:2, :] + x_next * conv_ref[2:3, :]
    r = y[:, 0:gw]
    k = y[:, gw:2 * gw]
    v = y[:, 2 * gw:3 * gw]
    wl = y[:, 3 * gw:3 * gw + 2 * DECAY_LORA]
    al = y[:, 3 * gw + 2 * DECAY_LORA:3 * gw + 2 * DECAY_LORA + 2 * ICLR_LORA]
    gl = y[:, 3 * gw + 2 * DECAY_LORA + 2 * ICLR_LORA:]
    if has_vres:
        mix = _sigmoid(v0_ref[...] + _dot1(_dot1(v, vdn_ref[...]), vup_ref[...]))
        v = v + (vf_ref[...] - v) * mix
    kk = k * kk_par_ref[...]
    nrm = jnp.maximum(jnp.sqrt(_segsum(kk * kk, seg_ref[...])), 1e-12)
    kk = kk / nrm
    r_o[...] = r
    kk_o[...] = kk
    v_o[...] = v
    tanh_wl = jnp.tanh(wl)
    for d, (k_o, w_o, kka_o) in enumerate(((k0_o, w0_o, kka0_o), (k1_o, w1_o, kka1_o))):
        w_raw = w0_ref[d:d + 1, :] + _dot1(tanh_wl, wup_ref[d])
        z = -w_raw
        softplus = jnp.maximum(z, 0.0) + jnp.log(1.0 + jnp.exp(-jnp.abs(z)))
        w_o[...] = jnp.exp(-jnp.exp(-softplus - 0.5))
        a = _sigmoid(a0_ref[d:d + 1, :] + _dot1(al, aup_ref[d]))
        k_o[...] = k * (1.0 + (a - 1.0) * ka_par_ref[...])
        kka_o[...] = kk * a
    g_o[...] = _dot1(_sigmoid(gl), gup_ref[...])


def rwkv_prepare(pa, seq, conv_w, w0, w_up, a0, a_up, g_up, k_k, k_a, v_first=None, vres=None):
    n_batch, lc, c = pa.shape
    gw = GROUP_W
    sub_per_tile = TM // SUBLANES
    n_sub = lc // SUBLANES
    pad_rows = lambda w, d, n: jnp.zeros((2 * n, gw), F32).at[d * n:(d + 1) * n].set(w)
    wup = jnp.stack([pad_rows(w_up[d], d, DECAY_LORA) for d in range(2)])
    aup = jnp.stack([pad_rows(a_up[d], d, ICLR_LORA) for d in range(2)])
    seg = _seg_matrix(gw, RWKV_HEAD)
    const = lambda a: pl.BlockSpec(a.shape, lambda b, i: (0,) * a.ndim)
    row = lambda n: pl.BlockSpec((None, TM, n), lambda b, i: (b, i, 0))
    halo_prev = pl.BlockSpec((None, SUBLANES, c), lambda b, i: (b, jnp.maximum(i * sub_per_tile - 1, 0), 0))
    halo_next = pl.BlockSpec((None, SUBLANES, c), lambda b, i: (b, jnp.minimum((i + 1) * sub_per_tile, n_sub - 1), 0))
    args = [pa, pa, pa, conv_w, w0, wup, a0, aup, g_up, k_k.reshape(1, gw), k_a.reshape(1, gw), seg]
    specs = [row(c), halo_prev, halo_next] + [const(a) for a in args[3:]]
    has_vres = vres is not None
    if has_vres:
        v0, v_down, v_up = vres
        vdn = jnp.zeros((gw, LANES), F32).at[:, :VRES_LORA].set(v_down)
        vup = jnp.zeros((LANES, gw), F32).at[:VRES_LORA].set(v_up)
        extra = [v_first, v0.reshape(1, gw), vdn, vup]
        args += extra
        specs += [row(gw)] + [const(a) for a in extra[1:]]
    return pl.pallas_call(
        functools.partial(_rwkv_prep_kernel, has_vres, seq // TM), name="rwkv_prep",
        out_shape=[jax.ShapeDtypeStruct((n_batch, lc, gw), F32)] * 10,
        grid=(n_batch, lc // TM),
        in_specs=specs,
        out_specs=[row(gw)] * 10,
        compiler_params=_cparams(("parallel", "arbitrary")),
    )(*args)


def _scan_kernel(n_batch, rf, kkf, vf, kf, wf, kaf, rb, kkb, vb, kb, wb, kab, bsel_ref, segbd_ref, ysel_ref,
                 yf_ref, yb_ref, s_ref, vk_ref, yt_ref):
    tb = rf.shape[1]
    n_pairs = GROUP_W // LANES
    half = RWKV_HEAD

    @pl.when(pl.program_id(0) == 0)
    def _():
        s_ref[...] = jnp.zeros_like(s_ref)
        yt_ref[...] = jnp.zeros_like(yt_ref)

    segbd = segbd_ref[...]
    dirs = ((rf, kkf, vf, kf, wf, kaf, yf_ref), (rb, kkb, vb, kb, wb, kab, yb_ref))
    chains = [(d, b, p) for d in range(2) for b in range(n_batch) for p in range(n_pairs)]
    n_ch = len(chains)
    lane = lax.broadcasted_iota(jnp.int32, (RWKV_HEAD, LANES), 1) % half

    n_sets = 2
    per_set = n_ch // n_sets
    sets = [list(range(si * per_set, (si + 1) * per_set)) for si in range(n_sets)]

    def side_by_side(parts):
        return jnp.concatenate([jnp.concatenate(parts[i:i + 2], axis=1) for i in range(0, per_set, 2)], axis=0)

    def piece(full, n):
        return full[(n // 2) * RWKV_HEAD:(n // 2 + 1) * RWKV_HEAD, (n % 2) * LANES:(n % 2 + 1) * LANES]

    tile_w = SCAN_GROUP * LANES

    def group(g, carry):
        bases = [(pl.multiple_of((g * SCAN_TILES + u) * SCAN_GROUP, SCAN_GROUP),
                  pl.multiple_of(tb - SCAN_GROUP - (g * SCAN_TILES + u) * SCAN_GROUP, SCAN_GROUP))
                 for u in range(SCAN_TILES)]

        def row(which, ci, u, j):
            d, b, p = chains[ci]
            t = j if d == 0 else SCAN_GROUP - 1 - j
            return dirs[d][which][b, pl.ds(bases[u][d], SCAN_GROUP), p * LANES:(p + 1) * LANES][t:t + 1]

        def sk_dot(states_bf, cs, u, j):
            return _dot(side_by_side([states_bf[n] * row(1, ci, u, j).astype(BF16) for n, ci in enumerate(cs)]),
                        segbd)

        def outer_products(cs, u):
            for ci in cs:
                d, b, p = chains[ci]
                v8 = dirs[d][2][b, pl.ds(bases[u][d], SCAN_GROUP), p * LANES:(p + 1) * LANES]
                k8 = dirs[d][3][b, pl.ds(bases[u][d], SCAN_GROUP), p * LANES:(p + 1) * LANES]
                vt = jnp.concatenate([v8[:, :RWKV_HEAD], v8[:, RWKV_HEAD:]], axis=0).T.astype(BF16)
                kmat = jnp.concatenate([bsel_ref[t] * k8[t:t + 1] for t in range(SCAN_GROUP)],
                                       axis=1).astype(BF16)
                vk_ref[ci, :, u * tile_w:(u + 1) * tile_w] = _dot(vt, kmat)

        sk = []
        for cs in sets:
            outer_products(cs, 0)
            sk.append(sk_dot([s_ref[ci].astype(BF16) for ci in cs], cs, 0, 0))
        for u in range(SCAN_TILES):
            y_tile = [None] * n_sets
            for j in range(SCAN_GROUP):
                for si, cs in enumerate(sets):
                    d = chains[cs[0]][0]
                    t = j if d == 0 else SCAN_GROUP - 1 - j
                    states_bf = []
                    for n, ci in enumerate(cs):
                        s = (s_ref[ci] * row(4, ci, u, j) - piece(sk[si], n) * row(5, ci, u, j)
                             + vk_ref[ci, :, u * tile_w + t * LANES:u * tile_w + (t + 1) * LANES])
                        s_ref[ci] = s
                        states_bf.append(s.astype(BF16))
                    y_lhs = side_by_side([s * row(0, ci, u, j).astype(BF16) for s, ci in zip(states_bf, cs)])
                    y_step = _dot(y_lhs, ysel_ref[t])
                    y_tile[si] = y_step if j == 0 else y_tile[si] + y_step
                    if j + 1 < SCAN_GROUP:
                        sk[si] = sk_dot(states_bf, cs, u, j + 1)
                    elif u + 1 < SCAN_TILES:
                        sk[si] = sk_dot(states_bf, cs, u + 1, 0)
                    if j == SCAN_GROUP // 2 and u + 1 < SCAN_TILES:
                        outer_products(cs, u + 1)
            for si, cs in enumerate(sets):
                d = chains[cs[0]][0]
                first = bases[u][d] % half
                for n, ci in enumerate(cs):
                    moved = pltpu.roll(piece(y_tile[si], n), first, 1)
                    yt_ref[ci] = jnp.where(lane // SCAN_GROUP == first // SCAN_GROUP, moved, yt_ref[ci])
        return carry

    def flush(first_row):
        for ci, (d, b, p) in enumerate(chains):
            t = yt_ref[ci].T
            rows = slice(first_row[d], first_row[d] + half)
            dirs[d][6][b, rows, p * LANES:(p + 1) * LANES] = jnp.concatenate([t[:half], t[half:]], axis=1)

    groups_per_flush = half // (SCAN_GROUP * SCAN_TILES)
    for part in range(tb // half):
        lax.fori_loop(part * groups_per_flush, (part + 1) * groups_per_flush, group, 0)
        flush((part * half, tb - (part + 1) * half))


def _scan_selectors():
    bsel = np.zeros((SCAN_GROUP, 2 * SCAN_GROUP, LANES), np.float32)
    for h in range(2):
        for j in range(SCAN_GROUP):
            bsel[j, h * SCAN_GROUP + j, h * RWKV_HEAD:(h + 1) * RWKV_HEAD] = 1.0
    block = np.arange(2 * LANES) // RWKV_HEAD
    ysel = np.zeros((SCAN_GROUP, 2 * LANES, 2 * LANES), np.float32)
    for t in range(SCAN_GROUP):
        ysel[t] = (block[:, None] == block[None, :]) & ((np.arange(2 * LANES) % RWKV_HEAD) == t)[None, :]
    return jnp.asarray(bsel, F32), _seg_matrix(2 * LANES, RWKV_HEAD), jnp.asarray(ysel, BF16)


def rwkv_scan(r, kk, v, dirs, seq):
    n_batch, lc, gw = r.shape
    tb = SCAN_TB
    nx = seq // tb
    nc = (lc - seq) // tb
    n_chains = 2 * n_batch * (gw // LANES)
    fwd = lambda i: (0, jnp.where(i < nc, nx + i, i - nc), 0)
    bwd = lambda i: (0, nx + nc - 1 - i, 0)
    blk = lambda m: pl.BlockSpec((n_batch, tb, gw), m)
    const = lambda a: pl.BlockSpec(a.shape, lambda i: (0,) * a.ndim)
    bsel, segbd, ysel = _scan_selectors()
    return pl.pallas_call(
        functools.partial(_scan_kernel, n_batch), name="rwkv_scan",
        out_shape=[jax.ShapeDtypeStruct((n_batch, lc, gw), F32)] * 2,
        grid=(nx + nc,),
        in_specs=[blk(fwd)] * 6 + [blk(bwd)] * 6 + [const(bsel), const(segbd), const(ysel)],
        out_specs=[blk(fwd), blk(bwd)],
        scratch_shapes=[pltpu.VMEM((n_chains, RWKV_HEAD, LANES), F32),
                        pltpu.VMEM((n_chains, RWKV_HEAD, SCAN_TILES * SCAN_GROUP * LANES), F32),
                        pltpu.VMEM((n_chains, RWKV_HEAD, LANES), F32)],
        compiler_params=_cparams(("arbitrary",)),
    )(r, kk, v, *dirs[0], r, kk, v, *dirs[1], bsel, segbd, ysel)


def _rwkv_readout_kernel(yf_ref, yb_ref, r_ref, k0_ref, k1_ref, v_ref, g_ref, rk_ref, lnw_ref, lnb_ref, seg_ref,
                         o_ref):
    seg = seg_ref[...]
    inv = 1.0 / RWKV_HEAD
    y = yf_ref[...] + yb_ref[...]
    mu = _segsum(y, seg) * inv
    dlt = y - mu
    var = _segsum(dlt * dlt, seg) * inv
    yn = dlt * lax.rsqrt(var + RWKV_GN_EPS) * lnw_ref[...] + lnb_ref[...]
    kbar = 0.5 * (k0_ref[...] + k1_ref[...])
    bonus = _segsum(r_ref[...] * kbar * rk_ref[...], seg) * v_ref[...]
    o_ref[...] = ((yn + bonus) * g_ref[...]).astype(o_ref.dtype)


def rwkv_readout(yf, yb, r, k0, k1, v, g, r_k, ln_w, ln_b):
    n_batch, lc, gw = r.shape
    seg = _seg_matrix(gw, RWKV_HEAD)
    row = pl.BlockSpec((None, TM, gw), lambda b, i: (b, i, 0))
    const = lambda a: pl.BlockSpec(a.shape, lambda b, i: (0,) * a.ndim)
    params = [r_k.reshape(1, gw), ln_w.reshape(1, gw), ln_b.reshape(1, gw), seg]
    return pl.pallas_call(
        _rwkv_readout_kernel, name="rwkv_readout",
        out_shape=jax.ShapeDtypeStruct((n_batch, lc, gw), BF16),
        grid=(n_batch, lc // TM),
        in_specs=[row] * 7 + [const(a) for a in params],
        out_specs=row,
        compiler_params=_cparams(("parallel", "arbitrary")),
    )(yf, yb, r, k0, k1, v, g, *params)


def _dft_tables(n):
    k = np.arange(n)
    ang = 2.0 * np.pi * ((k[:, None] * k[None, :]) % n) / n
    return np.cos(ang), np.sin(ang)


def _fft_a_kernel(nb, f_ref, gr_ref, gi_ref, zr_ref, zi_ref):
    n1 = gr_ref.shape[1]
    n2_total = f_ref.shape[0] // n1
    j = pl.program_id(2)
    for i in range(nb):
        rows = f_ref[pl.ds(j * nb + i, n1, stride=n2_total), :].astype(BF16)
        zr_ref[i * n1:(i + 1) * n1, :] = _dot(gr_ref[i], rows)
        zi_ref[i * n1:(i + 1) * n1, :] = _dot(gi_ref[i], rows)


def _fft_b_kernel(kb, scale, zr_ref, zi_ref, fc_ref, fs_ref, cc_ref, cs_ref, w_ref, b_ref, o_ref):
    n2 = fc_ref.shape[0]
    n1 = zr_ref.shape[0] // n2
    j = pl.program_id(2)
    fc, fs = fc_ref[...], fs_ref[...]
    w = w_ref[...].astype(BF16)
    xs = []
    for i in range(kb):
        zr = zr_ref[pl.ds(j * kb + i, n2, stride=n1), :].astype(BF16)
        zi = zi_ref[pl.ds(j * kb + i, n2, stride=n1), :].astype(BF16)
        xs.append((_dot(fc, zr) + _dot(fs, zi), _dot(fc, zi) - _dot(fs, zr)))
    res = [(_dot(xr.astype(BF16), cc_ref[...]) + _dot(xi.astype(BF16), cs_ref[...])) * scale for xr, xi in xs]
    for i in range(kb):
        o_ref[pl.ds(j * kb + i, n2, stride=n1), :] = _dot(res[i].astype(BF16), w) + b_ref[...]


def _dft_dense_kernel(scale, f_ref, fc_ref, fs_ref, cc_ref, cs_ref, w_ref, b_ref, o_ref):
    f = f_ref[...].astype(BF16)
    xr = _dot(fc_ref[...], f)
    xi = -_dot(fs_ref[...], f)
    re = (_dot(xr.astype(BF16), cc_ref[...]) + _dot(xi.astype(BF16), cs_ref[...])) * scale
    o_ref[...] = _dot(re.astype(BF16), w_ref[...].astype(BF16)) + b_ref[...]


def fourier_mix(pb, seq, w, bias):
    n_batch, lc, gw = pb.shape
    ctx = lc - seq
    ch = FOURIER_CH
    groups = FOURIER_GROUPS
    n2 = LANES
    n1 = seq // n2
    cc, cs = (jnp.asarray(t, BF16) for t in _dft_tables(ch))
    bias3 = bias.reshape(groups, 1, ch)

    k1 = np.arange(n1)[None, :, None]
    m1 = np.arange(n1)[None, None, :]
    m2 = np.arange(n2)[:, None, None]
    ang = 2.0 * np.pi * ((k1 * (n2 * m1 + m2)) % seq) / seq
    g_re = jnp.asarray(np.cos(ang), BF16)
    g_im = jnp.asarray(-np.sin(ang), BF16)
    nb = min(16, n2)
    slab = pl.BlockSpec((None, seq, ch), lambda b, g, j: (b, 0, g))
    gspec = pl.BlockSpec((nb, n1, n1), lambda b, g, j: (j, 0, 0))
    zspec = pl.BlockSpec((None, None, nb * n1, ch), lambda b, g, j: (b, g, j, 0))
    zr, zi = pl.pallas_call(
        functools.partial(_fft_a_kernel, nb), name="fft_stage_a",
        out_shape=[jax.ShapeDtypeStruct((n_batch, groups, seq, ch), F32)] * 2,
        grid=(n_batch, groups, n2 // nb),
        in_specs=[slab, gspec, gspec],
        out_specs=[zspec, zspec],
        compiler_params=_cparams(("parallel", "parallel", "arbitrary")),
    )(pb, g_re, g_im)

    fc, fs = (jnp.asarray(t, BF16) for t in _dft_tables(n2))
    kb = min(16, n1)
    scale = 1.0 / np.sqrt(float(seq) * ch)
    zslab = pl.BlockSpec((None, None, seq, ch), lambda b, g, j: (b, g, 0, 0), pipeline_mode=pl.Buffered(1))
    const = lambda a: pl.BlockSpec(a.shape, lambda b, g, j: (0,) * a.ndim)
    wspec = pl.BlockSpec((None, ch, ch), lambda b, g, j: (g, 0, 0))
    bspec = pl.BlockSpec((None, 1, ch), lambda b, g, j: (g, 0, 0))
    y_x = pl.pallas_call(
        functools.partial(_fft_b_kernel, kb, scale), name="fft_stage_b",
        out_shape=jax.ShapeDtypeStruct((n_batch, seq, gw), F32),
        grid=(n_batch, groups, n1 // kb),
        in_specs=[zslab, zslab, const(fc), const(fs), const(cc), const(cs), wspec, bspec],
        out_specs=pl.BlockSpec((None, seq, ch), lambda b, g, j: (b, 0, g)),
        compiler_params=_cparams(("parallel", "parallel", "arbitrary")),
    )(zr, zi, fc, fs, cc, cs, w, bias3)

    fcc, fsc = (jnp.asarray(t, BF16) for t in _dft_tables(ctx))
    const2 = lambda a: pl.BlockSpec(a.shape, lambda b, g: (0,) * a.ndim)
    ctx_tile = seq // ctx
    y_c = pl.pallas_call(
        functools.partial(_dft_dense_kernel, 1.0 / np.sqrt(float(ctx) * ch)), name="dft_ctx",
        out_shape=jax.ShapeDtypeStruct((n_batch, ctx, gw), F32),
        grid=(n_batch, groups),
        in_specs=[pl.BlockSpec((None, ctx, ch), lambda b, g: (b, ctx_tile, g)),
                  const2(fcc), const2(fsc), const2(cc), const2(cs),
                  pl.BlockSpec((None, ch, ch), lambda b, g: (g, 0, 0)),
                  pl.BlockSpec((None, 1, ch), lambda b, g: (g, 0, 0))],
        out_specs=pl.BlockSpec((None, ctx, ch), lambda b, g: (b, 0, g)),
        compiler_params=_cparams(("parallel", "arbitrary")),
    )(pb, fcc, fsc, cc, cs, w, bias3)
    return jnp.concatenate([y_x, y_c], axis=1)


def _gmlp_kernel(pc_ref, norm_ref, ws_ref, bs_ref, o_ref):
    gw = GROUP_W
    x = pc_ref[...]
    z = 0.5 * x * (1.0 + jnp.tanh(0.7978845608028654 * (x + 0.044715 * (x * x * x))))
    tm = x.shape[0]
    for g in range(GMLP_GROUPS):
        u = z[:, g * GMLP_CH:(g + 1) * GMLP_CH]
        v = z[:, gw + g * GMLP_CH:gw + (g + 1) * GMLP_CH]
        v = v * lax.rsqrt(jnp.mean(v * v, axis=-1, keepdims=True) + EPS) * norm_ref[g:g + 1, :]
        v = v.astype(BF16)
        ws = ws_ref[g].astype(BF16)
        for c in range(tm // GMLP_CHUNK):
            rows = slice(c * GMLP_CHUNK, (c + 1) * GMLP_CHUNK)
            f = _dot(ws, v[rows]) + bs_ref[g]
            o_ref[rows, g * GMLP_CH:(g + 1) * GMLP_CH] = (u[rows] * f).astype(o_ref.dtype)


def gmlp_mix(pc, norm_g, ws, bs):
    n_batch, lc, c = pc.shape
    bs_b = jnp.broadcast_to(bs[:, :, None], bs.shape + (GMLP_CH,))
    const = lambda a: pl.BlockSpec(a.shape, lambda b, i: (0,) * a.ndim)
    return pl.pallas_call(
        _gmlp_kernel, name="gmlp",
        out_shape=jax.ShapeDtypeStruct((n_batch, lc, GROUP_W), BF16),
        grid=(n_batch, lc // TM),
        in_specs=[pl.BlockSpec((None, TM, c), lambda b, i: (b, i, 0)), const(norm_g), const(ws), const(bs_b)],
        out_specs=pl.BlockSpec((None, TM, GROUP_W), lambda b, i: (b, i, 0)),
        compiler_params=_cparams(("parallel", "arbitrary")),
    )(pc, norm_g, ws, bs_b)


def _rope_tables(seq, ctx):
    half = ATT_HEAD // 4
    inv_freq = ROPE_BASE ** (-np.arange(half, dtype=np.float64) / half)
    pos = np.arange(seq)
    ang_r = (pos // GRID_W)[:, None] * inv_freq[None, :]
    ang_c = (pos % GRID_W)[:, None] * inv_freq[None, :]
    cos = np.concatenate([np.cos(ang_r)] * 2 + [np.cos(ang_c)] * 2, axis=1)
    sin = np.concatenate([-np.sin(ang_r), np.sin(ang_r), -np.sin(ang_c), np.sin(ang_c)], axis=1)
    cos = np.concatenate([cos, np.ones((ctx, ATT_HEAD))], axis=0)
    sin = np.concatenate([sin, np.zeros((ctx, ATT_HEAD))], axis=0)
    return jnp.asarray(np.tile(cos, (1, 2)), F32), jnp.asarray(np.tile(sin, (1, 2)), F32)


def _rope(t, cos, sin):
    n = t.shape[1]
    q = ATT_HEAD // 4
    lane = lax.broadcasted_iota(jnp.int32, t.shape, 1)
    swapped = jnp.where((lane % (2 * q)) < q, pltpu.roll(t, n - q, 1), pltpu.roll(t, q, 1))
    return t * cos + swapped * sin


def _attn_prep_kernel(pd_ref, cos_ref, sin_ref, qg_ref, kg_ref, seg_ref, q_o, k_o, v_o):
    gw = GROUP_W
    kvw = ATT_KV_HEADS * ATT_HEAD
    seg = seg_ref[...]
    inv = 1.0 / ATT_HEAD
    cos, sin = cos_ref[...], sin_ref[...]
    q = pd_ref[:, 0:gw]
    q = q * lax.rsqrt(_segsum(q * q, seg) * inv + EPS) * qg_ref[...]
    q = _rope(q, jnp.concatenate([cos] * (gw // kvw), axis=1), jnp.concatenate([sin] * (gw // kvw), axis=1))
    q_o[...] = (q * (ATT_HEAD ** -0.5)).astype(q_o.dtype)
    k = pd_ref[:, gw:gw + kvw]
    k = k * lax.rsqrt(_segsum(k * k, seg[:kvw, :kvw]) * inv + EPS) * kg_ref[...]
    k = _rope(k, cos, sin)
    v = pd_ref[:, gw + kvw:gw + 2 * kvw]
    lane = lax.broadcasted_iota(jnp.int32, k.shape, 1)

    def variants(t):
        h0 = jnp.where(lane < ATT_HEAD, t, 0.0)
        h1 = jnp.where(lane >= ATT_HEAD, t, 0.0)
        return jnp.concatenate([h0, pltpu.roll(h0, ATT_HEAD, 1), pltpu.roll(h1, ATT_HEAD, 1), h1], axis=1)

    k_o[...] = variants(k).astype(k_o.dtype)
    v_o[...] = variants(v).astype(v_o.dtype)


def attention_prepare(pd, seq, q_gain, k_gain):
    n_batch, lc, c = pd.shape
    gw = GROUP_W
    kvw = ATT_KV_HEADS * ATT_HEAD
    cos, sin = _rope_tables(seq, lc - seq)
    seg = _seg_matrix(gw, ATT_HEAD)
    qg = jnp.tile(q_gain, ATT_Q_HEADS).reshape(1, gw)
    kg = jnp.tile(k_gain, ATT_KV_HEADS).reshape(1, kvw)
    const = lambda a: pl.BlockSpec(a.shape, lambda b, i: (0,) * a.ndim)
    tab = pl.BlockSpec((TM, kvw), lambda b, i: (i, 0))
    row = lambda n: pl.BlockSpec((None, TM, n), lambda b, i: (b, i, 0))
    return pl.pallas_call(
        _attn_prep_kernel, name="attn_prep",
        out_shape=[jax.ShapeDtypeStruct((n_batch, lc, gw), BF16)] * 3,
        grid=(n_batch, lc // TM),
        in_specs=[row(c), tab, tab, const(qg), const(kg), const(seg)],
        out_specs=[row(gw)] * 3,
        compiler_params=_cparams(("parallel", "arbitrary")),
    )(pd, cos, sin, qg, kg, seg)


def _attn_kernel(nb, sink_ref, q_ref, kp_ref, kc_ref, kn_ref, kx_ref, vp_ref, vc_ref, vn_ref, vx_ref, o_ref):
    i = pl.program_id(1)
    blk = q_ref.shape[0]
    is_lat = i < nb
    rowi = lax.broadcasted_iota(jnp.int32, (blk, blk), 0)
    coli = lax.broadcasted_iota(jnp.int32, (blk, blk), 1)
    m_prev = (coli >= rowi) & is_lat & (i >= 1)
    m_cur = jnp.broadcast_to(is_lat, (blk, blk))
    m_next = (coli <= rowi) & (i < nb - 1)
    def cols(h):
        g = h // (ATT_Q_HEADS // ATT_KV_HEADS)
        return slice((2 * g + h % 2) * LANES, (2 * g + h % 2 + 1) * LANES)

    scores = []
    for h in range(ATT_Q_HEADS):
        q = q_ref[:, (h // 2) * LANES:(h // 2 + 1) * LANES]
        col = cols(h)
        scores.append((_dot_nt(q, kp_ref[:, col]), _dot_nt(q, kc_ref[:, col]), _dot_nt(q, kn_ref[:, col]),
                       _dot_nt(q, kx_ref[:, col])))
    acc = None
    for h in range(ATT_Q_HEADS):
        col = cols(h)
        sink = sink_ref[h]
        s1 = jnp.where(m_prev, scores[h][0], NEG_INF)
        s2 = jnp.where(m_cur, scores[h][1], NEG_INF)
        s3 = jnp.where(m_next, scores[h][2], NEG_INF)
        sx = scores[h][3]
        m = jnp.maximum(jnp.maximum(jnp.max(jnp.maximum(jnp.maximum(s1, s2), s3), axis=-1, keepdims=True),
                                    jnp.max(sx, axis=-1, keepdims=True)), sink)
        p1, p2, p3, px = jnp.exp(s1 - m), jnp.exp(s2 - m), jnp.exp(s3 - m), jnp.exp(sx - m)
        den = (jnp.sum(p1 + p2 + p3, axis=-1, keepdims=True) + jnp.sum(px, axis=-1, keepdims=True)
               + jnp.exp(sink - m))
        o = (_dot(p1.astype(BF16), vp_ref[:, col]) + _dot(p2.astype(BF16), vc_ref[:, col])
             + _dot(p3.astype(BF16), vn_ref[:, col]) + _dot(px.astype(BF16), vx_ref[:, col])) / den
        acc = o if h % 2 == 0 else acc + o
        if h % 2 == 1:
            o_ref[:, (h // 2) * LANES:(h // 2 + 1) * LANES] = acc.astype(o_ref.dtype)


def window_attention(q, kv, vv, sink, seq):
    n_batch, lc, gw = q.shape
    blk = ATT_BLOCK
    nb = seq // blk
    ctx = lc - seq
    wide = kv.shape[-1]
    qspec = pl.BlockSpec((None, blk, gw), lambda b, i, s: (b, i, 0))
    prev = pl.BlockSpec((None, blk, wide), lambda b, i, s: (b, jnp.clip(i - 1, 0, nb - 1), 0))
    cur = pl.BlockSpec((None, blk, wide), lambda b, i, s: (b, jnp.minimum(i, nb - 1), 0))
    nxt = pl.BlockSpec((None, blk, wide), lambda b, i, s: (b, jnp.clip(i + 1, 0, nb - 1), 0))
    cx = pl.BlockSpec((None, ctx, wide), lambda b, i, s: (b, seq // ctx, 0))
    return pl.pallas_call(
        functools.partial(_attn_kernel, nb), name="window_attn",
        out_shape=jax.ShapeDtypeStruct((n_batch, lc, gw), BF16),
        grid_spec=pltpu.PrefetchScalarGridSpec(
            num_scalar_prefetch=1, grid=(n_batch, lc // blk),
            in_specs=[qspec, prev, cur, nxt, cx, prev, cur, nxt, cx],
            out_specs=qspec),
        compiler_params=_cparams(("parallel", "arbitrary")),
    )(sink.astype(F32), q, kv, kv, kv, kv, vv, vv, vv, vv)


def _top2_rows(val, iota):
    m1 = jnp.max(val, axis=0, keepdims=True)
    i1 = jnp.min(jnp.where(val == m1, iota, SUBLANES), axis=0, keepdims=True)
    rest = jnp.where(iota == i1, -jnp.inf, val)
    m2 = jnp.max(rest, axis=0, keepdims=True)
    i2 = jnp.min(jnp.where(rest == m2, iota, SUBLANES), axis=0, keepdims=True)
    return m1, m2, i1, i2


def _outproj_kernel(ya_ref, yb_ref, yc_ref, yd_ref, x_ref, mod_ref, g_ref, wo_ref, wr_ref, rb_ref,
                    xo_ref, h_ref, e_ref, gw_ref):
    gw = GROUP_W
    tm = x_ref.shape[0] // OUTPROJ_SPLIT
    parts = [slice(part * tm, (part + 1) * tm) for part in range(OUTPROJ_SPLIT)]
    accs = []
    for rows in parts:
        acc = None
        for n, y_ref in enumerate((ya_ref, yb_ref, yc_ref, yd_ref)):
            part = _dot(y_ref[rows, :].astype(BF16), wo_ref[n * gw:(n + 1) * gw, :])
            acc = part if acc is None else acc + part
        accs.append(acc)
    for rows, acc in zip(parts, accs):
        _outproj_rows(rows, acc, x_ref, mod_ref, g_ref, wr_ref, rb_ref, xo_ref, h_ref, e_ref, gw_ref)


def _outproj_rows(rows, acc, x_ref, mod_ref, g_ref, wr_ref, rb_ref, xo_ref, h_ref, e_ref, gw_ref):
    d = x_ref.shape[-1]
    x = x_ref[rows, :] + mod_ref[:, 2 * d:3 * d] * acc
    xo_ref[rows, :] = x
    y = x * lax.rsqrt(jnp.mean(x * x, axis=-1, keepdims=True) + EPS) * g_ref[...]
    h = y * (1.0 + mod_ref[:, 4 * d:5 * d]) + mod_ref[:, 3 * d:4 * d]
    h_ref[rows, :] = _pack_bf16_halves(h)
    hh, hl = _split(h)
    wh, wl = _split(wr_ref[...])
    logits = _dot_nt(wh, hh) + _dot_nt(wh, hl) + _dot_nt(wl, hh)
    scores = _sigmoid(logits)
    sel = scores + rb_ref[:, rows]
    tm = x.shape[0]
    iota = lax.broadcasted_iota(jnp.int32, (EXPERTS_PER_GROUP, tm), 0)
    best_val, best = None, None
    for grp in range(N_EXPERT_GROUPS):
        m1, m2, _, _ = _top2_rows(sel[grp * EXPERTS_PER_GROUP:(grp + 1) * EXPERTS_PER_GROUP], iota)
        gs = m1 + m2
        if grp == 0:
            best_val, best = gs, jnp.zeros_like(gs, dtype=jnp.int32)
        else:
            upd = gs > best_val
            best_val = jnp.where(upd, gs, best_val)
            best = jnp.where(upd, grp, best)
    sel_in = jnp.zeros((EXPERTS_PER_GROUP, tm), F32)
    sc_in = jnp.zeros((EXPERTS_PER_GROUP, tm), F32)
    for grp in range(N_EXPERT_GROUPS):
        members = slice(grp * EXPERTS_PER_GROUP, (grp + 1) * EXPERTS_PER_GROUP)
        hit = best == grp
        sel_in = jnp.where(hit, sel[members], sel_in)
        sc_in = jnp.where(hit, scores[members], sc_in)
    _, _, i1, i2 = _top2_rows(sel_in, iota)
    w1 = jnp.sum(jnp.where(iota == i1, sc_in, 0.0), axis=0, keepdims=True)
    w2 = jnp.sum(jnp.where(iota == i2, sc_in, 0.0), axis=0, keepdims=True)
    tot = w1 + w2
    e_ref[0:1, rows] = best * EXPERTS_PER_GROUP + i1
    e_ref[1:2, rows] = best * EXPERTS_PER_GROUP + i2
    gw_ref[0:1, rows] = w1 / tot
    gw_ref[1:2, rows] = w2 / tot


def out_projection_router(ys, stream, mod, gain2, w_out, w_router, router_bias, seq):
    n_batch, lc, d = stream.shape
    gw = GROUP_W
    ctx_tile = seq // TM
    wr_t = w_router.T
    rb = jnp.broadcast_to(router_bias.astype(F32)[:, None], (N_EXPERTS, TM))
    row = lambda n: pl.BlockSpec((None, TM, n), lambda b, i: (b, i, 0))
    const = lambda a: pl.BlockSpec(a.shape, lambda b, i: (0,) * a.ndim)
    lane_row = pl.BlockSpec((None, TOP_K, TM), lambda b, i: (b, 0, i))
    return pl.pallas_call(
        _outproj_kernel, name="out_proj_router",
        out_shape=[jax.ShapeDtypeStruct((n_batch, lc, d), F32), jax.ShapeDtypeStruct((n_batch, lc, d // 2), jnp.uint32),
                   jax.ShapeDtypeStruct((n_batch, TOP_K, lc), jnp.int32),
                   jax.ShapeDtypeStruct((n_batch, TOP_K, lc), F32)],
        grid=(n_batch, lc // TM),
        in_specs=[row(gw)] * 4 + [row(d), pl.BlockSpec((None, 1, mod.shape[-1]), _mod_index(n_batch, ctx_tile)),
                                  pl.BlockSpec((1, d), lambda b, i: (0, 0)),
                                  pl.BlockSpec(w_out.shape, lambda b, i: (0, 0), pipeline_mode=pl.Buffered(1)),
                                  const(wr_t), const(rb)],
        out_specs=[row(d), row(d // 2), lane_row, lane_row],
        compiler_params=_cparams(("parallel", "arbitrary")),
    )(*ys, stream, mod, gain2.reshape(1, d), w_out, wr_t, rb)


def _expert_kernel(be_ref, nu_ref, x_ref, wg_ref, wu_ref, wd_ref, o_ref, wg_s, wu_s, wd_s):
    i = pl.program_id(0)
    prev = be_ref[jnp.maximum(i - 1, 0)]
    used = i < nu_ref[0]

    @pl.when(used & ((i == 0) | (be_ref[i] != prev)))
    def _():
        wg_s[...] = wg_ref[...].astype(BF16)
        wu_s[...] = wu_ref[...].astype(BF16)
        wd_s[...] = wd_ref[...].astype(BF16)

    @pl.when(used)
    def _():
        x_lo, x_hi = _unpack_bf16_halves(x_ref[...])
        n = x_lo.shape[1]
        a = _dot(x_lo, wg_s[:n, :]) + _dot(x_hi, wg_s[n:, :])
        u = _dot(x_lo, wu_s[:n, :]) + _dot(x_hi, wu_s[n:, :])
        hb = (a * _sigmoid(a)) * u
        o_ref[...] = _dot(hb.astype(BF16), wd_s[...])

    @pl.when(jnp.logical_not(used))
    def _():
        o_ref[...] = jnp.zeros_like(o_ref)


def expert_ffn(xs, block_e, n_used, wg, wu, wd, layer):
    rows, packed = xs.shape
    n_blocks = rows // MOE_BLOCK
    d, de = wg.shape[-2:]
    return pl.pallas_call(
        _expert_kernel, name="moe_experts",
        out_shape=jax.ShapeDtypeStruct((rows, d), F32),
        grid_spec=pltpu.PrefetchScalarGridSpec(
            num_scalar_prefetch=2, grid=(n_blocks,),
            in_specs=[pl.BlockSpec((MOE_BLOCK, packed), lambda i, be, nu: (i, 0)),
                      pl.BlockSpec((None, None, d, de), lambda i, be, nu: (layer, be[i], 0, 0)),
                      pl.BlockSpec((None, None, d, de), lambda i, be, nu: (layer, be[i], 0, 0)),
                      pl.BlockSpec((None, None, de, d), lambda i, be, nu: (layer, be[i], 0, 0))],
            out_specs=pl.BlockSpec((MOE_BLOCK, d), lambda i, be, nu: (i, 0)),
            scratch_shapes=[pltpu.VMEM((d, de), BF16), pltpu.VMEM((d, de), BF16), pltpu.VMEM((de, d), BF16)]),
        compiler_params=_cparams(("arbitrary",)),
    )(block_e, n_used, xs, wg, wu, wd)


RANK_WIDTHS = (1280, 1024, 768, 512, 256, 128)


def _rank_kernel(e_ref, tri_ref, rank_ref, count_ref, carry_ref):
    first = (pl.program_id(0) == 0) & (pl.program_id(1) == 0)

    @pl.when(first)
    def _():
        carry_ref[...] = jnp.zeros_like(carry_ref)

    width = e_ref.shape[-1]
    expert = lax.broadcasted_iota(jnp.int32, (N_EXPERTS, width), 0)
    carry = carry_ref[...]
    for k in range(TOP_K):
        onehot = jnp.where(expert == e_ref[k:k + 1, :], 1.0, 0.0)
        before = _dot(onehot.astype(BF16), tri_ref[...])
        rank_ref[k:k + 1, :] = jnp.sum(onehot * (before + carry), axis=0, keepdims=True).astype(jnp.int32)
        carry = carry + jnp.sum(onehot, axis=1, keepdims=True)
    carry_ref[...] = carry
    count_ref[...] = jnp.broadcast_to(carry, count_ref.shape)


def expert_ranks(eidx):
    n_batch, top_k, lc = eidx.shape
    width = next(w for w in RANK_WIDTHS if lc % w == 0)
    tri = jnp.asarray(np.triu(np.ones((width, width), np.float32), 1), BF16)
    blk = pl.BlockSpec((None, top_k, width), lambda b, i: (b, 0, i))
    rank, counts = pl.pallas_call(
        _rank_kernel, name="moe_rank",
        out_shape=[jax.ShapeDtypeStruct((n_batch, top_k, lc), jnp.int32),
                   jax.ShapeDtypeStruct((N_EXPERTS, LANES), F32)],
        grid=(n_batch, lc // width),
        in_specs=[blk, pl.BlockSpec(tri.shape, lambda b, i: (0, 0))],
        out_specs=[blk, pl.BlockSpec((N_EXPERTS, LANES), lambda b, i: (0, 0))],
        scratch_shapes=[pltpu.VMEM((N_EXPERTS, 1), F32)],
        compiler_params=_cparams(("arbitrary", "arbitrary")),
    )(eidx, tri)
    return rank, counts[:, 0].astype(jnp.int32)


def _combine_kernel(x_ref, y0_ref, y1_ref, g0_ref, g1_ref, mod_ref, o_ref):
    d = x_ref.shape[-1]
    y = y0_ref[...] * g0_ref[...] + y1_ref[...] * g1_ref[...]
    o_ref[...] = x_ref[...] + mod_ref[:, 5 * d:6 * d] * y


def moe_combine(stream, y0, y1, g0, g1, mod, seq, n_rows):
    n_batch, lc, d = stream.shape
    row = pl.BlockSpec((None, TM, d), lambda b, i: (b, i, 0))
    col = pl.BlockSpec((None, TM, 1), lambda b, i: (b, i, 0))
    return pl.pallas_call(
        _combine_kernel, name="moe_combine",
        out_shape=jax.ShapeDtypeStruct((n_batch, n_rows, d), F32),
        grid=(n_batch, n_rows // TM),
        in_specs=[row, row, row, col, col, pl.BlockSpec((None, 1, mod.shape[-1]), _mod_index(n_batch, seq // TM))],
        out_specs=row,
        compiler_params=_cparams(("parallel", "arbitrary")),
    )(stream, y0, y1, g0, g1, mod)


def moe_ffn(stream, h2, eidx, gates, mod, wg, wu, wd, layer, seq, n_rows):
    n_batch, lc, d = stream.shape
    n_tok = n_batch * lc
    n_assign = n_tok * TOP_K
    rank, counts = expert_ranks(eidx)
    padded = (counts + MOE_BLOCK - 1) // MOE_BLOCK * MOE_BLOCK
    pad_end = jnp.cumsum(padded)
    experts = jnp.arange(N_EXPERTS, dtype=jnp.int32)
    start = jnp.sum(jnp.where(eidx[..., None] == experts, pad_end - padded, 0), axis=-1)
    dest = start + rank
    n_blocks = n_assign // MOE_BLOCK + N_EXPERTS
    tok = jnp.broadcast_to((jnp.arange(n_batch, dtype=jnp.int32) * lc)[:, None, None]
                           + jnp.arange(lc, dtype=jnp.int32)[None, None, :], dest.shape)
    filler = jnp.arange(n_blocks * MOE_BLOCK, dtype=jnp.int32) % n_tok
    row_tok = filler.at[dest.reshape(-1)].set(tok.reshape(-1))
    block_row = jnp.arange(n_blocks, dtype=jnp.int32) * MOE_BLOCK
    block_e = jnp.minimum(jnp.sum((pad_end[None, :] <= block_row[:, None]).astype(jnp.int32), axis=1), N_EXPERTS - 1)
    n_used = (pad_end[-1:] // MOE_BLOCK).astype(jnp.int32)
    xs = h2.reshape(n_tok, h2.shape[-1])[row_tok]
    ys = expert_ffn(xs, block_e, n_used, wg, wu, wd, layer)
    y0 = ys[dest[:, 0].reshape(-1)].reshape(n_batch, lc, d)
    y1 = ys[dest[:, 1].reshape(-1)].reshape(n_batch, lc, d)
    g0 = gates[:, 0].reshape(n_batch, lc, 1)
    g1 = gates[:, 1].reshape(n_batch, lc, 1)
    return moe_combine(stream, y0, y1, g0, g1, mod, seq, n_rows)


def kernel(x, c, ctx, c_ctx, norm1, norm2, w_ada, b_ada, w_in, w_out, rwkv_conv, rwkv_w0, rwkv_w_up, rwkv_a0,
           rwkv_a_up, rwkv_g_up, rwkv_k_k, rwkv_k_a, rwkv_r_k, rwkv_ln_w, rwkv_ln_b, rwkv_v0, rwkv_v_down,
           rwkv_v_up, fourier_w, fourier_b, gmlp_norm, gmlp_ws, gmlp_bs, attn_q_norm, attn_k_norm, attn_sink,
           w_router, router_bias, w_e_gate, w_e_up, w_e_down):
    n_batch, seq, d = x.shape
    depth = w_in.shape[0]
    stream = jnp.concatenate([x, ctx], axis=1)
    c_all = jnp.zeros((SUBLANES, d), F32).at[:n_batch].set(c).at[n_batch].set(c_ctx)
    col_splits = np.cumsum([A_COLS, B_COLS, C_COLS])
    v_first = None
    for l in range(depth):
        mod = ada_modulation(c_all, w_ada, b_ada, l).reshape(SUBLANES, 1, 6 * d)
        w_parts = [w.astype(BF16) for w in jnp.split(w_in[l], col_splits, axis=1)]
        pa, pb, pc, pd = in_projection(stream, mod, norm1[l], w_parts, seq)

        vres = None if l == 0 else (rwkv_v0[l - 1], rwkv_v_down[l - 1], rwkv_v_up[l - 1])
        r, kk, v, k0, wd0, kka0, k1, wd1, kka1, g = rwkv_prepare(
            pa, seq, rwkv_conv[l], rwkv_w0[l], rwkv_w_up[l], rwkv_a0[l], rwkv_a_up[l], rwkv_g_up[l],
            rwkv_k_k[l], rwkv_k_a[l], v_first, vres)
        if l == 0:
            v_first = v
        yf, yb = rwkv_scan(r, kk, v, ((k0, wd0, kka0), (k1, wd1, kka1)), seq)
        y_a = rwkv_readout(yf, yb, r, k0, k1, v, g, rwkv_r_k[l].reshape(-1), rwkv_ln_w[l], rwkv_ln_b[l])

        y_b = fourier_mix(pb, seq, fourier_w[l], fourier_b[l])
        y_c = gmlp_mix(pc, gmlp_norm[l], gmlp_ws[l], gmlp_bs[l])
        q, kv, vv = attention_prepare(pd, seq, attn_q_norm[l], attn_k_norm[l])
        y_d = window_attention(q, kv, vv, attn_sink[l], seq)

        stream, h2, eidx, gates = out_projection_router(
            (y_a, y_b, y_c, y_d), stream, mod, norm2[l], w_out[l].astype(BF16), w_router, router_bias, seq)
        n_rows = seq if l == depth - 1 else stream.shape[1]
        stream = moe_ffn(stream, h2, eidx, gates, mod, w_e_gate, w_e_up, w_e_down, l, seq, n_rows)
    return stream
```
